```python
import jax, jax.numpy as jnp
from jax import lax
import numpy as np

D_MODEL = 1024
BATCH = 2
SEQ = 8192
DEPTH = 4
DEC_BATCH = 32
DEC_SEQ = 8
PAST_LEN = 8192
PAGE_SIZE = 128

N_MIXERS = 2
N_RWKV = (DEPTH + 1) // 2
N_NSA = DEPTH // 2
EXPAND = 2
INNER = EXPAND * D_MODEL
NORM_EPS = 1e-6
RW_HEAD = 64
RW_HEADS = INNER // RW_HEAD
DECAY_LORA = 64
AAA_LORA = 64
MV_LORA = 32
LNX_EPS = 64e-5
HEAD_DIM = 64
N_HEADS = INNER // HEAD_DIM
N_KV = 4
HPG = N_HEADS // N_KV
KV_W = N_KV * HEAD_DIM
ROT_DIM = HEAD_DIM // 4
ROPE_THETA = 500000.0
CMP_BLOCK = 32
CMP_STRIDE = 16
CMP_RATIO = CMP_BLOCK // CMP_STRIDE
CMP_HID = 2 * HEAD_DIM
SEL_BLOCK = 64
TOP_N = 16
WINDOW = 512
Q_BLOCK = 128
NEG = -1e30
FORCE_BONUS = 1e4
Q_END = INNER
KV_END = Q_END + 6 * KV_W
G_END = KV_END + 3 * N_HEADS
NSA_IN = G_END + INNER

kernel_name = "rwkv7_nsa_hybrid_step"


def rms_norm(x, g):
    xf = x.astype(jnp.float32)
    y = xf * lax.rsqrt(jnp.mean(xf * xf, -1, keepdims=True) + NORM_EPS)
    return (y * g.astype(jnp.float32)).astype(x.dtype)


def rope(x, pos):
    half = ROT_DIM // 2
    inv = ROPE_THETA ** (-jnp.arange(half, dtype=jnp.float32) / half)
    ang = pos.astype(jnp.float32)[:, None] * inv[None, :]
    cos = jnp.cos(ang)[None, :, None, :].astype(x.dtype)
    sin = jnp.sin(ang)[None, :, None, :].astype(x.dtype)
    x1, x2, rest = x[..., :half], x[..., half:ROT_DIM], x[..., ROT_DIM:]
    return jnp.concatenate([x1 * cos - x2 * sin, x1 * sin + x2 * cos, rest], -1)


def rwkv7_mixer(x, shift_prev, s0, v_first, mu, w_in, w_out, w0, w1, w2, a0, a1, a2,
                k_k, k_a, r_k, lnx_w, lnx_b, vres):
    f32 = jnp.float32
    B, T, _ = x.shape
    x_prev = jnp.concatenate([shift_prev[:, None, :].astype(x.dtype), x[:, :-1]], axis=1)
    mix = x[None] + (x_prev - x)[None] * mu[:, None, None, :]
    r, k, v, z = jnp.einsum('pbtd,pde->pbte', mix[:4], w_in)
    w_log = -jax.nn.softplus(-(w0 + jnp.tanh(mix[4] @ w1) @ w2).astype(f32)) - 0.5
    a = jax.nn.sigmoid((a0 + (mix[5] @ a1) @ a2).astype(f32))
    if vres is None:
        v_first = v
    else:
        v0, v1, v2 = vres
        v = v + (v_first - v) * jax.nn.sigmoid(v0 + (mix[2] @ v1) @ v2)
    heads = lambda t: t.astype(f32).reshape(B, T, RW_HEADS, RW_HEAD)
    kk = heads(k * k_k)
    kk = kk * lax.rsqrt(jnp.maximum(jnp.sum(kk * kk, -1, keepdims=True), 1e-24))
    k = k.astype(f32) * (1.0 + (a - 1.0) * k_a)
    r_h, k_h, v_h, a_h = heads(r), heads(k), heads(v), heads(a)
    decay = heads(jnp.exp(-jnp.exp(w_log)))

    def step(S, inp):
        r_t, d_t, k_t, v_t, kk_t, a_t = inp
        sa = jnp.einsum('bhvk,bhk->bhv', S, -kk_t)
        S = (S * d_t[:, :, None, :] + sa[..., None] * (kk_t * a_t)[:, :, None, :]
             + v_t[..., None] * k_t[:, :, None, :])
        return S, jnp.einsum('bhvk,bhk->bhv', S, r_t)

    tm = lambda t: jnp.swapaxes(t, 0, 1)
    s_T, y = lax.scan(step, s0.astype(f32), tuple(tm(t) for t in (r_h, decay, k_h, v_h, kk, a_h)))
    y = tm(y)
    mean = jnp.mean(y, -1, keepdims=True)
    var = jnp.mean(jnp.square(y - mean), -1, keepdims=True)
    y = ((y - mean) * lax.rsqrt(var + LNX_EPS)).reshape(B, T, INNER) * lnx_w + lnx_b
    y = y + (jnp.sum(r_h * k_h * r_k, -1, keepdims=True) * v_h).reshape(B, T, INNER)
    out = (y * jax.nn.silu(z.astype(f32))).astype(x.dtype) @ w_out
    return out, v_first, s_T.astype(s0.dtype), x[:, -1]


def nsa_project(x, w_in, pos):
    B, T, _ = x.shape
    p = x @ w_in
    q = p[..., :Q_END].reshape(B, T, N_HEADS, HEAD_DIM)
    kv = p[..., Q_END:KV_END].reshape(B, T, 6, N_KV, HEAD_DIM)
    gates = jax.nn.sigmoid(p[..., KV_END:G_END].astype(jnp.float32)).reshape(B, T, N_HEADS, 3)
    z = p[..., G_END:]
    kc, vc, ks, vs, kw, vw = (kv[:, :, i] for i in range(6))
    return q, rope(q, pos), gates, z, kc, vc, rope(ks, pos), vs, rope(kw, pos), vw


def compress(k, v, pe, w1, w2):
    def phi(t, a, b):
        B, T = t.shape[:2]
        n_chunk = T // CMP_STRIDE
        nc = n_chunk - CMP_RATIO + 1
        chunks = t[:, :n_chunk * CMP_STRIDE].reshape(B, n_chunk, CMP_STRIDE, N_KV, HEAD_DIM)
        parts = []
        for m in range(CMP_RATIO):
            sl = slice(m * CMP_STRIDE, (m + 1) * CMP_STRIDE)
            part = jnp.einsum('bcsgd,sdh->bcgh', chunks + pe[None, None, sl, None, :], a[sl])
            parts.append(part[:, m:m + nc])
        h = jax.nn.silu(sum(parts))
        return jnp.einsum('bngh,hd->bngd', h, b)
    ck = phi(k, w1[0], w2[0])
    cv = phi(v, w1[1], w2[1])
    c_end = jnp.arange(ck.shape[1]) * CMP_STRIDE + CMP_BLOCK - 1
    return ck, cv, c_end


def to_sel_blocks(t):
    B, T = t.shape[:2]
    ns = -(-T // SEL_BLOCK)
    t = jnp.pad(t, ((0, 0), (0, ns * SEL_BLOCK - T), (0, 0), (0, 0)))
    return t.reshape(B, ns, SEL_BLOCK, N_KV, HEAD_DIM).transpose(0, 3, 1, 2, 4)


def nsa_attend(q, qr, gates, q_pos, ck, cv, c_end, sk_b, sv_b, wk, wv, w_pos):
    f32 = jnp.float32
    B, Q = q.shape[:2]
    scale = HEAD_DIM ** -0.5
    qg = q.reshape(B, Q, N_KV, HPG, HEAD_DIM)
    qrg = qr.reshape(B, Q, N_KV, HPG, HEAD_DIM)
    s_c = jnp.einsum('bqghd,bngd->bqghn', qg, ck).astype(f32) * scale
    m_c = (c_end[None, :] <= q_pos[:, None])[None, :, None, None, :]
    p_c = jnp.where(m_c, jax.nn.softmax(jnp.where(m_c, s_c, NEG), -1), 0.0)
    o_c = jnp.einsum('bqghn,bngd->bqghd', p_c.astype(cv.dtype), cv)
    nc, ns = ck.shape[1], sk_b.shape[2]
    cs = jnp.arange(nc) * CMP_STRIDE
    ss = jnp.arange(ns) * SEL_BLOCK
    overlap = jnp.clip(jnp.minimum(cs[:, None] + CMP_BLOCK, ss[None, :] + SEL_BLOCK)
                       - jnp.maximum(cs[:, None], ss[None, :]), 0, None).astype(f32) / CMP_BLOCK
    imp = jnp.einsum('bqghn,nj->bqgj', p_c, overlap)
    cur = q_pos // SEL_BLOCK
    jb = jnp.arange(ns)
    valid = jb[None, :] <= cur[:, None]
    forced = (jb[None, :] == 0) | (jb[None, :] == cur[:, None]) | (jb[None, :] == cur[:, None] - 1)
    score = jnp.where(valid[None, :, None, :], imp + FORCE_BONUS * forced[None, :, None, :].astype(f32), NEG)
    n_top = min(TOP_N, ns)
    top_v, top_i = lax.top_k(score, n_top)
    ok = top_v > NEG / 2
    bi = jnp.arange(B)[:, None, None, None]
    gi = jnp.arange(N_KV)[None, None, :, None]
    gk = sk_b[bi, gi, top_i]
    gv = sv_b[bi, gi, top_i]
    kpos = top_i[..., None] * SEL_BLOCK + jnp.arange(SEL_BLOCK)
    m_s = (ok[..., None] & (kpos <= q_pos[None, :, None, None, None]))[:, :, :, None]
    s_s = jnp.einsum('bqghd,bqgnsd->bqghns', qrg, gk).astype(f32) * scale
    s_s = jnp.where(m_s, s_s, NEG).reshape(B, Q, N_KV, HPG, n_top * SEL_BLOCK)
    p_s = jax.nn.softmax(s_s, -1).reshape(B, Q, N_KV, HPG, n_top, SEL_BLOCK)
    o_s = jnp.einsum('bqghns,bqgnsd->bqghd', p_s.astype(gv.dtype), gv)
    s_w = jnp.einsum('bqghd,bwgd->bqghw', qrg, wk).astype(f32) * scale
    m_w = ((w_pos[None, :] <= q_pos[:, None]) & (w_pos[None, :] >= q_pos[:, None] - WINDOW)
           & (w_pos[None, :] >= 0))[None, :, None, None, :]
    p_w = jax.nn.softmax(jnp.where(m_w, s_w, NEG), -1)
    o_w = jnp.einsum('bqghw,bwgd->bqghd', p_w.astype(wv.dtype), wv)
    g = gates.reshape(B, Q, N_KV, HPG, 3)
    o = g[..., 0:1] * o_c + g[..., 1:2] * o_s + g[..., 2:3] * o_w
    return o.reshape(B, Q, INNER)


def nsa_prompt(x, w_in, w_out, pe, cw1, cw2):
    B, T, _ = x.shape
    pos = jnp.arange(T)
    q, qr, gates, z, kc, vc, ks, vs, kw, vw = nsa_project(x, w_in, pos)
    ck, cv, c_end = compress(kc, vc, pe, cw1, cw2)
    sk_b, sv_b = to_sel_blocks(ks), to_sel_blocks(vs)
    pad = ((0, 0), (WINDOW, 0), (0, 0), (0, 0))
    kw_pad, vw_pad = jnp.pad(kw, pad), jnp.pad(vw, pad)
    nb = T // Q_BLOCK
    blocks = lambda t: jnp.moveaxis(t.reshape(B, nb, Q_BLOCK, *t.shape[2:]), 1, 0)

    def one_block(args):
        qi, qri, gi, start = args
        q_pos = start + jnp.arange(Q_BLOCK)
        wk = lax.dynamic_slice_in_dim(kw_pad, start, WINDOW + Q_BLOCK, axis=1)
        wv = lax.dynamic_slice_in_dim(vw_pad, start, WINDOW + Q_BLOCK, axis=1)
        w_pos = start - WINDOW + jnp.arange(WINDOW + Q_BLOCK)
        return nsa_attend(qi, qri, gi, q_pos, ck, cv, c_end, sk_b, sv_b, wk, wv, w_pos)

    o = lax.map(one_block, (blocks(q), blocks(qr), blocks(gates), jnp.arange(nb) * Q_BLOCK))
    o = jnp.moveaxis(o, 0, 1).reshape(B, T, INNER)
    y = (o * jax.nn.silu(z.astype(jnp.float32))).astype(x.dtype) @ w_out
    n_keep = min(WINDOW, T)
    return (y, jnp.stack([kc, vc], 2), jnp.stack([ks, vs], 2),
            jnp.stack([kw[:, -n_keep:], vw[:, -n_keep:]], 2))


def nsa_sample(x, pool_cmp, pool_sel, win_buf, page_table, w_in, w_out, pe, cw1, cw2):
    B, T, _ = x.shape
    past = page_table.shape[1] * pool_cmp.shape[1]
    pos = past + jnp.arange(T)
    q, qr, gates, z, kc, vc, ks, vs, kw, vw = nsa_project(x, w_in, pos)

    def gather(pool):
        return pool[page_table].reshape(B, past, 2, N_KV, HEAD_DIM).astype(x.dtype)

    pc, ps = gather(pool_cmp), gather(pool_sel)
    ck, cv, c_end = compress(jnp.concatenate([pc[:, :, 0], kc], 1),
                             jnp.concatenate([pc[:, :, 1], vc], 1), pe, cw1, cw2)
    sk_b = to_sel_blocks(jnp.concatenate([ps[:, :, 0], ks], 1))
    sv_b = to_sel_blocks(jnp.concatenate([ps[:, :, 1], vs], 1))
    n_buf = win_buf.shape[1]
    wk = jnp.concatenate([win_buf[:, :, 0].astype(x.dtype), kw], 1)
    wv = jnp.concatenate([win_buf[:, :, 1].astype(x.dtype), vw], 1)
    w_pos = past - n_buf + jnp.arange(n_buf + T)
    o = nsa_attend(q, qr, gates, pos, ck, cv, c_end, sk_b, sv_b, wk, wv, w_pos)
    y = (o * jax.nn.silu(z.astype(jnp.float32))).astype(x.dtype) @ w_out
    n_keep = min(WINDOW, n_buf + T)
    return (y, jnp.stack([kc, vc], 2), jnp.stack([ks, vs], 2),
            jnp.stack([wk[:, -n_keep:], wv[:, -n_keep:]], 2))


def setup_inputs(seed: int = 0) -> dict:
    key = jax.random.key(seed)
    keys = jax.random.split(key, 40)
    cnt = iter(range(40))

    def nrm(shape, scale):
        return jax.random.normal(keys[next(cnt)], shape, jnp.float32) * scale

    def unif(shape, lo, hi):
        return jax.random.uniform(keys[next(cnt)], shape, jnp.float32, lo, hi)

    n_pages = PAST_LEN // PAGE_SIZE
    n_pool = (5 * DEC_BATCH * n_pages + 3) // 4
    win_len = min(WINDOW, PAST_LEN)
    inp = {}
    inp['x_prompt'] = nrm((BATCH, SEQ, D_MODEL), 1.0)
    inp['x_sample'] = nrm((DEC_BATCH, DEC_SEQ, D_MODEL), 1.0)
    inp['state_rwkv_wkv'] = nrm((N_RWKV, DEC_BATCH, RW_HEADS, RW_HEAD, RW_HEAD), 0.3)
    inp['state_rwkv_shift'] = nrm((N_RWKV, DEC_BATCH, D_MODEL), 1.0)
    inp['cache_nsa_cmp'] = nrm((N_NSA, n_pool, PAGE_SIZE, 2, N_KV, HEAD_DIM), 1.0)
    inp['cache_nsa_sel'] = nrm((N_NSA, n_pool, PAGE_SIZE, 2, N_KV, HEAD_DIM), 1.0)
    inp['cache_nsa_win'] = nrm((N_NSA, DEC_BATCH, win_len, 2, N_KV, HEAD_DIM), 1.0)
    perm = jax.random.permutation(keys[next(cnt)], n_pool)
    inp['page_table'] = perm[:DEC_BATCH * n_pages].reshape(DEC_BATCH, n_pages).astype(jnp.int32)
    inp['norm_g'] = 1.0 + nrm((DEPTH, D_MODEL), 0.02)
    inp['final_norm_g'] = 1.0 + nrm((D_MODEL,), 0.02)
    inp['rwkv_mu'] = unif((N_RWKV, 6, D_MODEL), 0.0, 1.0)
    inp['rwkv_w_in'] = nrm((N_RWKV, 4, D_MODEL, INNER), D_MODEL ** -0.5)
    inp['rwkv_w_out'] = nrm((N_RWKV, INNER, D_MODEL), INNER ** -0.5)
    inp['rwkv_w0'] = unif((N_RWKV, INNER), -6.0, -1.0)
    inp['rwkv_w1'] = nrm((N_RWKV, D_MODEL, DECAY_LORA), D_MODEL ** -0.5)
    inp['rwkv_w2'] = nrm((N_RWKV, DECAY_LORA, INNER), 0.5 * DECAY_LORA ** -0.5)
    inp['rwkv_a0'] = nrm((N_RWKV, INNER), 0.5)
    inp['rwkv_a1'] = nrm((N_RWKV, D_MODEL, AAA_LORA), D_MODEL ** -0.5)
    inp['rwkv_a2'] = nrm((N_RWKV, AAA_LORA, INNER), AAA_LORA ** -0.5)
    inp['rwkv_v0'] = nrm((N_RWKV - 1, INNER), 0.5)
    inp['rwkv_v1'] = nrm((N_RWKV - 1, D_MODEL, MV_LORA), D_MODEL ** -0.5)
    inp['rwkv_v2'] = nrm((N_RWKV - 1, MV_LORA, INNER), MV_LORA ** -0.5)
    inp['rwkv_k_k'] = 0.85 + nrm((N_RWKV, INNER), 0.05)
    inp['rwkv_k_a'] = 1.0 + nrm((N_RWKV, INNER), 0.05)
    inp['rwkv_r_k'] = nrm((N_RWKV, RW_HEADS, RW_HEAD), 0.1)
    inp['rwkv_lnx_w'] = 1.0 + nrm((N_RWKV, INNER), 0.02)
    inp['rwkv_lnx_b'] = nrm((N_RWKV, INNER), 0.02)
    inp['nsa_w_in'] = nrm((N_NSA, D_MODEL, NSA_IN), D_MODEL ** -0.5)
    inp['nsa_w_out'] = nrm((N_NSA, INNER, D_MODEL), INNER ** -0.5)
    inp['nsa_cmp_pe'] = nrm((N_NSA, CMP_BLOCK, HEAD_DIM), 0.5)
    inp['nsa_cmp_w1'] = nrm((N_NSA, 2, CMP_BLOCK, HEAD_DIM, CMP_HID), (CMP_BLOCK * HEAD_DIM) ** -0.5)
    inp['nsa_cmp_w2'] = nrm((N_NSA, 2, CMP_HID, HEAD_DIM), CMP_HID ** -0.5)
    return inp


def reference(x_prompt, x_sample, state_rwkv_wkv, state_rwkv_shift, cache_nsa_cmp, cache_nsa_sel,
              cache_nsa_win, page_table, norm_g, final_norm_g, rwkv_mu, rwkv_w_in, rwkv_w_out,
              rwkv_w0, rwkv_w1, rwkv_w2, rwkv_a0, rwkv_a1, rwkv_a2, rwkv_v0, rwkv_v1, rwkv_v2,
              rwkv_k_k, rwkv_k_a, rwkv_r_k, rwkv_lnx_w, rwkv_lnx_b, nsa_w_in, nsa_w_out,
              nsa_cmp_pe, nsa_cmp_w1, nsa_cmp_w2):

    def trunk(x, wkv0, shift0, sample):
        h = x
        v_first = None
        wkv, shift, cmp_rows, sel_rows, win_rows = [], [], [], [], []
        for layer in range(DEPTH):
            j = layer // N_MIXERS
            xn = rms_norm(h, norm_g[layer])
            if layer % N_MIXERS == 0:
                vres = None if j == 0 else (rwkv_v0[j - 1], rwkv_v1[j - 1], rwkv_v2[j - 1])
                o, v_first, s_T, last = rwkv7_mixer(
                    xn, shift0[j], wkv0[j], v_first, rwkv_mu[j], rwkv_w_in[j], rwkv_w_out[j],
                    rwkv_w0[j], rwkv_w1[j], rwkv_w2[j], rwkv_a0[j], rwkv_a1[j], rwkv_a2[j],
                    rwkv_k_k[j], rwkv_k_a[j], rwkv_r_k[j], rwkv_lnx_w[j], rwkv_lnx_b[j], vres)
                wkv.append(s_T)
                shift.append(last)
            else:
                if sample:
                    o, c, s, w = nsa_sample(xn, cache_nsa_cmp[j], cache_nsa_sel[j], cache_nsa_win[j],
                                            page_table, nsa_w_in[j], nsa_w_out[j], nsa_cmp_pe[j],
                                            nsa_cmp_w1[j], nsa_cmp_w2[j])
                else:
                    o, c, s, w = nsa_prompt(xn, nsa_w_in[j], nsa_w_out[j], nsa_cmp_pe[j],
                                            nsa_cmp_w1[j], nsa_cmp_w2[j])
                cmp_rows.append(c)
                sel_rows.append(s)
                win_rows.append(w)
            h = h + o
        return (rms_norm(h, final_norm_g), jnp.stack(wkv), jnp.stack(shift),
                jnp.stack(cmp_rows), jnp.stack(sel_rows), jnp.stack(win_rows))

    zeros_wkv = jnp.zeros((N_RWKV, x_prompt.shape[0], RW_HEADS, RW_HEAD, RW_HEAD), x_prompt.dtype)
    zeros_shift = jnp.zeros((N_RWKV, x_prompt.shape[0], D_MODEL), x_prompt.dtype)
    y_prompt, wkv_p, shift_p, cmp_p, sel_p, win_p = trunk(x_prompt, zeros_wkv, zeros_shift, False)
    y_sample, wkv_s, shift_s, cmp_s, sel_s, win_s = trunk(x_sample, state_rwkv_wkv, state_rwkv_shift, True)
    return (y_prompt, y_sample, wkv_p, wkv_s, shift_p, shift_s, cmp_p, cmp_s, sel_p, sel_s, win_p, win_s)
```

```python
import functools

import jax
import jax.numpy as jnp
from jax import lax
from jax.experimental import pallas as pl
from jax.experimental.pallas import tpu as pltpu

D_MODEL = 1024
DEPTH = 4
N_MIXERS = 2
INNER = 2 * D_MODEL
NORM_EPS = 1e-6
RW_HEAD = 64
RW_HEADS = INNER // RW_HEAD
LNX_EPS = 64e-5
HEAD_DIM = 64
N_HEADS = INNER // HEAD_DIM
N_KV = 4
HPG = N_HEADS // N_KV
KV_W = N_KV * HEAD_DIM
ROT_DIM = HEAD_DIM // 4
ROPE_THETA = 500000.0
CMP_BLOCK = 32
CMP_STRIDE = 16
CMP_RATIO = CMP_BLOCK // CMP_STRIDE
SEL_BLOCK = 64
TOP_N = 16
WINDOW = 512
Q_BLOCK = 128
NEG = -1e30
FORCE_BONUS = 1e4
Q_END = INNER
KV_END = Q_END + 6 * KV_W
G_END = KV_END + 3 * N_HEADS
NSA_IN = G_END + INNER

F32 = jnp.float32
BF16 = jnp.bfloat16
VMEM_LIMIT_BYTES = 48 * 1024 * 1024


def _matmul_kernel(x_ref, w_ref, o_ref):
    o_ref[...] = jnp.dot(x_ref[...].astype(BF16), w_ref[...].astype(BF16), preferred_element_type=F32)


def _pick_tile(n, pref):
    for t in pref:
        if n % t == 0:
            return t
    return n


def matmul(x, w):
    m, k = x.shape
    n = w.shape[1]
    tm = _pick_tile(m, (512, 256, 128, 64, 32, 16, 8))
    tn = _pick_tile(n, (1024, 512, 256, 128))
    return pl.pallas_call(
        _matmul_kernel,
        grid=(n // tn, m // tm),
        in_specs=[pl.BlockSpec((tm, k), lambda j, i: (i, 0)),
                  pl.BlockSpec((k, tn), lambda j, i: (0, j))],
        out_specs=pl.BlockSpec((tm, tn), lambda j, i: (i, j)),
        out_shape=jax.ShapeDtypeStruct((m, n), F32),
        compiler_params=pltpu.CompilerParams(
            dimension_semantics=("arbitrary", "arbitrary"), vmem_limit_bytes=VMEM_LIMIT_BYTES),
        name="matmul",
    )(x, w)


def proj(x, w):
    lead = x.shape[:-1]
    return matmul(x.reshape(-1, x.shape[-1]), w).reshape(*lead, w.shape[1])


LANES = 128
WKV_CHUNK = 64
WKV_T_BLOCK = 512
WKV_PAIRS = 8


def _dot(a, b):
    return jnp.dot(a.astype(BF16), b.astype(BF16), preferred_element_type=F32)


def _dot_nt(a, b):
    return lax.dot_general(a.astype(BF16), b.astype(BF16), (((1,), (1,)), ((), ())), preferred_element_type=F32)


def _dot_tn(a, b):
    return lax.dot_general(a.astype(BF16), b.astype(BF16), (((0,), (0,)), ((), ())), preferred_element_type=F32)


def _wkv_kernel(r_ref, ld_ref, k_ref, v_ref, kk_ref, a_ref, s0_ref, y_ref, st_ref, s_scr, *, chunk, n_chunks, n_pairs):
    C = chunk
    R = 2 * C
    tb = pl.program_id(2)

    @pl.when(tb == 0)
    def _():
        z = jnp.zeros((RW_HEAD, RW_HEAD), F32)
        for g in range(n_pairs):
            s_scr[g] = jnp.concatenate(
                [jnp.concatenate([s0_ref[0, 2 * g], z], axis=1),
                 jnp.concatenate([z, s0_ref[0, 2 * g + 1]], axis=1)], axis=0)

    lane = lax.broadcasted_iota(jnp.int32, (1, LANES), 1)
    head_a = lane < RW_HEAD
    row = lax.broadcasted_iota(jnp.int32, (R, R), 0)
    col = lax.broadcasted_iota(jnp.int32, (R, R), 1)
    same = (row // C) == (col // C)
    strict = same & ((col % C) < (row % C))
    incl = same & ((col % C) <= (row % C))
    eye = (row == col).astype(F32)
    tr = lax.broadcasted_iota(jnp.int32, (C, C), 0)
    tc = lax.broadcasted_iota(jnp.int32, (C, C), 1)
    tri = (tc <= tr).astype(BF16)

    def stack(z):
        return jnp.concatenate([jnp.where(head_a, z, 0.0), jnp.where(head_a, 0.0, z)], axis=0)

    def body(ci, carry):
        sl = pl.ds(pl.multiple_of(ci * C, C), C)
        G = range(n_pairs)
        ld = [ld_ref[0, sl, g * LANES:(g + 1) * LANES] for g in G]
        ld_hi = [x.astype(BF16) for x in ld]
        ld_lo = [(x - h.astype(F32)).astype(BF16) for x, h in zip(ld, ld_hi)]
        cum = [jnp.dot(tri, h, preferred_element_type=F32) + jnp.dot(tri, l, preferred_element_type=F32)
               for h, l in zip(ld_hi, ld_lo)]
        e_neg = [jnp.exp(-c) for c in cum]
        kk = [kk_ref[0, sl, g * LANES:(g + 1) * LANES] for g in G]
        a_s = [stack(-kk[g] * jnp.exp(cum[g] - ld[g])) for g in G]
        r_s = [stack(r_ref[0, sl, g * LANES:(g + 1) * LANES] * jnp.exp(cum[g])) for g in G]
        b_s = [stack(kk[g] * a_ref[0, sl, g * LANES:(g + 1) * LANES] * e_neg[g]) for g in G]
        k_s = [stack(k_ref[0, sl, g * LANES:(g + 1) * LANES] * e_neg[g]) for g in G]
        v_s = [stack(v_ref[0, sl, g * LANES:(g + 1) * LANES]) for g in G]
        s2 = [s_scr[g] for g in G]
        ab = [jnp.where(strict, _dot_nt(a_s[g], b_s[g]), 0.0) for g in G]
        ak = [jnp.where(strict, _dot_nt(a_s[g], k_s[g]), 0.0) for g in G]
        rb = [jnp.where(incl, _dot_nt(r_s[g], b_s[g]), 0.0) for g in G]
        rk = [jnp.where(incl, _dot_nt(r_s[g], k_s[g]), 0.0) for g in G]
        rhs = [_dot_nt(a_s[g], s2[g]) + _dot(ak[g], v_s[g]) for g in G]
        y0 = [_dot_nt(r_s[g], s2[g]) + _dot(rk[g], v_s[g]) for g in G]
        tm = [eye + ab[g] for g in G]
        p = ab
        n = 2
        while n < C:
            p = [_dot(p[g], p[g]) for g in G]
            tm = [tm[g] + _dot(tm[g], p[g]) for g in G]
            n *= 2
        u_s = [_dot(tm[g], rhs[g]) for g in G]
        y_s = [y0[g] + _dot(rb[g], u_s[g]) for g in G]
        for g in G:
            y_ref[0, sl, g * LANES:(g + 1) * LANES] = y_s[g][:C] + y_s[g][C:]
            gamma = jnp.exp(cum[g][C - 1:C, :])
            s_scr[g] = (s2[g] + _dot_tn(u_s[g], b_s[g]) + _dot_tn(v_s[g], k_s[g])) * gamma
        return carry

    lax.fori_loop(0, n_chunks, body, 0)

    @pl.when(tb == pl.num_programs(2) - 1)
    def _():
        for g in range(n_pairs):
            st_ref[0, 2 * g] = s_scr[g, :RW_HEAD, :RW_HEAD]
            st_ref[0, 2 * g + 1] = s_scr[g, RW_HEAD:, RW_HEAD:]


def wkv_scan(r, logd, k, v, kk, a, s0, *, chunk, t_block):
    B, T, inner = r.shape
    width = WKV_PAIRS * LANES
    assert T % t_block == 0 and t_block % chunk == 0 and inner % width == 0
    seq_spec = pl.BlockSpec((1, t_block, width), lambda b, p, t: (b, t, p))
    st_spec = pl.BlockSpec((1, 2 * WKV_PAIRS, RW_HEAD, RW_HEAD), lambda b, p, t: (b, p, 0, 0))
    return pl.pallas_call(
        functools.partial(_wkv_kernel, chunk=chunk, n_chunks=t_block // chunk, n_pairs=WKV_PAIRS),
        grid=(B, inner // width, T // t_block),
        in_specs=[seq_spec] * 6 + [st_spec],
        out_specs=[seq_spec, st_spec],
        out_shape=[jax.ShapeDtypeStruct((B, T, inner), F32), jax.ShapeDtypeStruct(s0.shape, F32)],
        scratch_shapes=[pltpu.VMEM((WKV_PAIRS, LANES, LANES), F32)],
        compiler_params=pltpu.CompilerParams(
            dimension_semantics=("arbitrary", "arbitrary", "arbitrary"), vmem_limit_bytes=VMEM_LIMIT_BYTES),
        name="wkv_scan",
    )(r, logd, k, v, kk, a, s0)


def rms_norm(x, g):
    y = x * lax.rsqrt(jnp.mean(x * x, -1, keepdims=True) + NORM_EPS)
    return y * g


def rope(x, pos):
    half = ROT_DIM // 2
    inv = ROPE_THETA ** (-jnp.arange(half, dtype=F32) / half)
    ang = pos.astype(F32)[:, None] * inv[None, :]
    cos = jnp.cos(ang)[None, :, None, :]
    sin = jnp.sin(ang)[None, :, None, :]
    x1, x2, rest = x[..., :half], x[..., half:ROT_DIM], x[..., ROT_DIM:]
    return jnp.concatenate([x1 * cos - x2 * sin, x1 * sin + x2 * cos, rest], -1)


def rwkv7_mixer(x, shift_prev, s0, v_first, mu, w_in, w_out, w0, w1, w2, a0, a1, a2,
                k_k, k_a, r_k, lnx_w, lnx_b, vres):
    B, T, _ = x.shape
    x_prev = jnp.concatenate([shift_prev[:, None, :], x[:, :-1]], axis=1)
    mix = x[None] + (x_prev - x)[None] * mu[:, None, None, :]
    r, k, v, z = (proj(mix[p], w_in[p]) for p in range(4))
    w_log = -jax.nn.softplus(-(w0 + jnp.tanh(mix[4] @ w1) @ w2)) - 0.5
    a = jax.nn.sigmoid(a0 + (mix[5] @ a1) @ a2)
    if vres is None:
        v_first = v
    else:
        v0, v1, v2 = vres
        v = v + (v_first - v) * jax.nn.sigmoid(v0 + (mix[2] @ v1) @ v2)
    heads = lambda t: t.reshape(B, T, RW_HEADS, RW_HEAD)
    kk = heads(k * k_k)
    kk = kk * lax.rsqrt(jnp.maximum(jnp.sum(kk * kk, -1, keepdims=True), 1e-24))
    k = k * (1.0 + (a - 1.0) * k_a)
    r_h, k_h, v_h, a_h = heads(r), heads(k), heads(v), heads(a)
    logd = -jnp.exp(w_log)
    chunk = WKV_CHUNK if T % WKV_CHUNK == 0 else T
    t_block = WKV_T_BLOCK if T % WKV_T_BLOCK == 0 else T
    y, s_T = wkv_scan(r, logd, k, v, kk.reshape(B, T, INNER), a, s0, chunk=chunk, t_block=t_block)
    y = heads(y)
    mean = jnp.mean(y, -1, keepdims=True)
    var = jnp.mean(jnp.square(y - mean), -1, keepdims=True)
    y = ((y - mean) * lax.rsqrt(var + LNX_EPS)).reshape(B, T, INNER) * lnx_w + lnx_b
    y = y + (jnp.sum(r_h * k_h * r_k, -1, keepdims=True) * v_h).reshape(B, T, INNER)
    out = proj(y * jax.nn.silu(z), w_out)
    return out, v_first, s_T, x[:, -1]


def nsa_project(x, w_in, pos):
    B, T, _ = x.shape
    p = proj(x, w_in[:, :KV_END])
    q = p[..., :Q_END].reshape(B, T, N_HEADS, HEAD_DIM)
    kv = p[..., Q_END:KV_END].reshape(B, T, 6, N_KV, HEAD_DIM)
    gates = jax.nn.sigmoid(x @ w_in[:, KV_END:G_END]).reshape(B, T, N_HEADS, 3)
    z = proj(x, w_in[:, G_END:])
    kc, vc, ks, vs, kw, vw = (kv[:, :, i] for i in range(6))
    return q, rope(q, pos), gates, z, kc, vc, rope(ks, pos), vs, rope(kw, pos), vw


def compress(k, v, pe, w1, w2):
    def phi(t, a, b):
        B, T = t.shape[:2]
        n_chunk = T // CMP_STRIDE
        nc = n_chunk - CMP_RATIO + 1
        chunks = t[:, :n_chunk * CMP_STRIDE].reshape(B, n_chunk, CMP_STRIDE, N_KV, HEAD_DIM)
        parts = []
        for m in range(CMP_RATIO):
            sl = slice(m * CMP_STRIDE, (m + 1) * CMP_STRIDE)
            part = jnp.einsum('bcsgd,sdh->bcgh', chunks + pe[None, None, sl, None, :], a[sl])
            parts.append(part[:, m:m + nc])
        h = jax.nn.silu(sum(parts))
        return jnp.einsum('bngh,hd->bngd', h, b)
    ck = phi(k, w1[0], w2[0])
    cv = phi(v, w1[1], w2[1])
    c_end = jnp.arange(ck.shape[1]) * CMP_STRIDE + CMP_BLOCK - 1
    return ck, cv, c_end


def to_sel_blocks(t):
    B, T = t.shape[:2]
    ns = -(-T // SEL_BLOCK)
    t = jnp.pad(t, ((0, 0), (0, ns * SEL_BLOCK - T), (0, 0), (0, 0)))
    return t.reshape(B, ns, SEL_BLOCK, N_KV, HEAD_DIM).transpose(0, 3, 1, 2, 4)


def nsa_attend(q, qr, gates, q_pos, ck, cv, c_end, sk_b, sv_b, wk, wv, w_pos):
    B, Q = q.shape[:2]
    scale = HEAD_DIM ** -0.5
    qg = q.reshape(B, Q, N_KV, HPG, HEAD_DIM)
    qrg = qr.reshape(B, Q, N_KV, HPG, HEAD_DIM)
    s_c = jnp.einsum('bqghd,bngd->bqghn', qg, ck) * scale
    m_c = (c_end[None, :] <= q_pos[:, None])[None, :, None, None, :]
    p_c = jnp.where(m_c, jax.nn.softmax(jnp.where(m_c, s_c, NEG), -1), 0.0)
    o_c = jnp.einsum('bqghn,bngd->bqghd', p_c, cv)
    nc, ns = ck.shape[1], sk_b.shape[2]
    cs = jnp.arange(nc) * CMP_STRIDE
    ss = jnp.arange(ns) * SEL_BLOCK
    overlap = jnp.clip(jnp.minimum(cs[:, None] + CMP_BLOCK, ss[None, :] + SEL_BLOCK)
                       - jnp.maximum(cs[:, None], ss[None, :]), 0, None).astype(F32) / CMP_BLOCK
    imp = jnp.einsum('bqghn,nj->bqgj', p_c, overlap)
    cur = q_pos // SEL_BLOCK
    jb = jnp.arange(ns)
    valid = jb[None, :] <= cur[:, None]
    forced = (jb[None, :] == 0) | (jb[None, :] == cur[:, None]) | (jb[None, :] == cur[:, None] - 1)
    score = jnp.where(valid[None, :, None, :], imp + FORCE_BONUS * forced[None, :, None, :].astype(F32), NEG)
    n_top = min(TOP_N, ns)
    top_v, top_i = lax.top_k(score, n_top)
    ok = top_v > NEG / 2
    bi = jnp.arange(B)[:, None, None, None]
    gi = jnp.arange(N_KV)[None, None, :, None]
    gk = sk_b[bi, gi, top_i]
    gv = sv_b[bi, gi, top_i]
    kpos = top_i[..., None] * SEL_BLOCK + jnp.arange(SEL_BLOCK)
    m_s = (ok[..., None] & (kpos <= q_pos[None, :, None, None, None]))[:, :, :, None]
    s_s = jnp.einsum('bqghd,bqgnsd->bqghns', qrg, gk) * scale
    s_s = jnp.where(m_s, s_s, NEG).reshape(B, Q, N_KV, HPG, n_top * SEL_BLOCK)
    p_s = jax.nn.softmax(s_s, -1).reshape(B, Q, N_KV, HPG, n_top, SEL_BLOCK)
    o_s = jnp.einsum('bqghns,bqgnsd->bqghd', p_s, gv)
    s_w = jnp.einsum('bqghd,bwgd->bqghw', qrg, wk) * scale
    m_w = ((w_pos[None, :] <= q_pos[:, None]) & (w_pos[None, :] >= q_pos[:, None] - WINDOW)
           & (w_pos[None, :] >= 0))[None, :, None, None, :]
    p_w = jax.nn.softmax(jnp.where(m_w, s_w, NEG), -1)
    o_w = jnp.einsum('bqghw,bwgd->bqghd', p_w, wv)
    g = gates.reshape(B, Q, N_KV, HPG, 3)
    o = g[..., 0:1] * o_c + g[..., 1:2] * o_s + g[..., 2:3] * o_w
    return o.reshape(B, Q, INNER)


def nsa_prompt(x, w_in, w_out, pe, cw1, cw2):
    B, T, _ = x.shape
    pos = jnp.arange(T)
    q, qr, gates, z, kc, vc, ks, vs, kw, vw = nsa_project(x, w_in, pos)
    ck, cv, c_end = compress(kc, vc, pe, cw1, cw2)
    sk_b, sv_b = to_sel_blocks(ks), to_sel_blocks(vs)
    pad = ((0, 0), (WINDOW, 0), (0, 0), (0, 0))
    kw_pad, vw_pad = jnp.pad(kw, pad), jnp.pad(vw, pad)
    nb = T // Q_BLOCK
    blocks = lambda t: jnp.moveaxis(t.reshape(B, nb, Q_BLOCK, *t.shape[2:]), 1, 0)

    def one_block(args):
        qi, qri, gi, start = args
        q_pos = start + jnp.arange(Q_BLOCK)
        wk = lax.dynamic_slice_in_dim(kw_pad, start, WINDOW + Q_BLOCK, axis=1)
        wv = lax.dynamic_slice_in_dim(vw_pad, start, WINDOW + Q_BLOCK, axis=1)
        w_pos = start - WINDOW + jnp.arange(WINDOW + Q_BLOCK)
        return nsa_attend(qi, qri, gi, q_pos, ck, cv, c_end, sk_b, sv_b, wk, wv, w_pos)

    o = lax.map(one_block, (blocks(q), blocks(qr), blocks(gates), jnp.arange(nb) * Q_BLOCK))
    o = jnp.moveaxis(o, 0, 1).reshape(B, T, INNER)
    y = proj(o * jax.nn.silu(z), w_out)
    n_keep = min(WINDOW, T)
    return (y, jnp.stack([kc, vc], 2), jnp.stack([ks, vs], 2),
            jnp.stack([kw[:, -n_keep:], vw[:, -n_keep:]], 2))


def nsa_sample(x, pool_cmp, pool_sel, win_buf, page_table, w_in, w_out, pe, cw1, cw2):
    B, T, _ = x.shape
    past = page_table.shape[1] * pool_cmp.shape[1]
    pos = past + jnp.arange(T)
    q, qr, gates, z, kc, vc, ks, vs, kw, vw = nsa_project(x, w_in, pos)

    def gather(pool):
        return pool[page_table].reshape(B, past, 2, N_KV, HEAD_DIM)

    pc, ps = gather(pool_cmp), gather(pool_sel)
    ck, cv, c_end = compress(jnp.concatenate([pc[:, :, 0], kc], 1),
                             jnp.concatenate([pc[:, :, 1], vc], 1), pe, cw1, cw2)
    sk_b = to_sel_blocks(jnp.concatenate([ps[:, :, 0], ks], 1))
    sv_b = to_sel_blocks(jnp.concatenate([ps[:, :, 1], vs], 1))
    n_buf = win_buf.shape[1]
    wk = jnp.concatenate([win_buf[:, :, 0], kw], 1)
    wv = jnp.concatenate([win_buf[:, :, 1], vw], 1)
    w_pos = past - n_buf + jnp.arange(n_buf + T)
    o = nsa_attend(q, qr, gates, pos, ck, cv, c_end, sk_b, sv_b, wk, wv, w_pos)
    y = proj(o * jax.nn.silu(z), w_out)
    n_keep = min(WINDOW, n_buf + T)
    return (y, jnp.stack([kc, vc], 2), jnp.stack([ks, vs], 2),
            jnp.stack([wk[:, -n_keep:], wv[:, -n_keep:]], 2))


def kernel(x_prompt, x_sample, state_rwkv_wkv, state_rwkv_shift, cache_nsa_cmp, cache_nsa_sel, cache_nsa_win, page_table, norm_g, final_norm_g, rwkv_mu, rwkv_w_in, rwkv_w_out, rwkv_w0, rwkv_w1, rwkv_w2, rwkv_a0, rwkv_a1, rwkv_a2, rwkv_v0, rwkv_v1, rwkv_v2, rwkv_k_k, rwkv_k_a, rwkv_r_k, rwkv_lnx_w, rwkv_lnx_b, nsa_w_in, nsa_w_out, nsa_cmp_pe, nsa_cmp_w1, nsa_cmp_w2):

    def trunk(x, wkv0, shift0, sample):
        h = x
        v_first = None
        wkv, shift, cmp_rows, sel_rows, win_rows = [], [], [], [], []
        for layer in range(DEPTH):
            j = layer // N_MIXERS
            xn = rms_norm(h, norm_g[layer])
            if layer % N_MIXERS == 0:
                vres = None if j == 0 else (rwkv_v0[j - 1], rwkv_v1[j - 1], rwkv_v2[j - 1])
                o, v_first, s_T, last = rwkv7_mixer(
                    xn, shift0[j], wkv0[j], v_first, rwkv_mu[j], rwkv_w_in[j], rwkv_w_out[j],
                    rwkv_w0[j], rwkv_w1[j], rwkv_w2[j], rwkv_a0[j], rwkv_a1[j], rwkv_a2[j],
                    rwkv_k_k[j], rwkv_k_a[j], rwkv_r_k[j], rwkv_lnx_w[j], rwkv_lnx_b[j], vres)
                wkv.append(s_T)
                shift.append(last)
            else:
                if sample:
                    o, c, s, w = nsa_sample(xn, cache_nsa_cmp[j], cache_nsa_sel[j], cache_nsa_win[j],
                                            page_table, nsa_w_in[j], nsa_w_out[j], nsa_cmp_pe[j],
                                            nsa_cmp_w1[j], nsa_cmp_w2[j])
                else:
                    o, c, s, w = nsa_prompt(xn, nsa_w_in[j], nsa_w_out[j], nsa_cmp_pe[j],
                                            nsa_cmp_w1[j], nsa_cmp_w2[j])
                cmp_rows.append(c)
                sel_rows.append(s)
                win_rows.append(w)
            h = h + o
        return (rms_norm(h, final_norm_g), jnp.stack(wkv), jnp.stack(shift),
                jnp.stack(cmp_rows), jnp.stack(sel_rows), jnp.stack(win_rows))

    zeros_wkv = jnp.zeros((rwkv_mu.shape[0], x_prompt.shape[0], RW_HEADS, RW_HEAD, RW_HEAD), x_prompt.dtype)
    zeros_shift = jnp.zeros((rwkv_mu.shape[0], x_prompt.shape[0], D_MODEL), x_prompt.dtype)
    y_prompt, wkv_p, shift_p, cmp_p, sel_p, win_p = trunk(x_prompt, zeros_wkv, zeros_shift, False)
    y_sample, wkv_s, shift_s, cmp_s, sel_s, win_s = trunk(x_sample, state_rwkv_wkv, state_rwkv_shift, True)
    return (y_prompt, y_sample, wkv_p, wkv_s, shift_p, shift_s, cmp_p, cmp_s, sel_p, sel_s, win_p, win_s)
```

```python
import functools

import jax
import jax.numpy as jnp
from jax import lax
from jax.experimental import pallas as pl
from jax.experimental.pallas import tpu as pltpu

D_MODEL = 1024
DEPTH = 4
N_MIXERS = 2
INNER = 2 * D_MODEL
NORM_EPS = 1e-6
RW_HEAD = 64
RW_HEADS = INNER // RW_HEAD
LNX_EPS = 64e-5
HEAD_DIM = 64
N_HEADS = INNER // HEAD_DIM
N_KV = 4
HPG = N_HEADS // N_KV
KV_W = N_KV * HEAD_DIM
ROT_DIM = HEAD_DIM // 4
ROPE_THETA = 500000.0
CMP_BLOCK = 32
CMP_STRIDE = 16
CMP_RATIO = CMP_BLOCK // CMP_STRIDE
SEL_BLOCK = 64
TOP_N = 16
WINDOW = 512
Q_BLOCK = 128
NEG = -1e30
FORCE_BONUS = 1e4
Q_END = INNER
KV_END = Q_END + 6 * KV_W
G_END = KV_END + 3 * N_HEADS
NSA_IN = G_END + INNER

F32 = jnp.float32
BF16 = jnp.bfloat16
VMEM_LIMIT_BYTES = 48 * 1024 * 1024


def _matmul_kernel(x_ref, w_ref, o_ref):
    o_ref[...] = jnp.dot(x_ref[...].astype(BF16), w_ref[...].astype(BF16), preferred_element_type=F32)


def _pick_tile(n, pref):
    for t in pref:
        if n % t == 0:
            return t
    return n


def matmul(x, w):
    m, k = x.shape
    n = w.shape[1]
    tm = _pick_tile(m, (512, 256, 128, 64, 32, 16, 8))
    tn = _pick_tile(n, (1024, 512, 256, 128))
    return pl.pallas_call(
        _matmul_kernel,
        grid=(n // tn, m // tm),
        in_specs=[pl.BlockSpec((tm, k), lambda j, i: (i, 0)),
                  pl.BlockSpec((k, tn), lambda j, i: (0, j))],
        out_specs=pl.BlockSpec((tm, tn), lambda j, i: (i, j)),
        out_shape=jax.ShapeDtypeStruct((m, n), F32),
        compiler_params=pltpu.CompilerParams(
            dimension_semantics=("arbitrary", "arbitrary"), vmem_limit_bytes=VMEM_LIMIT_BYTES),
        name="matmul",
    )(x, w)


def proj(x, w):
    lead = x.shape[:-1]
    return matmul(x.reshape(-1, x.shape[-1]), w).reshape(*lead, w.shape[1])


LANES = 128
WKV_CHUNK = 64
WKV_T_BLOCK = 512
WKV_PAIRS = 8


def _dot(a, b):
    return jnp.dot(a.astype(BF16), b.astype(BF16), preferred_element_type=F32)


def _dot_nt(a, b):
    return lax.dot_general(a.astype(BF16), b.astype(BF16), (((1,), (1,)), ((), ())), preferred_element_type=F32)


def _dot_tn(a, b):
    return lax.dot_general(a.astype(BF16), b.astype(BF16), (((0,), (0,)), ((), ())), preferred_element_type=F32)


def _wkv_kernel(r_ref, ld_ref, k_ref, v_ref, kk_ref, a_ref, s0_ref, y_ref, st_ref, s_scr, *, chunk, n_chunks, n_pairs):
    C = chunk
    R = 2 * C
    tb = pl.program_id(2)

    @pl.when(tb == 0)
    def _():
        z = jnp.zeros((RW_HEAD, RW_HEAD), F32)
        for g in range(n_pairs):
            s_scr[g] = jnp.concatenate(
                [jnp.concatenate([s0_ref[0, 2 * g], z], axis=1),
                 jnp.concatenate([z, s0_ref[0, 2 * g + 1]], axis=1)], axis=0)

    lane = lax.broadcasted_iota(jnp.int32, (1, LANES), 1)
    head_a = lane < RW_HEAD
    row = lax.broadcasted_iota(jnp.int32, (R, R), 0)
    col = lax.broadcasted_iota(jnp.int32, (R, R), 1)
    same = (row // C) == (col // C)
    strict = same & ((col % C) < (row % C))
    incl = same & ((col % C) <= (row % C))
    eye = (row == col).astype(F32)
    tr = lax.broadcasted_iota(jnp.int32, (C, C), 0)
    tc = lax.broadcasted_iota(jnp.int32, (C, C), 1)
    tri = (tc <= tr).astype(BF16)

    def stack(z):
        return jnp.concatenate([jnp.where(head_a, z, 0.0), jnp.where(head_a, 0.0, z)], axis=0)

    def body(ci, carry):
        sl = pl.ds(pl.multiple_of(ci * C, C), C)
        G = range(n_pairs)
        ld = [ld_ref[0, sl, g * LANES:(g + 1) * LANES] for g in G]
        ld_hi = [x.astype(BF16) for x in ld]
        ld_lo = [(x - h.astype(F32)).astype(BF16) for x, h in zip(ld, ld_hi)]
        cum = [jnp.dot(tri, h, preferred_element_type=F32) + jnp.dot(tri, l, preferred_element_type=F32)
               for h, l in zip(ld_hi, ld_lo)]
        e_neg = [jnp.exp(-c) for c in cum]
        kk = [kk_ref[0, sl, g * LANES:(g + 1) * LANES] for g in G]
        a_s = [stack(-kk[g] * jnp.exp(cum[g] - ld[g])) for g in G]
        r_s = [stack(r_ref[0, sl, g * LANES:(g + 1) * LANES] * jnp.exp(cum[g])) for g in G]
        b_s = [stack(kk[g] * a_ref[0, sl, g * LANES:(g + 1) * LANES] * e_neg[g]) for g in G]
        k_s = [stack(k_ref[0, sl, g * LANES:(g + 1) * LANES] * e_neg[g]) for g in G]
        v_s = [stack(v_ref[0, sl, g * LANES:(g + 1) * LANES]) for g in G]
        s2 = [s_scr[g] for g in G]
        ab = [jnp.where(strict, _dot_nt(a_s[g], b_s[g]), 0.0) for g in G]
        ak = [jnp.where(strict, _dot_nt(a_s[g], k_s[g]), 0.0) for g in G]
        rb = [jnp.where(incl, _dot_nt(r_s[g], b_s[g]), 0.0) for g in G]
        rk = [jnp.where(incl, _dot_nt(r_s[g], k_s[g]), 0.0) for g in G]
        rhs = [_dot_nt(a_s[g], s2[g]) + _dot(ak[g], v_s[g]) for g in G]
        y0 = [_dot_nt(r_s[g], s2[g]) + _dot(rk[g], v_s[g]) for g in G]
        tm = [eye + ab[g] for g in G]
        p = ab
        n = 2
        while n < C:
            p = [_dot(p[g], p[g]) for g in G]
            tm = [tm[g] + _dot(tm[g], p[g]) for g in G]
            n *= 2
        u_s = [_dot(tm[g], rhs[g]) for g in G]
        y_s = [y0[g] + _dot(rb[g], u_s[g]) for g in G]
        for g in G:
            y_ref[0, sl, g * LANES:(g + 1) * LANES] = y_s[g][:C] + y_s[g][C:]
            gamma = jnp.exp(cum[g][C - 1:C, :])
            s_scr[g] = (s2[g] + _dot_tn(u_s[g], b_s[g]) + _dot_tn(v_s[g], k_s[g])) * gamma
        return carry

    lax.fori_loop(0, n_chunks, body, 0)

    @pl.when(tb == pl.num_programs(2) - 1)
    def _():
        for g in range(n_pairs):
            st_ref[0, 2 * g] = s_scr[g, :RW_HEAD, :RW_HEAD]
            st_ref[0, 2 * g + 1] = s_scr[g, RW_HEAD:, RW_HEAD:]


def wkv_scan(r, logd, k, v, kk, a, s0, *, chunk, t_block):
    B, T, inner = r.shape
    width = WKV_PAIRS * LANES
    assert T % t_block == 0 and t_block % chunk == 0 and inner % width == 0
    seq_spec = pl.BlockSpec((1, t_block, width), lambda b, p, t: (b, t, p))
    st_spec = pl.BlockSpec((1, 2 * WKV_PAIRS, RW_HEAD, RW_HEAD), lambda b, p, t: (b, p, 0, 0))
    return pl.pallas_call(
        functools.partial(_wkv_kernel, chunk=chunk, n_chunks=t_block // chunk, n_pairs=WKV_PAIRS),
        grid=(B, inner // width, T // t_block),
        in_specs=[seq_spec] * 6 + [st_spec],
        out_specs=[seq_spec, st_spec],
        out_shape=[jax.ShapeDtypeStruct((B, T, inner), F32), jax.ShapeDtypeStruct(s0.shape, F32)],
        scratch_shapes=[pltpu.VMEM((WKV_PAIRS, LANES, LANES), F32)],
        compiler_params=pltpu.CompilerParams(
            dimension_semantics=("arbitrary", "arbitrary", "arbitrary"), vmem_limit_bytes=VMEM_LIMIT_BYTES),
        name="wkv_scan",
    )(r, logd, k, v, kk, a, s0)


SEL_TILE = 512
WIN_TILE = 256
BLOCKS_PER_TILE = SEL_TILE // SEL_BLOCK
LANE_GROUPS = 2
ONES_ROWS = 16
LOG2_E = 1.4426950408889634
PICKED = -3e38


def _attend_tile(k_tile, vT_tile, bias, qT, m_ref, l_ref, acc_ref):
    width = qT.shape[1] // LANE_GROUPS
    cols = [slice(c * width, (c + 1) * width) for c in range(LANE_GROUPS)]
    s = [jnp.dot(k_tile, qT[:, c], preferred_element_type=F32) + bias[:, c] for c in cols]
    m_old = [m_ref[:, c] for c in cols]
    m_new = [jnp.maximum(mo, jnp.max(sc, axis=0, keepdims=True)) for mo, sc in zip(m_old, s)]
    p = [jnp.exp2(sc - mn).astype(BF16) for sc, mn in zip(s, m_new)]
    pv = [jnp.dot(vT_tile, pc, preferred_element_type=F32) for pc in p]
    for c, mo, mn, pvc in zip(cols, m_old, m_new, pv):
        alpha = jnp.exp2(mo - mn)
        l_ref[:, c] = alpha * l_ref[:, c] + pvc[HEAD_DIM:HEAD_DIM + 1, :]
        acc_ref[:, c] = alpha * acc_ref[:, c] + pvc[:HEAD_DIM, :]
        m_ref[:, c] = mn


def _nsa_prompt_kernel(qT_ref, qrT_ref, gT_ref, ck_ref, cvT_ref, ovT_ref, ks_ref, vsT_ref, kw_ref, vwT_ref,
                       oT_ref, sel_scr, m_scr, l_scr, acc_scr, o_scr, *, n_cmp, n_sel):
    i = pl.program_id(2)
    q0 = i * Q_BLOCK
    qpos = q0 + lax.broadcasted_iota(jnp.int32, (1, Q_BLOCK), 1)

    def heads_on_lanes(ref):
        return jnp.concatenate([ref[0, h * HEAD_DIM:(h + 1) * HEAD_DIM, :] for h in range(HPG)], axis=1)

    def gate_row(br):
        return jnp.concatenate([gT_ref[0, 0, br, h:h + 1, :] for h in range(HPG)], axis=1)

    def per_head(x):
        return jnp.concatenate([x] * HPG, axis=1)

    n_iota = lax.broadcasted_iota(jnp.int32, (n_cmp, Q_BLOCK), 0)
    c_valid = per_head((n_iota * CMP_STRIDE + (CMP_BLOCK - 1)) <= qpos)
    s = jnp.dot(ck_ref[0, 0], heads_on_lanes(qT_ref), preferred_element_type=F32)
    s = jnp.where(c_valid, s, NEG)
    mx = jnp.max(s, axis=0, keepdims=True)
    e = jnp.where(c_valid, jnp.exp(s - mx), 0.0)
    l = jnp.sum(e, axis=0, keepdims=True)
    p = e * (1.0 / jnp.where(l > 0.0, l, 1.0))
    o_scr[...] = gate_row(0) * jnp.dot(cvT_ref[0, 0], p.astype(BF16), preferred_element_type=F32)
    p_sum = p[:, :Q_BLOCK]
    for h in range(1, HPG):
        p_sum = p_sum + p[:, h * Q_BLOCK:(h + 1) * Q_BLOCK]

    ps_hi = p_sum.astype(BF16)
    ps_lo = (p_sum - ps_hi.astype(F32)).astype(BF16)
    ovT = ovT_ref[...]
    imp = jnp.dot(ovT, ps_hi, preferred_element_type=F32) + jnp.dot(ovT, ps_lo, preferred_element_type=F32)
    j_iota = lax.broadcasted_iota(jnp.int32, (n_sel, Q_BLOCK), 0)
    cur = qpos // SEL_BLOCK
    forced = (j_iota == 0) | (j_iota == cur) | (j_iota == cur - 1)
    score = jnp.where(j_iota <= cur, imp + jnp.where(forced, FORCE_BONUS, 0.0), NEG)
    sel = jnp.zeros((n_sel, Q_BLOCK), F32)
    for _ in range(TOP_N):
        mx = jnp.max(score, axis=0, keepdims=True)
        first = jnp.min(jnp.where(score == mx, j_iota, n_sel), axis=0, keepdims=True)
        pick = j_iota == first
        sel = jnp.where(pick & (mx > NEG / 2), 1.0, sel)
        score = jnp.where(pick, PICKED, score)
    sel_scr[...] = sel

    def reset():
        m_scr[...] = jnp.full(m_scr.shape, NEG, F32)
        l_scr[...] = jnp.zeros(l_scr.shape, F32)
        acc_scr[...] = jnp.zeros(acc_scr.shape, F32)

    qrT = heads_on_lanes(qrT_ref)
    sel_rows = lax.broadcasted_iota(jnp.int32, (SEL_TILE, Q_BLOCK), 0)
    win_rows = lax.broadcasted_iota(jnp.int32, (WIN_TILE, Q_BLOCK), 0)

    reset()

    def sel_body(kt, carry):
        chosen = sel_scr[pl.ds(pl.multiple_of(kt * BLOCKS_PER_TILE, BLOCKS_PER_TILE), BLOCKS_PER_TILE), :]
        chosen = jnp.concatenate(
            [jnp.broadcast_to(chosen[b:b + 1, :], (SEL_BLOCK, Q_BLOCK)) for b in range(BLOCKS_PER_TILE)], axis=0)
        allowed = (chosen > 0.5) & ((kt * SEL_TILE + sel_rows) <= qpos)
        bias = per_head(jnp.where(allowed, 0.0, NEG))
        _attend_tile(ks_ref[0, 0, kt], vsT_ref[0, 0, kt], bias, qrT, m_scr, l_scr, acc_scr)
        return carry

    lax.fori_loop(0, (q0 + Q_BLOCK + SEL_TILE - 1) // SEL_TILE, sel_body, 0)
    o_scr[...] = o_scr[...] + gate_row(1) * acc_scr[...] * (1.0 / l_scr[...])

    reset()

    def win_body(wt, carry):
        kp = wt * WIN_TILE + win_rows
        allowed = (kp <= qpos) & (kp >= qpos - WINDOW)
        bias = per_head(jnp.where(allowed, 0.0, NEG))
        _attend_tile(kw_ref[0, 0, wt], vwT_ref[0, 0, wt], bias, qrT, m_scr, l_scr, acc_scr)
        return carry

    lax.fori_loop(jnp.maximum(q0 - WINDOW, 0) // WIN_TILE, (q0 + Q_BLOCK + WIN_TILE - 1) // WIN_TILE, win_body, 0)
    o = o_scr[...] + gate_row(2) * acc_scr[...] * (1.0 / l_scr[...])
    for h in range(HPG):
        oT_ref[0, h * HEAD_DIM:(h + 1) * HEAD_DIM, :] = o[:, h * Q_BLOCK:(h + 1) * Q_BLOCK]


def nsa_prompt_attend(q, qr, gates, ck, cv, ks, vs, kw, vw):
    B, T, inner = q.shape
    n_sel = T // SEL_BLOCK
    nc = ck.shape[1]
    n_cmp = -(-nc // LANES) * LANES
    grp = HPG * HEAD_DIM
    qT = jnp.swapaxes(q, 1, 2).astype(BF16)
    qrT = jnp.swapaxes(qr, 1, 2).astype(BF16)
    gT = gates.reshape(B, T, N_KV, HPG, 3).transpose(0, 2, 4, 3, 1)
    pad_c = ((0, 0), (0, n_cmp - nc), (0, 0), (0, 0))
    ck_p = jnp.pad(ck, pad_c).transpose(0, 2, 1, 3).astype(BF16)
    cvT = jnp.pad(cv, pad_c).transpose(0, 2, 3, 1).astype(BF16)
    cs = jnp.arange(n_cmp) * CMP_STRIDE
    ss = jnp.arange(n_sel) * SEL_BLOCK
    ovT = (jnp.clip(jnp.minimum(cs[None, :] + CMP_BLOCK, ss[:, None] + SEL_BLOCK)
                    - jnp.maximum(cs[None, :], ss[:, None]), 0, None).astype(F32) / CMP_BLOCK)
    ovT = jnp.where(jnp.arange(n_cmp)[None, :] < nc, ovT, 0.0).astype(BF16)

    def key_tiles(t, tile):
        return t.reshape(B, T // tile, tile, N_KV, HEAD_DIM).transpose(0, 3, 1, 2, 4).astype(BF16)

    def val_tiles(t, tile):
        t = t.reshape(B, T // tile, tile, N_KV, HEAD_DIM).transpose(0, 3, 1, 4, 2)
        ones = jnp.ones(t.shape[:3] + (ONES_ROWS, tile), t.dtype)
        return jnp.concatenate([t, ones], axis=3).astype(BF16)

    q_spec = pl.BlockSpec((1, grp, Q_BLOCK), lambda b, g, i: (b, g, i))
    per_group = lambda shape: pl.BlockSpec((1, 1) + shape, lambda b, g, i: (b, g) + (0,) * len(shape))
    oT = pl.pallas_call(
        functools.partial(_nsa_prompt_kernel, n_cmp=n_cmp, n_sel=n_sel),
        grid=(B, N_KV, T // Q_BLOCK),
        in_specs=[q_spec, q_spec,
                  pl.BlockSpec((1, 1, 3, HPG, Q_BLOCK), lambda b, g, i: (b, g, 0, 0, i)),
                  per_group((n_cmp, HEAD_DIM)), per_group((HEAD_DIM, n_cmp)),
                  pl.BlockSpec((n_sel, n_cmp), lambda b, g, i: (0, 0)),
                  per_group((T // SEL_TILE, SEL_TILE, HEAD_DIM)), per_group((T // SEL_TILE, HEAD_DIM + ONES_ROWS, SEL_TILE)),
                  per_group((T // WIN_TILE, WIN_TILE, HEAD_DIM)), per_group((T // WIN_TILE, HEAD_DIM + ONES_ROWS, WIN_TILE))],
        out_specs=q_spec,
        out_shape=jax.ShapeDtypeStruct((B, inner, T), F32),
        scratch_shapes=[pltpu.VMEM((n_sel, Q_BLOCK), F32),
                        pltpu.VMEM((1, HPG * Q_BLOCK), F32),
                        pltpu.VMEM((1, HPG * Q_BLOCK), F32),
                        pltpu.VMEM((HEAD_DIM, HPG * Q_BLOCK), F32),
                        pltpu.VMEM((HEAD_DIM, HPG * Q_BLOCK), F32)],
        compiler_params=pltpu.CompilerParams(
            dimension_semantics=("arbitrary", "arbitrary", "arbitrary"), vmem_limit_bytes=VMEM_LIMIT_BYTES),
        name="nsa_prompt_attend",
    )(qT, qrT, gT, ck_p, cvT, ovT, key_tiles(ks, SEL_TILE), val_tiles(vs, SEL_TILE),
      key_tiles(kw, WIN_TILE), val_tiles(vw, WIN_TILE))
    return jnp.swapaxes(oT, 1, 2)


def rms_norm(x, g):
    y = x * lax.rsqrt(jnp.mean(x * x, -1, keepdims=True) + NORM_EPS)
    return y * g


def rope(x, pos):
    half = ROT_DIM // 2
    inv = ROPE_THETA ** (-jnp.arange(half, dtype=F32) / half)
    ang = pos.astype(F32)[:, None] * inv[None, :]
    cos = jnp.cos(ang)[None, :, None, :]
    sin = jnp.sin(ang)[None, :, None, :]
    x1, x2, rest = x[..., :half], x[..., half:ROT_DIM], x[..., ROT_DIM:]
    return jnp.concatenate([x1 * cos - x2 * sin, x1 * sin + x2 * cos, rest], -1)


def rwkv7_mixer(x, shift_prev, s0, v_first, mu, w_in, w_out, w0, w1, w2, a0, a1, a2,
                k_k, k_a, r_k, lnx_w, lnx_b, vres):
    B, T, _ = x.shape
    x_prev = jnp.concatenate([shift_prev[:, None, :], x[:, :-1]], axis=1)
    mix = x[None] + (x_prev - x)[None] * mu[:, None, None, :]
    r, k, v, z = (proj(mix[p], w_in[p]) for p in range(4))
    w_log = -jax.nn.softplus(-(w0 + jnp.tanh(mix[4] @ w1) @ w2)) - 0.5
    a = jax.nn.sigmoid(a0 + (mix[5] @ a1) @ a2)
    if vres is None:
        v_first = v
    else:
        v0, v1, v2 = vres
        v = v + (v_first - v) * jax.nn.sigmoid(v0 + (mix[2] @ v1) @ v2)
    heads = lambda t: t.reshape(B, T, RW_HEADS, RW_HEAD)
    kk = heads(k * k_k)
    kk = kk * lax.rsqrt(jnp.maximum(jnp.sum(kk * kk, -1, keepdims=True), 1e-24))
    k = k * (1.0 + (a - 1.0) * k_a)
    r_h, k_h, v_h, a_h = heads(r), heads(k), heads(v), heads(a)
    logd = -jnp.exp(w_log)
    chunk = WKV_CHUNK if T % WKV_CHUNK == 0 else T
    t_block = WKV_T_BLOCK if T % WKV_T_BLOCK == 0 else T
    y, s_T = wkv_scan(r, logd, k, v, kk.reshape(B, T, INNER), a, s0, chunk=chunk, t_block=t_block)
    y = heads(y)
    mean = jnp.mean(y, -1, keepdims=True)
    var = jnp.mean(jnp.square(y - mean), -1, keepdims=True)
    y = ((y - mean) * lax.rsqrt(var + LNX_EPS)).reshape(B, T, INNER) * lnx_w + lnx_b
    y = y + (jnp.sum(r_h * k_h * r_k, -1, keepdims=True) * v_h).reshape(B, T, INNER)
    out = proj(y * jax.nn.silu(z), w_out)
    return out, v_first, s_T, x[:, -1]


def nsa_project(x, w_in, pos):
    B, T, _ = x.shape
    p = proj(x, w_in[:, :KV_END])
    q = p[..., :Q_END].reshape(B, T, N_HEADS, HEAD_DIM)
    kv = p[..., Q_END:KV_END].reshape(B, T, 6, N_KV, HEAD_DIM)
    gates = jax.nn.sigmoid(x @ w_in[:, KV_END:G_END]).reshape(B, T, N_HEADS, 3)
    z = proj(x, w_in[:, G_END:])
    kc, vc, ks, vs, kw, vw = (kv[:, :, i] for i in range(6))
    return q, rope(q, pos), gates, z, kc, vc, rope(ks, pos), vs, rope(kw, pos), vw


def compress(k, v, pe, w1, w2):
    def phi(t, a, b):
        B, T = t.shape[:2]
        n_chunk = T // CMP_STRIDE
        nc = n_chunk - CMP_RATIO + 1
        chunks = t[:, :n_chunk * CMP_STRIDE].reshape(B, n_chunk, CMP_STRIDE, N_KV, HEAD_DIM)
        parts = []
        for m in range(CMP_RATIO):
            sl = slice(m * CMP_STRIDE, (m + 1) * CMP_STRIDE)
            part = jnp.einsum('bcsgd,sdh->bcgh', chunks + pe[None, None, sl, None, :], a[sl])
            parts.append(part[:, m:m + nc])
        h = jax.nn.silu(sum(parts))
        return jnp.einsum('bngh,hd->bngd', h, b)
    ck = phi(k, w1[0], w2[0])
    cv = phi(v, w1[1], w2[1])
    c_end = jnp.arange(ck.shape[1]) * CMP_STRIDE + CMP_BLOCK - 1
    return ck, cv, c_end


def to_sel_blocks(t):
    B, T = t.shape[:2]
    ns = -(-T // SEL_BLOCK)
    t = jnp.pad(t, ((0, 0), (0, ns * SEL_BLOCK - T), (0, 0), (0, 0)))
    return t.reshape(B, ns, SEL_BLOCK, N_KV, HEAD_DIM).transpose(0, 3, 1, 2, 4)


def nsa_attend(q, qr, gates, q_pos, ck, cv, c_end, sk_b, sv_b, wk, wv, w_pos):
    B, Q = q.shape[:2]
    scale = HEAD_DIM ** -0.5
    qg = q.reshape(B, Q, N_KV, HPG, HEAD_DIM)
    qrg = qr.reshape(B, Q, N_KV, HPG, HEAD_DIM)
    s_c = jnp.einsum('bqghd,bngd->bqghn', qg, ck) * scale
    m_c = (c_end[None, :] <= q_pos[:, None])[None, :, None, None, :]
    p_c = jnp.where(m_c, jax.nn.softmax(jnp.where(m_c, s_c, NEG), -1), 0.0)
    o_c = jnp.einsum('bqghn,bngd->bqghd', p_c, cv)
    nc, ns = ck.shape[1], sk_b.shape[2]
    cs = jnp.arange(nc) * CMP_STRIDE
    ss = jnp.arange(ns) * SEL_BLOCK
    overlap = jnp.clip(jnp.minimum(cs[:, None] + CMP_BLOCK, ss[None, :] + SEL_BLOCK)
                       - jnp.maximum(cs[:, None], ss[None, :]), 0, None).astype(F32) / CMP_BLOCK
    imp = jnp.einsum('bqghn,nj->bqgj', p_c, overlap)
    cur = q_pos // SEL_BLOCK
    jb = jnp.arange(ns)
    valid = jb[None, :] <= cur[:, None]
    forced = (jb[None, :] == 0) | (jb[None, :] == cur[:, None]) | (jb[None, :] == cur[:, None] - 1)
    score = jnp.where(valid[None, :, None, :], imp + FORCE_BONUS * forced[None, :, None, :].astype(F32), NEG)
    n_top = min(TOP_N, ns)
    top_v, top_i = lax.top_k(score, n_top)
    ok = top_v > NEG / 2
    bi = jnp.arange(B)[:, None, None, None]
    gi = jnp.arange(N_KV)[None, None, :, None]
    gk = sk_b[bi, gi, top_i]
    gv = sv_b[bi, gi, top_i]
    kpos = top_i[..., None] * SEL_BLOCK + jnp.arange(SEL_BLOCK)
    m_s = (ok[..., None] & (kpos <= q_pos[None, :, None, None, None]))[:, :, :, None]
    s_s = jnp.einsum('bqghd,bqgnsd->bqghns', qrg, gk) * scale
    s_s = jnp.where(m_s, s_s, NEG).reshape(B, Q, N_KV, HPG, n_top * SEL_BLOCK)
    p_s = jax.nn.softmax(s_s, -1).reshape(B, Q, N_KV, HPG, n_top, SEL_BLOCK)
    o_s = jnp.einsum('bqghns,bqgnsd->bqghd', p_s, gv)
    s_w = jnp.einsum('bqghd,bwgd->bqghw', qrg, wk) * scale
    m_w = ((w_pos[None, :] <= q_pos[:, None]) & (w_pos[None, :] >= q_pos[:, None] - WINDOW)
           & (w_pos[None, :] >= 0))[None, :, None, None, :]
    p_w = jax.nn.softmax(jnp.where(m_w, s_w, NEG), -1)
    o_w = jnp.einsum('bqghw,bwgd->bqghd', p_w, wv)
    g = gates.reshape(B, Q, N_KV, HPG, 3)
    o = g[..., 0:1] * o_c + g[..., 1:2] * o_s + g[..., 2:3] * o_w
    return o.reshape(B, Q, INNER)


def nsa_prompt(x, w_in, w_out, pe, cw1, cw2):
    B, T, _ = x.shape
    pos = jnp.arange(T)
    q, qr, gates, z, kc, vc, ks, vs, kw, vw = nsa_project(x, w_in, pos)
    ck, cv, c_end = compress(kc, vc, pe, cw1, cw2)
    scale = HEAD_DIM ** -0.5
    o = nsa_prompt_attend(q.reshape(B, T, INNER) * scale, qr.reshape(B, T, INNER) * (scale * LOG2_E), gates,
                          ck, cv, ks, vs, kw, vw)
    y = proj(o * jax.nn.silu(z), w_out)
    n_keep = min(WINDOW, T)
    return (y, jnp.stack([kc, vc], 2), jnp.stack([ks, vs], 2),
            jnp.stack([kw[:, -n_keep:], vw[:, -n_keep:]], 2))


def nsa_sample(x, pool_cmp, pool_sel, win_buf, page_table, w_in, w_out, pe, cw1, cw2):
    B, T, _ = x.shape
    past = page_table.shape[1] * pool_cmp.shape[1]
    pos = past + jnp.arange(T)
    q, qr, gates, z, kc, vc, ks, vs, kw, vw = nsa_project(x, w_in, pos)

    def gather(pool):
        return pool[page_table].reshape(B, past, 2, N_KV, HEAD_DIM)

    pc, ps = gather(pool_cmp), gather(pool_sel)
    ck, cv, c_end = compress(jnp.concatenate([pc[:, :, 0], kc], 1),
                             jnp.concatenate([pc[:, :, 1], vc], 1), pe, cw1, cw2)
    sk_b = to_sel_blocks(jnp.concatenate([ps[:, :, 0], ks], 1))
    sv_b = to_sel_blocks(jnp.concatenate([ps[:, :, 1], vs], 1))
    n_buf = win_buf.shape[1]
    wk = jnp.concatenate([win_buf[:, :, 0], kw], 1)
    wv = jnp.concatenate([win_buf[:, :, 1], vw], 1)
    w_pos = past - n_buf + jnp.arange(n_buf + T)
    o = nsa_attend(q, qr, gates, pos, ck, cv, c_end, sk_b, sv_b, wk, wv, w_pos)
    y = proj(o * jax.nn.silu(z), w_out)
    n_keep = min(WINDOW, n_buf + T)
    return (y, jnp.stack([kc, vc], 2), jnp.stack([ks, vs], 2),
            jnp.stack([wk[:, -n_keep:], wv[:, -n_keep:]], 2))


def kernel(x_prompt, x_sample, state_rwkv_wkv, state_rwkv_shift, cache_nsa_cmp, cache_nsa_sel, cache_nsa_win, page_table, norm_g, final_norm_g, rwkv_mu, rwkv_w_in, rwkv_w_out, rwkv_w0, rwkv_w1, rwkv_w2, rwkv_a0, rwkv_a1, rwkv_a2, rwkv_v0, rwkv_v1, rwkv_v2, rwkv_k_k, rwkv_k_a, rwkv_r_k, rwkv_lnx_w, rwkv_lnx_b, nsa_w_in, nsa_w_out, nsa_cmp_pe, nsa_cmp_w1, nsa_cmp_w2):

    def trunk(x, wkv0, shift0, sample):
        h = x
        v_first = None
        wkv, shift, cmp_rows, sel_rows, win_rows = [], [], [], [], []
        for layer in range(DEPTH):
            j = layer // N_MIXERS
            xn = rms_norm(h, norm_g[layer])
            if layer % N_MIXERS == 0:
                vres = None if j == 0 else (rwkv_v0[j - 1], rwkv_v1[j - 1], rwkv_v2[j - 1])
                o, v_first, s_T, last = rwkv7_mixer(
                    xn, shift0[j], wkv0[j], v_first, rwkv_mu[j], rwkv_w_in[j], rwkv_w_out[j],
                    rwkv_w0[j], rwkv_w1[j], rwkv_w2[j], rwkv_a0[j], rwkv_a1[j], rwkv_a2[j],
                    rwkv_k_k[j], rwkv_k_a[j], rwkv_r_k[j], rwkv_lnx_w[j], rwkv_lnx_b[j], vres)
                wkv.append(s_T)
                shift.append(last)
            else:
                if sample:
                    o, c, s, w = nsa_sample(xn, cache_nsa_cmp[j], cache_nsa_sel[j], cache_nsa_win[j],
                                            page_table, nsa_w_in[j], nsa_w_out[j], nsa_cmp_pe[j],
                                            nsa_cmp_w1[j], nsa_cmp_w2[j])
                else:
                    o, c, s, w = nsa_prompt(xn, nsa_w_in[j], nsa_w_out[j], nsa_cmp_pe[j],
                                            nsa_cmp_w1[j], nsa_cmp_w2[j])
                cmp_rows.append(c)
                sel_rows.append(s)
                win_rows.append(w)
            h = h + o
        return (rms_norm(h, final_norm_g), jnp.stack(wkv), jnp.stack(shift),
                jnp.stack(cmp_rows), jnp.stack(sel_rows), jnp.stack(win_rows))

    zeros_wkv = jnp.zeros((rwkv_mu.shape[0], x_prompt.shape[0], RW_HEADS, RW_HEAD, RW_HEAD), x_prompt.dtype)
    zeros_shift = jnp.zeros((rwkv_mu.shape[0], x_prompt.shape[0], D_MODEL), x_prompt.dtype)
    y_prompt, wkv_p, shift_p, cmp_p, sel_p, win_p = trunk(x_prompt, zeros_wkv, zeros_shift, False)
    y_sample, wkv_s, shift_s, cmp_s, sel_s, win_s = trunk(x_sample, state_rwkv_wkv, state_rwkv_shift, True)
    return (y_prompt, y_sample, wkv_p, wkv_s, shift_p, shift_s, cmp_p, cmp_s, sel_p, sel_s, win_p, win_s)
```

```python
import functools

import jax
import jax.numpy as jnp
from jax import lax
from jax.experimental import pallas as pl
from jax.experimental.pallas import tpu as pltpu

D_MODEL = 1024
DEPTH = 4
N_MIXERS = 2
INNER = 2 * D_MODEL
NORM_EPS = 1e-6
RW_HEAD = 64
RW_HEADS = INNER // RW_HEAD
LNX_EPS = 64e-5
HEAD_DIM = 64
N_HEADS = INNER // HEAD_DIM
N_KV = 4
HPG = N_HEADS // N_KV
KV_W = N_KV * HEAD_DIM
ROT_DIM = HEAD_DIM // 4
ROPE_THETA = 500000.0
CMP_BLOCK = 32
CMP_STRIDE = 16
CMP_RATIO = CMP_BLOCK // CMP_STRIDE
SEL_BLOCK = 64
TOP_N = 16
WINDOW = 512
Q_BLOCK = 128
NEG = -1e30
FORCE_BONUS = 1e4
Q_END = INNER
KV_END = Q_END + 6 * KV_W
G_END = KV_END + 3 * N_HEADS
NSA_IN = G_END + INNER

F32 = jnp.float32
BF16 = jnp.bfloat16
VMEM_LIMIT_BYTES = 48 * 1024 * 1024


def _matmul_kernel(x_ref, w_ref, o_ref):
    o_ref[...] = jnp.dot(x_ref[...].astype(BF16), w_ref[...].astype(BF16), preferred_element_type=F32)


def _pick_tile(n, pref):
    for t in pref:
        if n % t == 0:
            return t
    return n


def matmul(x, w):
    m, k = x.shape
    n = w.shape[1]
    tm = _pick_tile(m, (512, 256, 128, 64, 32, 16, 8))
    tn = _pick_tile(n, (1024, 512, 256, 128))
    return pl.pallas_call(
        _matmul_kernel,
        grid=(n // tn, m // tm),
        in_specs=[pl.BlockSpec((tm, k), lambda j, i: (i, 0)),
                  pl.BlockSpec((k, tn), lambda j, i: (0, j))],
        out_specs=pl.BlockSpec((tm, tn), lambda j, i: (i, j)),
        out_shape=jax.ShapeDtypeStruct((m, n), F32),
        compiler_params=pltpu.CompilerParams(
            dimension_semantics=("arbitrary", "arbitrary"), vmem_limit_bytes=VMEM_LIMIT_BYTES),
        name="matmul",
    )(x, w)


def proj(x, w):
    lead = x.shape[:-1]
    return matmul(x.reshape(-1, x.shape[-1]), w).reshape(*lead, w.shape[1])


LANES = 128
WKV_CHUNK = 64
WKV_T_BLOCK = 512
WKV_PAIRS = 8


def _dot(a, b):
    return jnp.dot(a.astype(BF16), b.astype(BF16), preferred_element_type=F32)


def _dot_nt(a, b):
    return lax.dot_general(a.astype(BF16), b.astype(BF16), (((1,), (1,)), ((), ())), preferred_element_type=F32)


def _dot_tn(a, b):
    return lax.dot_general(a.astype(BF16), b.astype(BF16), (((0,), (0,)), ((), ())), preferred_element_type=F32)


def _wkv_kernel(r_ref, ld_ref, k_ref, v_ref, kk_ref, a_ref, s0_ref, y_ref, st_ref, s_scr, *, chunk, n_chunks, n_pairs):
    C = chunk
    R = 2 * C
    tb = pl.program_id(2)

    @pl.when(tb == 0)
    def _():
        z = jnp.zeros((RW_HEAD, RW_HEAD), F32)
        for g in range(n_pairs):
            s_scr[g] = jnp.concatenate(
                [jnp.concatenate([s0_ref[0, 2 * g], z], axis=1),
                 jnp.concatenate([z, s0_ref[0, 2 * g + 1]], axis=1)], axis=0)

    lane = lax.broadcasted_iota(jnp.int32, (1, LANES), 1)
    head_a = lane < RW_HEAD
    row = lax.broadcasted_iota(jnp.int32, (R, R), 0)
    col = lax.broadcasted_iota(jnp.int32, (R, R), 1)
    same = (row // C) == (col // C)
    strict = same & ((col % C) < (row % C))
    incl = same & ((col % C) <= (row % C))
    eye = (row == col).astype(F32)
    tr = lax.broadcasted_iota(jnp.int32, (C, C), 0)
    tc = lax.broadcasted_iota(jnp.int32, (C, C), 1)
    tri = (tc <= tr).astype(BF16)

    def stack(z):
        return jnp.concatenate([jnp.where(head_a, z, 0.0), jnp.where(head_a, 0.0, z)], axis=0)

    def body(ci, carry):
        sl = pl.ds(pl.multiple_of(ci * C, C), C)
        G = range(n_pairs)
        ld = [ld_ref[0, sl, g * LANES:(g + 1) * LANES] for g in G]
        ld_hi = [x.astype(BF16) for x in ld]
        ld_lo = [(x - h.astype(F32)).astype(BF16) for x, h in zip(ld, ld_hi)]
        cum = [jnp.dot(tri, h, preferred_element_type=F32) + jnp.dot(tri, l, preferred_element_type=F32)
               for h, l in zip(ld_hi, ld_lo)]
        e_neg = [jnp.exp(-c) for c in cum]
        kk = [kk_ref[0, sl, g * LANES:(g + 1) * LANES] for g in G]
        a_s = [stack(-kk[g] * jnp.exp(cum[g] - ld[g])) for g in G]
        r_s = [stack(r_ref[0, sl, g * LANES:(g + 1) * LANES] * jnp.exp(cum[g])) for g in G]
        b_s = [stack(kk[g] * a_ref[0, sl, g * LANES:(g + 1) * LANES] * e_neg[g]) for g in G]
        k_s = [stack(k_ref[0, sl, g * LANES:(g + 1) * LANES] * e_neg[g]) for g in G]
        v_s = [stack(v_ref[0, sl, g * LANES:(g + 1) * LANES]) for g in G]
        s2 = [s_scr[g] for g in G]
        ab = [jnp.where(strict, _dot_nt(a_s[g], b_s[g]), 0.0) for g in G]
        ak = [jnp.where(strict, _dot_nt(a_s[g], k_s[g]), 0.0) for g in G]
        rb = [jnp.where(incl, _dot_nt(r_s[g], b_s[g]), 0.0) for g in G]
        rk = [jnp.where(incl, _dot_nt(r_s[g], k_s[g]), 0.0) for g in G]
        rhs = [_dot_nt(a_s[g], s2[g]) + _dot(ak[g], v_s[g]) for g in G]
        y0 = [_dot_nt(r_s[g], s2[g]) + _dot(rk[g], v_s[g]) for g in G]
        tm = [eye + ab[g] for g in G]
        p = ab
        n = 2
        while n < C:
            p = [_dot(p[g], p[g]) for g in G]
            tm = [tm[g] + _dot(tm[g], p[g]) for g in G]
            n *= 2
        u_s = [_dot(tm[g], rhs[g]) for g in G]
        y_s = [y0[g] + _dot(rb[g], u_s[g]) for g in G]
        for g in G:
            y_ref[0, sl, g * LANES:(g + 1) * LANES] = y_s[g][:C] + y_s[g][C:]
            gamma = jnp.exp(cum[g][C - 1:C, :])
            s_scr[g] = (s2[g] + _dot_tn(u_s[g], b_s[g]) + _dot_tn(v_s[g], k_s[g])) * gamma
        return carry

    lax.fori_loop(0, n_chunks, body, 0)

    @pl.when(tb == pl.num_programs(2) - 1)
    def _():
        for g in range(n_pairs):
            st_ref[0, 2 * g] = s_scr[g, :RW_HEAD, :RW_HEAD]
            st_ref[0, 2 * g + 1] = s_scr[g, RW_HEAD:, RW_HEAD:]


def wkv_scan(r, logd, k, v, kk, a, s0, *, chunk, t_block):
    B, T, inner = r.shape
    width = WKV_PAIRS * LANES
    assert T % t_block == 0 and t_block % chunk == 0 and inner % width == 0
    seq_spec = pl.BlockSpec((1, t_block, width), lambda b, p, t: (b, t, p))
    st_spec = pl.BlockSpec((1, 2 * WKV_PAIRS, RW_HEAD, RW_HEAD), lambda b, p, t: (b, p, 0, 0))
    return pl.pallas_call(
        functools.partial(_wkv_kernel, chunk=chunk, n_chunks=t_block // chunk, n_pairs=WKV_PAIRS),
        grid=(B, inner // width, T // t_block),
        in_specs=[seq_spec] * 6 + [st_spec],
        out_specs=[seq_spec, st_spec],
        out_shape=[jax.ShapeDtypeStruct((B, T, inner), F32), jax.ShapeDtypeStruct(s0.shape, F32)],
        scratch_shapes=[pltpu.VMEM((WKV_PAIRS, LANES, LANES), F32)],
        compiler_params=pltpu.CompilerParams(
            dimension_semantics=("arbitrary", "arbitrary", "arbitrary"), vmem_limit_bytes=VMEM_LIMIT_BYTES),
        name="wkv_scan",
    )(r, logd, k, v, kk, a, s0)


SEL_TILE = 512
WIN_TILE = 256
BLOCKS_PER_TILE = SEL_TILE // SEL_BLOCK
LANE_GROUPS = 2
ONES_ROWS = 16
LOG2_E = 1.4426950408889634
PICKED = -3e38


def _attend_tile(k_tile, vT_tile, bias, qT, m_ref, l_ref, acc_ref):
    width = qT.shape[1] // LANE_GROUPS
    cols = [slice(c * width, (c + 1) * width) for c in range(LANE_GROUPS)]
    s = [jnp.dot(k_tile, qT[:, c], preferred_element_type=F32) + bias[:, c] for c in cols]
    m_old = [m_ref[:, c] for c in cols]
    m_new = [jnp.maximum(mo, jnp.max(sc, axis=0, keepdims=True)) for mo, sc in zip(m_old, s)]
    p = [jnp.exp2(sc - mn).astype(BF16) for sc, mn in zip(s, m_new)]
    pv = [jnp.dot(vT_tile, pc, preferred_element_type=F32) for pc in p]
    for c, mo, mn, pvc in zip(cols, m_old, m_new, pv):
        alpha = jnp.exp2(mo - mn)
        l_ref[:, c] = alpha * l_ref[:, c] + pvc[HEAD_DIM:HEAD_DIM + 1, :]
        acc_ref[:, c] = alpha * acc_ref[:, c] + pvc[:HEAD_DIM, :]
        m_ref[:, c] = mn


def _nsa_prompt_kernel(qT_ref, qrT_ref, gT_ref, z_ref, ck_ref, cvT_ref, ovT_ref, ks_ref, vsT_ref, kw_ref, vwT_ref,
                       o_ref, sel_scr, m_scr, l_scr, acc_scr, o_scr, *, n_cmp, n_sel):
    i = pl.program_id(2)
    q0 = i * Q_BLOCK
    qpos = q0 + lax.broadcasted_iota(jnp.int32, (1, Q_BLOCK), 1)

    def heads_on_lanes(ref):
        return jnp.concatenate([ref[0, h * HEAD_DIM:(h + 1) * HEAD_DIM, :] for h in range(HPG)], axis=1)

    def gate_row(br):
        return jnp.concatenate([gT_ref[0, br * HPG + h:br * HPG + h + 1, :] for h in range(HPG)], axis=1)

    def per_head(x):
        return jnp.concatenate([x] * HPG, axis=1)

    n_iota = lax.broadcasted_iota(jnp.int32, (n_cmp, Q_BLOCK), 0)
    c_valid = per_head((n_iota * CMP_STRIDE + (CMP_BLOCK - 1)) <= qpos)
    s = jnp.dot(ck_ref[0, 0], heads_on_lanes(qT_ref), preferred_element_type=F32)
    s = jnp.where(c_valid, s, NEG)
    mx = jnp.max(s, axis=0, keepdims=True)
    e = jnp.where(c_valid, jnp.exp(s - mx), 0.0)
    l = jnp.sum(e, axis=0, keepdims=True)
    p = e * (1.0 / jnp.where(l > 0.0, l, 1.0))
    o_scr[...] = gate_row(0) * jnp.dot(cvT_ref[0, 0], p.astype(BF16), preferred_element_type=F32)
    p_sum = p[:, :Q_BLOCK]
    for h in range(1, HPG):
        p_sum = p_sum + p[:, h * Q_BLOCK:(h + 1) * Q_BLOCK]

    ps_hi = p_sum.astype(BF16)
    ps_lo = (p_sum - ps_hi.astype(F32)).astype(BF16)
    ovT = ovT_ref[...]
    imp = jnp.dot(ovT, ps_hi, preferred_element_type=F32) + jnp.dot(ovT, ps_lo, preferred_element_type=F32)
    j_iota = lax.broadcasted_iota(jnp.int32, (n_sel, Q_BLOCK), 0)
    cur = qpos // SEL_BLOCK
    forced = (j_iota == 0) | (j_iota == cur) | (j_iota == cur - 1)
    score = jnp.where(j_iota <= cur, imp + jnp.where(forced, FORCE_BONUS, 0.0), NEG)
    sel = jnp.zeros((n_sel, Q_BLOCK), F32)
    for _ in range(TOP_N):
        mx = jnp.max(score, axis=0, keepdims=True)
        first = jnp.min(jnp.where(score == mx, j_iota, n_sel), axis=0, keepdims=True)
        pick = j_iota == first
        sel = jnp.where(pick & (mx > NEG / 2), 1.0, sel)
        score = jnp.where(pick, PICKED, score)
    sel_scr[...] = sel

    def reset():
        m_scr[...] = jnp.full(m_scr.shape, NEG, F32)
        l_scr[...] = jnp.zeros(l_scr.shape, F32)
        acc_scr[...] = jnp.zeros(acc_scr.shape, F32)

    qrT = heads_on_lanes(qrT_ref)
    sel_rows = lax.broadcasted_iota(jnp.int32, (SEL_TILE, Q_BLOCK), 0)
    win_rows = lax.broadcasted_iota(jnp.int32, (WIN_TILE, Q_BLOCK), 0)

    reset()

    def sel_body(kt, carry):
        chosen = sel_scr[pl.ds(pl.multiple_of(kt * BLOCKS_PER_TILE, BLOCKS_PER_TILE), BLOCKS_PER_TILE), :]
        chosen = jnp.concatenate(
            [jnp.broadcast_to(chosen[b:b + 1, :], (SEL_BLOCK, Q_BLOCK)) for b in range(BLOCKS_PER_TILE)], axis=0)
        allowed = (chosen > 0.5) & ((kt * SEL_TILE + sel_rows) <= qpos)
        bias = per_head(jnp.where(allowed, 0.0, NEG))
        _attend_tile(ks_ref[0, 0, kt], vsT_ref[0, 0, kt], bias, qrT, m_scr, l_scr, acc_scr)
        return carry

    lax.fori_loop(0, (q0 + Q_BLOCK + SEL_TILE - 1) // SEL_TILE, sel_body, 0)
    o_scr[...] = o_scr[...] + gate_row(1) * acc_scr[...] * (1.0 / l_scr[...])

    reset()

    def win_body(wt, carry):
        kp = wt * WIN_TILE + win_rows
        allowed = (kp <= qpos) & (kp >= qpos - WINDOW)
        bias = per_head(jnp.where(allowed, 0.0, NEG))
        _attend_tile(kw_ref[0, 0, wt], vwT_ref[0, 0, wt], bias, qrT, m_scr, l_scr, acc_scr)
        return carry

    lax.fori_loop(jnp.maximum(q0 - WINDOW, 0) // WIN_TILE, (q0 + Q_BLOCK + WIN_TILE - 1) // WIN_TILE, win_body, 0)
    o = o_scr[...] + gate_row(2) * acc_scr[...] * (1.0 / l_scr[...])
    cols = []
    for hp in range(HPG // 2):
        pair = jnp.concatenate([o[:, (2 * hp) * Q_BLOCK:(2 * hp + 1) * Q_BLOCK],
                                o[:, (2 * hp + 1) * Q_BLOCK:(2 * hp + 2) * Q_BLOCK]], axis=0)
        cols.append(pair.T)
    z = z_ref[0]
    o_ref[0] = (jnp.concatenate(cols, axis=1) * (z * _sigmoid(z))).astype(o_ref.dtype)


def nsa_prompt_attend(qT, qrT, gT, z, ck, cv, ks, vsT, kw, vwT):
    B, inner, T = qT.shape
    n_sel = T // SEL_BLOCK
    nc = ck.shape[1]
    n_cmp = -(-nc // LANES) * LANES
    grp = HPG * HEAD_DIM
    pad_c = ((0, 0), (0, n_cmp - nc), (0, 0), (0, 0))
    ck_p = jnp.pad(ck, pad_c).transpose(0, 2, 1, 3).astype(BF16)
    cvT = jnp.pad(cv, pad_c).transpose(0, 2, 3, 1).astype(BF16)
    cs = jnp.arange(n_cmp) * CMP_STRIDE
    ss = jnp.arange(n_sel) * SEL_BLOCK
    ovT = (jnp.clip(jnp.minimum(cs[None, :] + CMP_BLOCK, ss[:, None] + SEL_BLOCK)
                    - jnp.maximum(cs[None, :], ss[:, None]), 0, None).astype(F32) / CMP_BLOCK)
    ovT = jnp.where(jnp.arange(n_cmp)[None, :] < nc, ovT, 0.0).astype(BF16)

    def key_tiles(t, tile):
        return t.reshape(B, T // tile, tile, N_KV, HEAD_DIM).transpose(0, 3, 1, 2, 4)

    def val_tiles(t, tile):
        t = t.reshape(B, N_KV, HEAD_DIM, T // tile, tile).transpose(0, 1, 3, 2, 4)
        ones = jnp.ones(t.shape[:3] + (ONES_ROWS, tile), t.dtype)
        return jnp.concatenate([t, ones], axis=3)

    q_spec = pl.BlockSpec((1, grp, Q_BLOCK), lambda b, g, i: (b, g, i))
    row_spec = pl.BlockSpec((1, Q_BLOCK, grp), lambda b, g, i: (b, i, g))
    per_group = lambda shape: pl.BlockSpec((1, 1) + shape, lambda b, g, i: (b, g) + (0,) * len(shape))
    return pl.pallas_call(
        functools.partial(_nsa_prompt_kernel, n_cmp=n_cmp, n_sel=n_sel),
        grid=(B, N_KV, T // Q_BLOCK),
        in_specs=[q_spec, q_spec,
                  pl.BlockSpec((1, 3 * HPG, Q_BLOCK), lambda b, g, i: (b, g, i)),
                  row_spec,
                  per_group((n_cmp, HEAD_DIM)), per_group((HEAD_DIM, n_cmp)),
                  pl.BlockSpec((n_sel, n_cmp), lambda b, g, i: (0, 0)),
                  per_group((T // SEL_TILE, SEL_TILE, HEAD_DIM)), per_group((T // SEL_TILE, HEAD_DIM + ONES_ROWS, SEL_TILE)),
                  per_group((T // WIN_TILE, WIN_TILE, HEAD_DIM)), per_group((T // WIN_TILE, HEAD_DIM + ONES_ROWS, WIN_TILE))],
        out_specs=row_spec,
        out_shape=jax.ShapeDtypeStruct((B, T, inner), BF16),
        scratch_shapes=[pltpu.VMEM((n_sel, Q_BLOCK), F32),
                        pltpu.VMEM((1, HPG * Q_BLOCK), F32),
                        pltpu.VMEM((1, HPG * Q_BLOCK), F32),
                        pltpu.VMEM((HEAD_DIM, HPG * Q_BLOCK), F32),
                        pltpu.VMEM((HEAD_DIM, HPG * Q_BLOCK), F32)],
        compiler_params=pltpu.CompilerParams(
            dimension_semantics=("arbitrary", "arbitrary", "arbitrary"), vmem_limit_bytes=VMEM_LIMIT_BYTES),
        name="nsa_prompt_attend",
    )(qT, qrT, gT, z, ck_p, cvT, ovT, key_tiles(ks, SEL_TILE), val_tiles(vsT, SEL_TILE),
      key_tiles(kw, WIN_TILE), val_tiles(vwT, WIN_TILE))


def rms_norm(x, g):
    y = x * lax.rsqrt(jnp.mean(x * x, -1, keepdims=True) + NORM_EPS)
    return y * g


def rope(x, pos):
    half = ROT_DIM // 2
    inv = ROPE_THETA ** (-jnp.arange(half, dtype=F32) / half)
    ang = pos.astype(F32)[:, None] * inv[None, :]
    cos = jnp.cos(ang)[None, :, None, :]
    sin = jnp.sin(ang)[None, :, None, :]
    x1, x2, rest = x[..., :half], x[..., half:ROT_DIM], x[..., ROT_DIM:]
    return jnp.concatenate([x1 * cos - x2 * sin, x1 * sin + x2 * cos, rest], -1)


TOKEN_TILE = 256
SUBLANES = 8


def _bdot(a, b):
    return jnp.dot(a.astype(BF16), b.astype(BF16), preferred_element_type=F32)


def _split_dot(x, m, dims):
    hi = x.astype(BF16)
    lo = (x - hi.astype(F32)).astype(BF16)
    return (lax.dot_general(hi, m, (dims, ((), ())), preferred_element_type=F32)
            + lax.dot_general(lo, m, (dims, ((), ())), preferred_element_type=F32))


def _head_sum(x, seg):
    sums = _split_dot(x, seg, ((1,), (0,)))
    return _split_dot(sums, seg, ((1,), (1,)))


def _rms(x, g):
    return x * lax.rsqrt(jnp.mean(x * x, axis=-1, keepdims=True) + NORM_EPS) * g


def _normed_and_prev(h_ref, hprev_ref, shift_ref, g_ref):
    g = g_ref[...]
    xn = _rms(h_ref[0], g)
    prev_last = _rms(hprev_ref[0], g)[SUBLANES - 1:SUBLANES, :]
    first = jnp.where(pl.program_id(1) == 0, shift_ref[0], prev_last)
    row = lax.broadcasted_iota(jnp.int32, (xn.shape[0], 1), 0)
    return xn, jnp.where(row == 0, first, pltpu.roll(xn, 1, 0))


def _softplus(u):
    return jnp.maximum(u, 0.0) + jnp.log(1.0 + jnp.exp(-jnp.abs(u)))


def _sigmoid(u):
    return 1.0 / (1.0 + jnp.exp(-u))


def _rwkv_r_kernel(h_ref, hprev_ref, shift_ref, g_ref, mu_ref, w_ref, w0_ref, w1_ref, w2_ref, r_ref, ld_ref):
    xn, xp = _normed_and_prev(h_ref, hprev_ref, shift_ref, g_ref)
    dx = xp - xn
    r_ref[0] = _bdot(xn + dx * mu_ref[0:1, :], w_ref[...])
    lora = _bdot(jnp.tanh(_bdot(xn + dx * mu_ref[1:2, :], w1_ref[...])), w2_ref[...])
    w_log = -_softplus(-(w0_ref[...] + lora)) - 0.5
    ld_ref[0] = -jnp.exp(w_log)


def _rwkv_k_kernel(h_ref, hprev_ref, shift_ref, g_ref, mu_ref, w_ref, a0_ref, a1_ref, a2_ref, kk_w_ref, ka_ref,
                   seg_ref, k_ref, kk_ref, a_ref):
    xn, xp = _normed_and_prev(h_ref, hprev_ref, shift_ref, g_ref)
    dx = xp - xn
    k = _bdot(xn + dx * mu_ref[0:1, :], w_ref[...])
    a = _sigmoid(a0_ref[...] + _bdot(_bdot(xn + dx * mu_ref[1:2, :], a1_ref[...]), a2_ref[...]))
    kk = k * kk_w_ref[...]
    kk_ref[0] = kk * lax.rsqrt(jnp.maximum(_head_sum(kk * kk, seg_ref[...]), 1e-24))
    k_ref[0] = k * (1.0 + (a - 1.0) * ka_ref[...])
    a_ref[0] = a


def _rwkv_v_kernel(h_ref, hprev_ref, shift_ref, g_ref, mu_ref, w_ref, *rest, residual):
    xn, xp = _normed_and_prev(h_ref, hprev_ref, shift_ref, g_ref)
    mix = xn + (xp - xn) * mu_ref[0:1, :]
    v = _bdot(mix, w_ref[...])
    if residual:
        vfirst_ref, v0_ref, v1_ref, v2_ref, v_ref = rest
        v = v + (vfirst_ref[0] - v) * _sigmoid(v0_ref[...] + _bdot(_bdot(mix, v1_ref[...]), v2_ref[...]))
    else:
        (v_ref,) = rest
    v_ref[0] = v


def _rwkv_z_kernel(h_ref, hprev_ref, shift_ref, g_ref, mu_ref, w_ref, z_ref):
    xn, xp = _normed_and_prev(h_ref, hprev_ref, shift_ref, g_ref)
    z_ref[0] = _bdot(xn + (xp - xn) * mu_ref[0:1, :], w_ref[...])


def _rwkv_post_kernel(y_ref, r_ref, k_ref, v_ref, z_ref, h_ref, lnw_ref, lnb_ref, rk_ref, seg_ref, wout_ref, o_ref):
    seg = seg_ref[...]
    y = y_ref[0]
    d = y - _head_sum(y, seg) * (1.0 / RW_HEAD)
    var = _head_sum(d * d, seg) * (1.0 / RW_HEAD)
    yn = d * lax.rsqrt(var + LNX_EPS) * lnw_ref[...] + lnb_ref[...]
    yn = yn + _head_sum(r_ref[0] * k_ref[0] * rk_ref[...], seg) * v_ref[0]
    z = z_ref[0]
    o_ref[0] = h_ref[0] + _bdot(yn * (z * _sigmoid(z)), wout_ref[...])


def _head_indicator():
    return (jnp.arange(INNER)[:, None] // RW_HEAD == jnp.arange(LANES)[None, :]).astype(BF16)


def _token_call(kernel, operands, out_widths, name, out_dtype=F32):
    B, T = next(a.shape[:2] for kind, a in operands if kind == 'tile')
    tm = TOKEN_TILE if T % TOKEN_TILE == 0 else T
    per_tile = tm // SUBLANES
    specs = []
    for kind, a in operands:
        if kind == 'tile':
            specs.append(pl.BlockSpec((1, tm, a.shape[2]), lambda b, t: (b, t, 0)))
        elif kind == 'prev':
            specs.append(pl.BlockSpec((1, SUBLANES, a.shape[2]), lambda b, t: (b, jnp.maximum(t * per_tile - 1, 0), 0)))
        elif kind == 'batch':
            specs.append(pl.BlockSpec((1, 1, a.shape[2]), lambda b, t: (b, 0, 0)))
        else:
            specs.append(pl.BlockSpec(a.shape, lambda b, t, n=a.ndim: (0,) * n))
    return pl.pallas_call(
        kernel,
        grid=(B, T // tm),
        in_specs=specs,
        out_specs=[pl.BlockSpec((1, tm, w), lambda b, t: (b, t, 0)) for w in out_widths],
        out_shape=[jax.ShapeDtypeStruct((B, T, w), out_dtype) for w in out_widths],
        compiler_params=pltpu.CompilerParams(
            dimension_semantics=("arbitrary", "arbitrary"), vmem_limit_bytes=VMEM_LIMIT_BYTES),
        name=name,
    )(*[a for _, a in operands])


def rwkv_layer(h, shift_prev, s0, v_first, norm_g, mu, w_in, w_out, w0, w1, w2, a0, a1, a2, k_k, k_a, r_k,
               lnx_w, lnx_b, vres):
    B, T, D = h.shape
    row = lambda x: x.reshape(1, -1)
    bf = lambda x: x.astype(BF16)
    seg = _head_indicator()
    common = [('tile', h), ('prev', h), ('batch', shift_prev.reshape(B, 1, D)), ('const', row(norm_g))]
    r, logd = _token_call(
        _rwkv_r_kernel, common + [('const', mu[jnp.array([0, 4])]), ('const', bf(w_in[0])), ('const', row(w0)),
                                  ('const', bf(w1)), ('const', bf(w2))], [INNER, INNER], "rwkv_r")
    k, kk, a = _token_call(
        _rwkv_k_kernel, common + [('const', mu[jnp.array([1, 5])]), ('const', bf(w_in[1])), ('const', row(a0)),
                                  ('const', bf(a1)), ('const', bf(a2)), ('const', row(k_k)), ('const', row(k_a)),
                                  ('const', seg)], [INNER] * 3, "rwkv_k")
    if vres is None:
        (v,) = _token_call(functools.partial(_rwkv_v_kernel, residual=False),
                           common + [('const', mu[2:3]), ('const', bf(w_in[2]))], [INNER], "rwkv_v")
        v_first = v
    else:
        v0, v1, v2 = vres
        (v,) = _token_call(functools.partial(_rwkv_v_kernel, residual=True),
                           common + [('const', mu[2:3]), ('const', bf(w_in[2])), ('tile', v_first), ('const', row(v0)),
                                     ('const', bf(v1)), ('const', bf(v2))], [INNER], "rwkv_v")
    (z,) = _token_call(_rwkv_z_kernel, common + [('const', mu[3:4]), ('const', bf(w_in[3]))], [INNER], "rwkv_z")
    y, s_T = wkv_scan(r, logd, k, v, kk, a, s0, chunk=WKV_CHUNK if T % WKV_CHUNK == 0 else T,
                      t_block=WKV_T_BLOCK if T % WKV_T_BLOCK == 0 else T)
    (h_new,) = _token_call(
        _rwkv_post_kernel,
        [('tile', y), ('tile', r), ('tile', k), ('tile', v), ('tile', z), ('tile', h), ('const', row(lnx_w)),
         ('const', row(lnx_b)), ('const', r_k.reshape(1, INNER)), ('const', seg), ('const', bf(w_out))],
        [D], "rwkv_post")
    return h_new, v_first, s_T


ROT_HALF = ROT_DIM // 2


def _rope_rows(x, cos, sin_lo, sin_hi):
    out = []
    for c in range(x.shape[1] // LANES):
        xc = x[:, c * LANES:(c + 1) * LANES]
        out.append(xc * cos + pltpu.roll(xc, LANES - ROT_HALF, 1) * sin_lo + pltpu.roll(xc, ROT_HALF, 1) * sin_hi)
    return jnp.concatenate(out, axis=1)


def _rope_cols(x, cos, sin):
    n = x.shape[0] // HEAD_DIM
    x = x.reshape(n, HEAD_DIM, x.shape[1])
    x1, x2 = x[:, :ROT_HALF], x[:, ROT_HALF:ROT_DIM]
    y = jnp.concatenate([x1 * cos - x2 * sin, x1 * sin + x2 * cos, x[:, ROT_DIM:]], axis=1)
    return y.reshape(n * HEAD_DIM, y.shape[2])


def _nsa_rows_kernel(h_ref, g_ref, w_ref, cos_ref, slo_ref, shi_ref, cmp_ref, sel_ref, win_ref, z_ref, ks_ref, kw_ref):
    xn = _rms(h_ref[0], g_ref[...])
    p = _bdot(xn, w_ref[...])
    cos, slo, shi = cos_ref[0], slo_ref[0], shi_ref[0]
    cmp_ref[0] = p[:, :2 * KV_W]
    ks = _rope_rows(p[:, 2 * KV_W:3 * KV_W], cos, slo, shi)
    kw = _rope_rows(p[:, 4 * KV_W:5 * KV_W], cos, slo, shi)
    sel_ref[0] = jnp.concatenate([ks, p[:, 3 * KV_W:4 * KV_W]], axis=1)
    win_ref[0] = jnp.concatenate([kw, p[:, 5 * KV_W:6 * KV_W]], axis=1)
    z_ref[0] = p[:, 6 * KV_W:]
    ks_ref[0] = ks.astype(BF16)
    kw_ref[0] = kw.astype(BF16)


def _nsa_cols_kernel(h_ref, g_ref, wT_ref, cos_ref, sin_ref, qT_ref, qrT_ref, vsT_ref, vwT_ref, gT_ref):
    xn = _rms(h_ref[0], g_ref[...]).astype(BF16)
    pT = lax.dot_general(wT_ref[...], xn, (((1,), (1,)), ((), ())), preferred_element_type=F32)
    q = pT[:INNER]
    qT_ref[0] = q.astype(BF16)
    qrT_ref[0] = (_rope_cols(q, cos_ref[...], sin_ref[...]) * LOG2_E).astype(BF16)
    vsT_ref[0] = pT[INNER:INNER + KV_W].astype(BF16)
    vwT_ref[0] = pT[INNER + KV_W:INNER + 2 * KV_W].astype(BF16)
    gT_ref[0] = _sigmoid(pT[INNER + 2 * KV_W:])


def _rope_tables(pos):
    inv = ROPE_THETA ** (-jnp.arange(ROT_HALF, dtype=F32) / ROT_HALF)
    ang = pos.astype(F32)[:, None] * inv[None, :]
    cos, sin = jnp.cos(ang), jnp.sin(ang)
    lane = jnp.arange(LANES) % HEAD_DIM
    f = lane % ROT_HALF
    cos_l = jnp.where(lane[None, :] < ROT_DIM, cos[:, f], 1.0)
    slo_l = jnp.where(lane[None, :] < ROT_HALF, -sin[:, f], 0.0)
    shi_l = jnp.where((lane[None, :] >= ROT_HALF) & (lane[None, :] < ROT_DIM), sin[:, f], 0.0)
    return cos_l, slo_l, shi_l, cos.T, sin.T


def nsa_pre(h, norm_g, w_in, pos):
    B, T, D = h.shape
    tm = TOKEN_TILE if T % TOKEN_TILE == 0 else T
    cos_l, slo_l, shi_l, cosT, sinT = _rope_tables(pos)
    scale = HEAD_DIM ** -0.5
    g_row = norm_g.reshape(1, D)
    w_rows = jnp.concatenate([w_in[:, Q_END:KV_END], w_in[:, G_END:]], axis=1).astype(BF16)
    w_g = w_in[:, KV_END:G_END].reshape(D, N_KV, HPG, 3).transpose(0, 1, 3, 2).reshape(D, 3 * N_HEADS)
    w_cols = jnp.concatenate([w_in[:, :Q_END] * scale, w_in[:, Q_END + 3 * KV_W:Q_END + 4 * KV_W],
                              w_in[:, Q_END + 5 * KV_W:Q_END + 6 * KV_W], w_g], axis=1).T.astype(BF16)
    tile = lambda w: pl.BlockSpec((1, tm, w), lambda b, t: (b, t, 0))
    whole = lambda a: pl.BlockSpec(a.shape, lambda b, t, n=a.ndim: (0,) * n)
    tab = pl.BlockSpec((1, tm, LANES), lambda b, t: (0, t, 0))
    params = pltpu.CompilerParams(dimension_semantics=("arbitrary", "arbitrary"), vmem_limit_bytes=VMEM_LIMIT_BYTES)
    cmp_rows, sel_rows, win_rows, z, ks, kw = pl.pallas_call(
        _nsa_rows_kernel,
        grid=(B, T // tm),
        in_specs=[tile(D), whole(g_row), whole(w_rows), tab, tab, tab],
        out_specs=[tile(2 * KV_W), tile(2 * KV_W), tile(2 * KV_W), tile(INNER), tile(KV_W), tile(KV_W)],
        out_shape=[jax.ShapeDtypeStruct((B, T, 2 * KV_W), F32)] * 3 + [jax.ShapeDtypeStruct((B, T, INNER), F32)]
        + [jax.ShapeDtypeStruct((B, T, KV_W), BF16)] * 2,
        compiler_params=params, name="nsa_rows",
    )(h, g_row, w_rows, cos_l[None], slo_l[None], shi_l[None])
    colt = lambda r: pl.BlockSpec((1, r, tm), lambda b, t: (b, 0, t))
    tabT = pl.BlockSpec((ROT_HALF, tm), lambda b, t: (0, t))
    qT, qrT, vsT, vwT, gT = pl.pallas_call(
        _nsa_cols_kernel,
        grid=(B, T // tm),
        in_specs=[tile(D), whole(g_row), whole(w_cols), tabT, tabT],
        out_specs=[colt(INNER), colt(INNER), colt(KV_W), colt(KV_W), colt(3 * N_HEADS)],
        out_shape=[jax.ShapeDtypeStruct((B, INNER, T), BF16)] * 2 + [jax.ShapeDtypeStruct((B, KV_W, T), BF16)] * 2
        + [jax.ShapeDtypeStruct((B, 3 * N_HEADS, T), F32)],
        compiler_params=params, name="nsa_cols",
    )(h, g_row, w_cols, cosT, sinT)
    return cmp_rows, sel_rows, win_rows, z, ks, kw, qT, qrT, vsT, vwT, gT


def nsa_project(x, w_in, pos):
    B, T, _ = x.shape
    p = proj(x, w_in[:, :KV_END])
    q = p[..., :Q_END].reshape(B, T, N_HEADS, HEAD_DIM)
    kv = p[..., Q_END:KV_END].reshape(B, T, 6, N_KV, HEAD_DIM)
    gates = jax.nn.sigmoid(x @ w_in[:, KV_END:G_END]).reshape(B, T, N_HEADS, 3)
    z = proj(x, w_in[:, G_END:])
    kc, vc, ks, vs, kw, vw = (kv[:, :, i] for i in range(6))
    return q, rope(q, pos), gates, z, kc, vc, rope(ks, pos), vs, rope(kw, pos), vw


def compress(k, v, pe, w1, w2):
    def phi(t, a, b):
        B, T = t.shape[:2]
        n_chunk = T // CMP_STRIDE
        nc = n_chunk - CMP_RATIO + 1
        chunks = t[:, :n_chunk * CMP_STRIDE].reshape(B, n_chunk, CMP_STRIDE, N_KV, HEAD_DIM)
        parts = []
        for m in range(CMP_RATIO):
            sl = slice(m * CMP_STRIDE, (m + 1) * CMP_STRIDE)
            part = jnp.einsum('bcsgd,sdh->bcgh', chunks + pe[None, None, sl, None, :], a[sl])
            parts.append(part[:, m:m + nc])
        h = jax.nn.silu(sum(parts))
        return jnp.einsum('bngh,hd->bngd', h, b)
    ck = phi(k, w1[0], w2[0])
    cv = phi(v, w1[1], w2[1])
    c_end = jnp.arange(ck.shape[1]) * CMP_STRIDE + CMP_BLOCK - 1
    return ck, cv, c_end


def to_sel_blocks(t):
    B, T = t.shape[:2]
    ns = -(-T // SEL_BLOCK)
    t = jnp.pad(t, ((0, 0), (0, ns * SEL_BLOCK - T), (0, 0), (0, 0)))
    return t.reshape(B, ns, SEL_BLOCK, N_KV, HEAD_DIM).transpose(0, 3, 1, 2, 4)


def nsa_attend(q, qr, gates, q_pos, ck, cv, c_end, sk_b, sv_b, wk, wv, w_pos):
    B, Q = q.shape[:2]
    scale = HEAD_DIM ** -0.5
    qg = q.reshape(B, Q, N_KV, HPG, HEAD_DIM)
    qrg = qr.reshape(B, Q, N_KV, HPG, HEAD_DIM)
    s_c = jnp.einsum('bqghd,bngd->bqghn', qg, ck) * scale
    m_c = (c_end[None, :] <= q_pos[:, None])[None, :, None, None, :]
    p_c = jnp.where(m_c, jax.nn.softmax(jnp.where(m_c, s_c, NEG), -1), 0.0)
    o_c = jnp.einsum('bqghn,bngd->bqghd', p_c, cv)
    nc, ns = ck.shape[1], sk_b.shape[2]
    cs = jnp.arange(nc) * CMP_STRIDE
    ss = jnp.arange(ns) * SEL_BLOCK
    overlap = jnp.clip(jnp.minimum(cs[:, None] + CMP_BLOCK, ss[None, :] + SEL_BLOCK)
                       - jnp.maximum(cs[:, None], ss[None, :]), 0, None).astype(F32) / CMP_BLOCK
    imp = jnp.einsum('bqghn,nj->bqgj', p_c, overlap)
    cur = q_pos // SEL_BLOCK
    jb = jnp.arange(ns)
    valid = jb[None, :] <= cur[:, None]
    forced = (jb[None, :] == 0) | (jb[None, :] == cur[:, None]) | (jb[None, :] == cur[:, None] - 1)
    score = jnp.where(valid[None, :, None, :], imp + FORCE_BONUS * forced[None, :, None, :].astype(F32), NEG)
    n_top = min(TOP_N, ns)
    top_v, top_i = lax.top_k(score, n_top)
    ok = top_v > NEG / 2
    bi = jnp.arange(B)[:, None, None, None]
    gi = jnp.arange(N_KV)[None, None, :, None]
    gk = sk_b[bi, gi, top_i]
    gv = sv_b[bi, gi, top_i]
    kpos = top_i[..., None] * SEL_BLOCK + jnp.arange(SEL_BLOCK)
    m_s = (ok[..., None] & (kpos <= q_pos[None, :, None, None, None]))[:, :, :, None]
    s_s = jnp.einsum('bqghd,bqgnsd->bqghns', qrg, gk) * scale
    s_s = jnp.where(m_s, s_s, NEG).reshape(B, Q, N_KV, HPG, n_top * SEL_BLOCK)
    p_s = jax.nn.softmax(s_s, -1).reshape(B, Q, N_KV, HPG, n_top, SEL_BLOCK)
    o_s = jnp.einsum('bqghns,bqgnsd->bqghd', p_s, gv)
    s_w = jnp.einsum('bqghd,bwgd->bqghw', qrg, wk) * scale
    m_w = ((w_pos[None, :] <= q_pos[:, None]) & (w_pos[None, :] >= q_pos[:, None] - WINDOW)
           & (w_pos[None, :] >= 0))[None, :, None, None, :]
    p_w = jax.nn.softmax(jnp.where(m_w, s_w, NEG), -1)
    o_w = jnp.einsum('bqghw,bwgd->bqghd', p_w, wv)
    g = gates.reshape(B, Q, N_KV, HPG, 3)
    o = g[..., 0:1] * o_c + g[..., 1:2] * o_s + g[..., 2:3] * o_w
    return o.reshape(B, Q, INNER)


def _residual_matmul_kernel(x_ref, h_ref, w_ref, o_ref):
    o_ref[0] = h_ref[0] + jnp.dot(x_ref[0], w_ref[...], preferred_element_type=F32)


def nsa_prompt(h, norm_g, w_in, w_out, pe, cw1, cw2):
    B, T, _ = h.shape
    cmp_rows, sel_rows, win_rows, z, ks, kw, qT, qrT, vsT, vwT, gT = nsa_pre(h, norm_g, w_in, jnp.arange(T))
    heads = lambda t: t.reshape(B, T, N_KV, HEAD_DIM)
    ck, cv, _ = compress(heads(cmp_rows[..., :KV_W]), heads(cmp_rows[..., KV_W:]), pe, cw1, cw2)
    gated = nsa_prompt_attend(qT, qrT, gT, z, ck, cv, ks, vsT, kw, vwT)
    (h_new,) = _token_call(_residual_matmul_kernel, [('tile', gated), ('tile', h), ('const', w_out.astype(BF16))],
                           [D_MODEL], "nsa_out")
    n_keep = min(WINDOW, T)
    rows = lambda t: t.reshape(B, t.shape[1], 2, N_KV, HEAD_DIM)
    return h_new, rows(cmp_rows), rows(sel_rows), rows(win_rows[:, -n_keep:])


def nsa_sample(x, pool_cmp, pool_sel, win_buf, page_table, w_in, w_out, pe, cw1, cw2):
    B, T, _ = x.shape
    past = page_table.shape[1] * pool_cmp.shape[1]
    pos = past + jnp.arange(T)
    q, qr, gates, z, kc, vc, ks, vs, kw, vw = nsa_project(x, w_in, pos)

    def gather(pool):
        return pool[page_table].reshape(B, past, 2, N_KV, HEAD_DIM)

    pc, ps = gather(pool_cmp), gather(pool_sel)
    ck, cv, c_end = compress(jnp.concatenate([pc[:, :, 0], kc], 1),
                             jnp.concatenate([pc[:, :, 1], vc], 1), pe, cw1, cw2)
    sk_b = to_sel_blocks(jnp.concatenate([ps[:, :, 0], ks], 1))
    sv_b = to_sel_blocks(jnp.concatenate([ps[:, :, 1], vs], 1))
    n_buf = win_buf.shape[1]
    wk = jnp.concatenate([win_buf[:, :, 0], kw], 1)
    wv = jnp.concatenate([win_buf[:, :, 1], vw], 1)
    w_pos = past - n_buf + jnp.arange(n_buf + T)
    o = nsa_attend(q, qr, gates, pos, ck, cv, c_end, sk_b, sv_b, wk, wv, w_pos)
    y = proj(o * jax.nn.silu(z), w_out)
    n_keep = min(WINDOW, n_buf + T)
    return (y, jnp.stack([kc, vc], 2), jnp.stack([ks, vs], 2),
            jnp.stack([wk[:, -n_keep:], wv[:, -n_keep:]], 2))


def kernel(x_prompt, x_sample, state_rwkv_wkv, state_rwkv_shift, cache_nsa_cmp, cache_nsa_sel, cache_nsa_win, page_table, norm_g, final_norm_g, rwkv_mu, rwkv_w_in, rwkv_w_out, rwkv_w0, rwkv_w1, rwkv_w2, rwkv_a0, rwkv_a1, rwkv_a2, rwkv_v0, rwkv_v1, rwkv_v2, rwkv_k_k, rwkv_k_a, rwkv_r_k, rwkv_lnx_w, rwkv_lnx_b, nsa_w_in, nsa_w_out, nsa_cmp_pe, nsa_cmp_w1, nsa_cmp_w2):

    def trunk(x, wkv0, shift0, sample):
        h = x
        v_first = None
        wkv, shift, cmp_rows, sel_rows, win_rows = [], [], [], [], []
        for layer in range(DEPTH):
            j = layer // N_MIXERS
            if layer % N_MIXERS == 0:
                vres = None if j == 0 else (rwkv_v0[j - 1], rwkv_v1[j - 1], rwkv_v2[j - 1])
                shift.append(rms_norm(h[:, -1], norm_g[layer]))
                h, v_first, s_T = rwkv_layer(
                    h, shift0[j], wkv0[j], v_first, norm_g[layer], rwkv_mu[j], rwkv_w_in[j], rwkv_w_out[j],
                    rwkv_w0[j], rwkv_w1[j], rwkv_w2[j], rwkv_a0[j], rwkv_a1[j], rwkv_a2[j],
                    rwkv_k_k[j], rwkv_k_a[j], rwkv_r_k[j], rwkv_lnx_w[j], rwkv_lnx_b[j], vres)
                wkv.append(s_T)
            else:
                if sample:
                    xn = rms_norm(h, norm_g[layer])
                    o, c, s, w = nsa_sample(xn, cache_nsa_cmp[j], cache_nsa_sel[j], cache_nsa_win[j],
                                            page_table, nsa_w_in[j], nsa_w_out[j], nsa_cmp_pe[j],
                                            nsa_cmp_w1[j], nsa_cmp_w2[j])
                    h = h + o
                else:
                    h, c, s, w = nsa_prompt(h, norm_g[layer], nsa_w_in[j], nsa_w_out[j], nsa_cmp_pe[j],
                                            nsa_cmp_w1[j], nsa_cmp_w2[j])
                cmp_rows.append(c)
                sel_rows.append(s)
                win_rows.append(w)
        return (rms_norm(h, final_norm_g), jnp.stack(wkv), jnp.stack(shift),
                jnp.stack(cmp_rows), jnp.stack(sel_rows), jnp.stack(win_rows))

    zeros_wkv = jnp.zeros((rwkv_mu.shape[0], x_prompt.shape[0], RW_HEADS, RW_HEAD, RW_HEAD), x_prompt.dtype)
    zeros_shift = jnp.zeros((rwkv_mu.shape[0], x_prompt.shape[0], D_MODEL), x_prompt.dtype)
    y_prompt, wkv_p, shift_p, cmp_p, sel_p, win_p = trunk(x_prompt, zeros_wkv, zeros_shift, False)
    y_sample, wkv_s, shift_s, cmp_s, sel_s, win_s = trunk(x_sample, state_rwkv_wkv, state_rwkv_shift, True)
    return (y_prompt, y_sample, wkv_p, wkv_s, shift_p, shift_s, cmp_p, cmp_s, sel_p, sel_s, win_p, win_s)
```

```python
import functools

import jax
import jax.numpy as jnp
from jax import lax
from jax.experimental import pallas as pl
from jax.experimental.pallas import tpu as pltpu

D_MODEL = 1024
DEPTH = 4
N_MIXERS = 2
INNER = 2 * D_MODEL
NORM_EPS = 1e-6
RW_HEAD = 64
RW_HEADS = INNER // RW_HEAD
LNX_EPS = 64e-5
HEAD_DIM = 64
N_HEADS = INNER // HEAD_DIM
N_KV = 4
HPG = N_HEADS // N_KV
KV_W = N_KV * HEAD_DIM
ROT_DIM = HEAD_DIM // 4
ROPE_THETA = 500000.0
CMP_BLOCK = 32
CMP_STRIDE = 16
CMP_RATIO = CMP_BLOCK // CMP_STRIDE
SEL_BLOCK = 64
TOP_N = 16
WINDOW = 512
Q_BLOCK = 128
NEG = -1e30
FORCE_BONUS = 1e4
Q_END = INNER
KV_END = Q_END + 6 * KV_W
G_END = KV_END + 3 * N_HEADS
NSA_IN = G_END + INNER

F32 = jnp.float32
BF16 = jnp.bfloat16
VMEM_LIMIT_BYTES = 48 * 1024 * 1024


LANES = 128
WKV_CHUNK = 64
WKV_T_BLOCK = 512
WKV_PAIRS = 8


def _dot(a, b):
    return jnp.dot(a.astype(BF16), b.astype(BF16), preferred_element_type=F32)


def _dot_nt(a, b):
    return lax.dot_general(a.astype(BF16), b.astype(BF16), (((1,), (1,)), ((), ())), preferred_element_type=F32)


def _dot_tn(a, b):
    return lax.dot_general(a.astype(BF16), b.astype(BF16), (((0,), (0,)), ((), ())), preferred_element_type=F32)


def _wkv_kernel(r_ref, ld_ref, k_ref, v_ref, kk_ref, a_ref, s0_ref, y_ref, st_ref, s_scr, *, chunk, n_chunks, n_pairs):
    C = chunk
    R = 2 * C
    tb = pl.program_id(2)

    @pl.when(tb == 0)
    def _():
        z = jnp.zeros((RW_HEAD, RW_HEAD), F32)
        for g in range(n_pairs):
            s_scr[g] = jnp.concatenate(
                [jnp.concatenate([s0_ref[0, 2 * g], z], axis=1),
                 jnp.concatenate([z, s0_ref[0, 2 * g + 1]], axis=1)], axis=0)

    lane = lax.broadcasted_iota(jnp.int32, (1, LANES), 1)
    head_a = lane < RW_HEAD
    row = lax.broadcasted_iota(jnp.int32, (R, R), 0)
    col = lax.broadcasted_iota(jnp.int32, (R, R), 1)
    same = (row // C) == (col // C)
    strict = same & ((col % C) < (row % C))
    incl = same & ((col % C) <= (row % C))
    eye = (row == col).astype(F32)
    tr = lax.broadcasted_iota(jnp.int32, (C, C), 0)
    tc = lax.broadcasted_iota(jnp.int32, (C, C), 1)
    tri = (tc <= tr).astype(BF16)

    def stack(z):
        return jnp.concatenate([jnp.where(head_a, z, 0.0), jnp.where(head_a, 0.0, z)], axis=0)

    def body(ci, carry):
        sl = pl.ds(pl.multiple_of(ci * C, C), C)
        G = range(n_pairs)
        ld = [ld_ref[0, sl, g * LANES:(g + 1) * LANES] for g in G]
        ld_hi = [x.astype(BF16) for x in ld]
        ld_lo = [(x - h.astype(F32)).astype(BF16) for x, h in zip(ld, ld_hi)]
        cum = [jnp.dot(tri, h, preferred_element_type=F32) + jnp.dot(tri, l, preferred_element_type=F32)
               for h, l in zip(ld_hi, ld_lo)]
        e_neg = [jnp.exp(-c) for c in cum]
        kk = [kk_ref[0, sl, g * LANES:(g + 1) * LANES] for g in G]
        a_s = [stack(-kk[g] * jnp.exp(cum[g] - ld[g])) for g in G]
        r_s = [stack(r_ref[0, sl, g * LANES:(g + 1) * LANES] * jnp.exp(cum[g])) for g in G]
        b_s = [stack(kk[g] * a_ref[0, sl, g * LANES:(g + 1) * LANES] * e_neg[g]) for g in G]
        k_s = [stack(k_ref[0, sl, g * LANES:(g + 1) * LANES] * e_neg[g]) for g in G]
        v_s = [stack(v_ref[0, sl, g * LANES:(g + 1) * LANES]) for g in G]
        s2 = [s_scr[g] for g in G]
        ab = [jnp.where(strict, _dot_nt(a_s[g], b_s[g]), 0.0) for g in G]
        ak = [jnp.where(strict, _dot_nt(a_s[g], k_s[g]), 0.0) for g in G]
        rb = [jnp.where(incl, _dot_nt(r_s[g], b_s[g]), 0.0) for g in G]
        rk = [jnp.where(incl, _dot_nt(r_s[g], k_s[g]), 0.0) for g in G]
        rhs = [_dot_nt(a_s[g], s2[g]) + _dot(ak[g], v_s[g]) for g in G]
        y0 = [_dot_nt(r_s[g], s2[g]) + _dot(rk[g], v_s[g]) for g in G]
        tm = [eye + ab[g] for g in G]
        p = ab
        n = 2
        while n < C:
            p = [_dot(p[g], p[g]) for g in G]
            tm = [tm[g] + _dot(tm[g], p[g]) for g in G]
            n *= 2
        u_s = [_dot(tm[g], rhs[g]) for g in G]
        y_s = [y0[g] + _dot(rb[g], u_s[g]) for g in G]
        for g in G:
            y_ref[0, sl, g * LANES:(g + 1) * LANES] = y_s[g][:C] + y_s[g][C:]
            gamma = jnp.exp(cum[g][C - 1:C, :])
            s_scr[g] = (s2[g] + _dot_tn(u_s[g], b_s[g]) + _dot_tn(v_s[g], k_s[g])) * gamma
        return carry

    lax.fori_loop(0, n_chunks, body, 0)

    @pl.when(tb == pl.num_programs(2) - 1)
    def _():
        for g in range(n_pairs):
            st_ref[0, 2 * g] = s_scr[g, :RW_HEAD, :RW_HEAD]
            st_ref[0, 2 * g + 1] = s_scr[g, RW_HEAD:, RW_HEAD:]


def wkv_scan(r, logd, k, v, kk, a, s0, *, chunk, t_block):
    B, T, inner = r.shape
    width = WKV_PAIRS * LANES
    assert T % t_block == 0 and t_block % chunk == 0 and inner % width == 0
    seq_spec = pl.BlockSpec((1, t_block, width), lambda b, p, t: (b, t, p))
    st_spec = pl.BlockSpec((1, 2 * WKV_PAIRS, RW_HEAD, RW_HEAD), lambda b, p, t: (b, p, 0, 0))
    return pl.pallas_call(
        functools.partial(_wkv_kernel, chunk=chunk, n_chunks=t_block // chunk, n_pairs=WKV_PAIRS),
        grid=(B, inner // width, T // t_block),
        in_specs=[seq_spec] * 6 + [st_spec],
        out_specs=[seq_spec, st_spec],
        out_shape=[jax.ShapeDtypeStruct((B, T, inner), F32), jax.ShapeDtypeStruct(s0.shape, F32)],
        scratch_shapes=[pltpu.VMEM((WKV_PAIRS, LANES, LANES), F32)],
        compiler_params=pltpu.CompilerParams(
            dimension_semantics=("arbitrary", "arbitrary", "arbitrary"), vmem_limit_bytes=VMEM_LIMIT_BYTES),
        name="wkv_scan",
    )(r, logd, k, v, kk, a, s0)


SEL_TILE = 512
WIN_TILE = 256
BLOCKS_PER_TILE = SEL_TILE // SEL_BLOCK
LANE_GROUPS = 2
ONES_ROWS = 16
LOG2_E = 1.4426950408889634
PICKED = -3e38


def _attend_tile(k_tile, vT_tile, bias, qT, m_ref, l_ref, acc_ref):
    width = qT.shape[1] // LANE_GROUPS
    cols = [slice(c * width, (c + 1) * width) for c in range(LANE_GROUPS)]
    s = [jnp.dot(k_tile, qT[:, c], preferred_element_type=F32) + bias[:, c] for c in cols]
    m_old = [m_ref[:, c] for c in cols]
    m_new = [jnp.maximum(mo, jnp.max(sc, axis=0, keepdims=True)) for mo, sc in zip(m_old, s)]
    p = [jnp.exp2(sc - mn).astype(BF16) for sc, mn in zip(s, m_new)]
    pv = [jnp.dot(vT_tile, pc, preferred_element_type=F32) for pc in p]
    for c, mo, mn, pvc in zip(cols, m_old, m_new, pv):
        alpha = jnp.exp2(mo - mn)
        l_ref[:, c] = alpha * l_ref[:, c] + pvc[HEAD_DIM:HEAD_DIM + 1, :]
        acc_ref[:, c] = alpha * acc_ref[:, c] + pvc[:HEAD_DIM, :]
        m_ref[:, c] = mn


def _nsa_prompt_kernel(qT_ref, qrT_ref, gT_ref, z_ref, ck_ref, cvT_ref, ovT_ref, ks_ref, vsT_ref, kw_ref, vwT_ref,
                       o_ref, sel_scr, m_scr, l_scr, acc_scr, o_scr, *, n_cmp, n_sel):
    i = pl.program_id(2)
    q0 = i * Q_BLOCK
    qpos = q0 + lax.broadcasted_iota(jnp.int32, (1, Q_BLOCK), 1)

    def heads_on_lanes(ref):
        return jnp.concatenate([ref[0, h * HEAD_DIM:(h + 1) * HEAD_DIM, :] for h in range(HPG)], axis=1)

    def gate_row(br):
        return jnp.concatenate([gT_ref[0, br * HPG + h:br * HPG + h + 1, :] for h in range(HPG)], axis=1)

    def per_head(x):
        return jnp.concatenate([x] * HPG, axis=1)

    n_iota = lax.broadcasted_iota(jnp.int32, (n_cmp, Q_BLOCK), 0)
    c_valid = per_head((n_iota * CMP_STRIDE + (CMP_BLOCK - 1)) <= qpos)
    s = jnp.dot(ck_ref[0, 0], heads_on_lanes(qT_ref), preferred_element_type=F32)
    s = jnp.where(c_valid, s, NEG)
    mx = jnp.max(s, axis=0, keepdims=True)
    e = jnp.where(c_valid, jnp.exp(s - mx), 0.0)
    l = jnp.sum(e, axis=0, keepdims=True)
    p = e * (1.0 / jnp.where(l > 0.0, l, 1.0))
    o_scr[...] = gate_row(0) * jnp.dot(cvT_ref[0, 0], p.astype(BF16), preferred_element_type=F32)
    p_sum = p[:, :Q_BLOCK]
    for h in range(1, HPG):
        p_sum = p_sum + p[:, h * Q_BLOCK:(h + 1) * Q_BLOCK]

    ps_hi = p_sum.astype(BF16)
    ps_lo = (p_sum - ps_hi.astype(F32)).astype(BF16)
    ovT = ovT_ref[...]
    imp = jnp.dot(ovT, ps_hi, preferred_element_type=F32) + jnp.dot(ovT, ps_lo, preferred_element_type=F32)
    j_iota = lax.broadcasted_iota(jnp.int32, (n_sel, Q_BLOCK), 0)
    cur = qpos // SEL_BLOCK
    forced = (j_iota == 0) | (j_iota == cur) | (j_iota == cur - 1)
    score = jnp.where(j_iota <= cur, imp + jnp.where(forced, FORCE_BONUS, 0.0), NEG)
    sel = jnp.zeros((n_sel, Q_BLOCK), F32)
    for _ in range(TOP_N):
        mx = jnp.max(score, axis=0, keepdims=True)
        first = jnp.min(jnp.where(score == mx, j_iota, n_sel), axis=0, keepdims=True)
        pick = j_iota == first
        sel = jnp.where(pick & (mx > NEG / 2), 1.0, sel)
        score = jnp.where(pick, PICKED, score)
    sel_scr[...] = sel

    def reset():
        m_scr[...] = jnp.full(m_scr.shape, NEG, F32)
        l_scr[...] = jnp.zeros(l_scr.shape, F32)
        acc_scr[...] = jnp.zeros(acc_scr.shape, F32)

    qrT = heads_on_lanes(qrT_ref)
    sel_rows = lax.broadcasted_iota(jnp.int32, (SEL_TILE, Q_BLOCK), 0)
    win_rows = lax.broadcasted_iota(jnp.int32, (WIN_TILE, Q_BLOCK), 0)

    reset()

    def sel_body(kt, carry):
        chosen = sel_scr[pl.ds(pl.multiple_of(kt * BLOCKS_PER_TILE, BLOCKS_PER_TILE), BLOCKS_PER_TILE), :]
        chosen = jnp.concatenate(
            [jnp.broadcast_to(chosen[b:b + 1, :], (SEL_BLOCK, Q_BLOCK)) for b in range(BLOCKS_PER_TILE)], axis=0)
        allowed = (chosen > 0.5) & ((kt * SEL_TILE + sel_rows) <= qpos)
        bias = per_head(jnp.where(allowed, 0.0, NEG))
        _attend_tile(ks_ref[0, 0, kt], vsT_ref[0, 0, kt], bias, qrT, m_scr, l_scr, acc_scr)
        return carry

    lax.fori_loop(0, (q0 + Q_BLOCK + SEL_TILE - 1) // SEL_TILE, sel_body, 0)
    o_scr[...] = o_scr[...] + gate_row(1) * acc_scr[...] * (1.0 / l_scr[...])

    reset()

    def win_body(wt, carry):
        kp = wt * WIN_TILE + win_rows
        allowed = (kp <= qpos) & (kp >= qpos - WINDOW)
        bias = per_head(jnp.where(allowed, 0.0, NEG))
        _attend_tile(kw_ref[0, 0, wt], vwT_ref[0, 0, wt], bias, qrT, m_scr, l_scr, acc_scr)
        return carry

    lax.fori_loop(jnp.maximum(q0 - WINDOW, 0) // WIN_TILE, (q0 + Q_BLOCK + WIN_TILE - 1) // WIN_TILE, win_body, 0)
    o = o_scr[...] + gate_row(2) * acc_scr[...] * (1.0 / l_scr[...])
    cols = []
    for hp in range(HPG // 2):
        pair = jnp.concatenate([o[:, (2 * hp) * Q_BLOCK:(2 * hp + 1) * Q_BLOCK],
                                o[:, (2 * hp + 1) * Q_BLOCK:(2 * hp + 2) * Q_BLOCK]], axis=0)
        cols.append(pair.T)
    z = z_ref[0]
    o_ref[0] = (jnp.concatenate(cols, axis=1) * (z * _sigmoid(z))).astype(o_ref.dtype)


def nsa_prompt_attend(qT, qrT, gT, z, ck, cv, ks, vsT, kw, vwT):
    B, inner, T = qT.shape
    n_sel = T // SEL_BLOCK
    nc = ck.shape[1]
    n_cmp = -(-nc // LANES) * LANES
    grp = HPG * HEAD_DIM
    pad_c = ((0, 0), (0, n_cmp - nc), (0, 0), (0, 0))
    ck_p = jnp.pad(ck, pad_c).transpose(0, 2, 1, 3).astype(BF16)
    cvT = jnp.pad(cv, pad_c).transpose(0, 2, 3, 1).astype(BF16)
    cs = jnp.arange(n_cmp) * CMP_STRIDE
    ss = jnp.arange(n_sel) * SEL_BLOCK
    ovT = (jnp.clip(jnp.minimum(cs[None, :] + CMP_BLOCK, ss[:, None] + SEL_BLOCK)
                    - jnp.maximum(cs[None, :], ss[:, None]), 0, None).astype(F32) / CMP_BLOCK)
    ovT = jnp.where(jnp.arange(n_cmp)[None, :] < nc, ovT, 0.0).astype(BF16)

    def key_tiles(t, tile):
        return t.reshape(B, T // tile, tile, N_KV, HEAD_DIM).transpose(0, 3, 1, 2, 4)

    def val_tiles(t, tile):
        t = t.reshape(B, N_KV, HEAD_DIM, T // tile, tile).transpose(0, 1, 3, 2, 4)
        ones = jnp.ones(t.shape[:3] + (ONES_ROWS, tile), t.dtype)
        return jnp.concatenate([t, ones], axis=3)

    q_spec = pl.BlockSpec((1, grp, Q_BLOCK), lambda b, g, i: (b, g, i))
    row_spec = pl.BlockSpec((1, Q_BLOCK, grp), lambda b, g, i: (b, i, g))
    per_group = lambda shape: pl.BlockSpec((1, 1) + shape, lambda b, g, i: (b, g) + (0,) * len(shape))
    return pl.pallas_call(
        functools.partial(_nsa_prompt_kernel, n_cmp=n_cmp, n_sel=n_sel),
        grid=(B, N_KV, T // Q_BLOCK),
        in_specs=[q_spec, q_spec,
                  pl.BlockSpec((1, 3 * HPG, Q_BLOCK), lambda b, g, i: (b, g, i)),
                  row_spec,
                  per_group((n_cmp, HEAD_DIM)), per_group((HEAD_DIM, n_cmp)),
                  pl.BlockSpec((n_sel, n_cmp), lambda b, g, i: (0, 0)),
                  per_group((T // SEL_TILE, SEL_TILE, HEAD_DIM)), per_group((T // SEL_TILE, HEAD_DIM + ONES_ROWS, SEL_TILE)),
                  per_group((T // WIN_TILE, WIN_TILE, HEAD_DIM)), per_group((T // WIN_TILE, HEAD_DIM + ONES_ROWS, WIN_TILE))],
        out_specs=row_spec,
        out_shape=jax.ShapeDtypeStruct((B, T, inner), BF16),
        scratch_shapes=[pltpu.VMEM((n_sel, Q_BLOCK), F32),
                        pltpu.VMEM((1, HPG * Q_BLOCK), F32),
                        pltpu.VMEM((1, HPG * Q_BLOCK), F32),
                        pltpu.VMEM((HEAD_DIM, HPG * Q_BLOCK), F32),
                        pltpu.VMEM((HEAD_DIM, HPG * Q_BLOCK), F32)],
        compiler_params=pltpu.CompilerParams(
            dimension_semantics=("arbitrary", "arbitrary", "arbitrary"), vmem_limit_bytes=VMEM_LIMIT_BYTES),
        name="nsa_prompt_attend",
    )(qT, qrT, gT, z, ck_p, cvT, ovT, key_tiles(ks, SEL_TILE), val_tiles(vsT, SEL_TILE),
      key_tiles(kw, WIN_TILE), val_tiles(vwT, WIN_TILE))


def rms_norm(x, g):
    y = x * lax.rsqrt(jnp.mean(x * x, -1, keepdims=True) + NORM_EPS)
    return y * g


TOKEN_TILE = 256
SUBLANES = 8


def _bdot(a, b):
    return jnp.dot(a.astype(BF16), b.astype(BF16), preferred_element_type=F32)


def _split_dot(x, m, dims):
    hi = x.astype(BF16)
    lo = (x - hi.astype(F32)).astype(BF16)
    return (lax.dot_general(hi, m, (dims, ((), ())), preferred_element_type=F32)
            + lax.dot_general(lo, m, (dims, ((), ())), preferred_element_type=F32))


def _head_sum(x, seg):
    sums = _split_dot(x, seg, ((1,), (0,)))
    return _split_dot(sums, seg, ((1,), (1,)))


def _rms(x, g):
    return x * lax.rsqrt(jnp.mean(x * x, axis=-1, keepdims=True) + NORM_EPS) * g


def _normed_and_prev(h_ref, hprev_ref, shift_ref, g_ref):
    g = g_ref[...]
    xn = _rms(h_ref[0], g)
    prev_last = _rms(hprev_ref[0], g)[SUBLANES - 1:SUBLANES, :]
    first = jnp.where(pl.program_id(1) == 0, shift_ref[0], prev_last)
    row = lax.broadcasted_iota(jnp.int32, (xn.shape[0], 1), 0)
    return xn, jnp.where(row == 0, first, pltpu.roll(xn, 1, 0))


def _softplus(u):
    return jnp.maximum(u, 0.0) + jnp.log(1.0 + jnp.exp(-jnp.abs(u)))


def _sigmoid(u):
    return 1.0 / (1.0 + jnp.exp(-u))


def _rwkv_r_kernel(h_ref, hprev_ref, shift_ref, g_ref, mu_ref, w_ref, w0_ref, w1_ref, w2_ref, r_ref, ld_ref):
    xn, xp = _normed_and_prev(h_ref, hprev_ref, shift_ref, g_ref)
    dx = xp - xn
    r_ref[0] = _bdot(xn + dx * mu_ref[0:1, :], w_ref[...])
    lora = _bdot(jnp.tanh(_bdot(xn + dx * mu_ref[1:2, :], w1_ref[...])), w2_ref[...])
    w_log = -_softplus(-(w0_ref[...] + lora)) - 0.5
    ld_ref[0] = -jnp.exp(w_log)


def _rwkv_k_kernel(h_ref, hprev_ref, shift_ref, g_ref, mu_ref, w_ref, a0_ref, a1_ref, a2_ref, kk_w_ref, ka_ref,
                   seg_ref, k_ref, kk_ref, a_ref):
    xn, xp = _normed_and_prev(h_ref, hprev_ref, shift_ref, g_ref)
    dx = xp - xn
    k = _bdot(xn + dx * mu_ref[0:1, :], w_ref[...])
    a = _sigmoid(a0_ref[...] + _bdot(_bdot(xn + dx * mu_ref[1:2, :], a1_ref[...]), a2_ref[...]))
    kk = k * kk_w_ref[...]
    kk_ref[0] = kk * lax.rsqrt(jnp.maximum(_head_sum(kk * kk, seg_ref[...]), 1e-24))
    k_ref[0] = k * (1.0 + (a - 1.0) * ka_ref[...])
    a_ref[0] = a


def _rwkv_v_kernel(h_ref, hprev_ref, shift_ref, g_ref, mu_ref, w_ref, *rest, residual):
    xn, xp = _normed_and_prev(h_ref, hprev_ref, shift_ref, g_ref)
    mix = xn + (xp - xn) * mu_ref[0:1, :]
    v = _bdot(mix, w_ref[...])
    if residual:
        vfirst_ref, v0_ref, v1_ref, v2_ref, v_ref = rest
        v = v + (vfirst_ref[0] - v) * _sigmoid(v0_ref[...] + _bdot(_bdot(mix, v1_ref[...]), v2_ref[...]))
    else:
        (v_ref,) = rest
    v_ref[0] = v


def _rwkv_z_kernel(h_ref, hprev_ref, shift_ref, g_ref, mu_ref, w_ref, z_ref):
    xn, xp = _normed_and_prev(h_ref, hprev_ref, shift_ref, g_ref)
    z_ref[0] = _bdot(xn + (xp - xn) * mu_ref[0:1, :], w_ref[...])


def _rwkv_post_kernel(y_ref, r_ref, k_ref, v_ref, z_ref, h_ref, lnw_ref, lnb_ref, rk_ref, seg_ref, wout_ref, o_ref):
    seg = seg_ref[...]
    y = y_ref[0]
    d = y - _head_sum(y, seg) * (1.0 / RW_HEAD)
    var = _head_sum(d * d, seg) * (1.0 / RW_HEAD)
    yn = d * lax.rsqrt(var + LNX_EPS) * lnw_ref[...] + lnb_ref[...]
    yn = yn + _head_sum(r_ref[0] * k_ref[0] * rk_ref[...], seg) * v_ref[0]
    z = z_ref[0]
    o_ref[0] = h_ref[0] + _bdot(yn * (z * _sigmoid(z)), wout_ref[...])


def _head_indicator():
    return (jnp.arange(INNER)[:, None] // RW_HEAD == jnp.arange(LANES)[None, :]).astype(BF16)


def _token_call(kernel, operands, out_widths, name, out_dtype=F32):
    B, T = next(a.shape[:2] for kind, a in operands if kind == 'tile')
    tm = TOKEN_TILE if T % TOKEN_TILE == 0 else T
    per_tile = tm // SUBLANES
    specs = []
    for kind, a in operands:
        if kind == 'tile':
            specs.append(pl.BlockSpec((1, tm, a.shape[2]), lambda b, t: (b, t, 0)))
        elif kind == 'prev':
            specs.append(pl.BlockSpec((1, SUBLANES, a.shape[2]), lambda b, t: (b, jnp.maximum(t * per_tile - 1, 0), 0)))
        elif kind == 'batch':
            specs.append(pl.BlockSpec((1, 1, a.shape[2]), lambda b, t: (b, 0, 0)))
        else:
            specs.append(pl.BlockSpec(a.shape, lambda b, t, n=a.ndim: (0,) * n))
    return pl.pallas_call(
        kernel,
        grid=(B, T // tm),
        in_specs=specs,
        out_specs=[pl.BlockSpec((1, tm, w), lambda b, t: (b, t, 0)) for w in out_widths],
        out_shape=[jax.ShapeDtypeStruct((B, T, w), out_dtype) for w in out_widths],
        compiler_params=pltpu.CompilerParams(
            dimension_semantics=("arbitrary", "arbitrary"), vmem_limit_bytes=VMEM_LIMIT_BYTES),
        name=name,
    )(*[a for _, a in operands])


def rwkv_layer(h, shift_prev, s0, v_first, norm_g, mu, w_in, w_out, w0, w1, w2, a0, a1, a2, k_k, k_a, r_k,
               lnx_w, lnx_b, vres):
    B, T, D = h.shape
    row = lambda x: x.reshape(1, -1)
    bf = lambda x: x.astype(BF16)
    seg = _head_indicator()
    common = [('tile', h), ('prev', h), ('batch', shift_prev.reshape(B, 1, D)), ('const', row(norm_g))]
    r, logd = _token_call(
        _rwkv_r_kernel, common + [('const', mu[jnp.array([0, 4])]), ('const', bf(w_in[0])), ('const', row(w0)),
                                  ('const', bf(w1)), ('const', bf(w2))], [INNER, INNER], "rwkv_r")
    k, kk, a = _token_call(
        _rwkv_k_kernel, common + [('const', mu[jnp.array([1, 5])]), ('const', bf(w_in[1])), ('const', row(a0)),
                                  ('const', bf(a1)), ('const', bf(a2)), ('const', row(k_k)), ('const', row(k_a)),
                                  ('const', seg)], [INNER] * 3, "rwkv_k")
    if vres is None:
        (v,) = _token_call(functools.partial(_rwkv_v_kernel, residual=False),
                           common + [('const', mu[2:3]), ('const', bf(w_in[2]))], [INNER], "rwkv_v")
        v_first = v
    else:
        v0, v1, v2 = vres
        (v,) = _token_call(functools.partial(_rwkv_v_kernel, residual=True),
                           common + [('const', mu[2:3]), ('const', bf(w_in[2])), ('tile', v_first), ('const', row(v0)),
                                     ('const', bf(v1)), ('const', bf(v2))], [INNER], "rwkv_v")
    (z,) = _token_call(_rwkv_z_kernel, common + [('const', mu[3:4]), ('const', bf(w_in[3]))], [INNER], "rwkv_z")
    y, s_T = wkv_scan(r, logd, k, v, kk, a, s0, chunk=WKV_CHUNK if T % WKV_CHUNK == 0 else T,
                      t_block=WKV_T_BLOCK if T % WKV_T_BLOCK == 0 else T)
    (h_new,) = _token_call(
        _rwkv_post_kernel,
        [('tile', y), ('tile', r), ('tile', k), ('tile', v), ('tile', z), ('tile', h), ('const', row(lnx_w)),
         ('const', row(lnx_b)), ('const', r_k.reshape(1, INNER)), ('const', seg), ('const', bf(w_out))],
        [D], "rwkv_post")
    return h_new, v_first, s_T


ROT_HALF = ROT_DIM // 2


def _rope_rows(x, cos, sin_lo, sin_hi):
    out = []
    for c in range(x.shape[1] // LANES):
        xc = x[:, c * LANES:(c + 1) * LANES]
        out.append(xc * cos + pltpu.roll(xc, LANES - ROT_HALF, 1) * sin_lo + pltpu.roll(xc, ROT_HALF, 1) * sin_hi)
    return jnp.concatenate(out, axis=1)


def _rope_cols(x, cos, sin):
    n = x.shape[0] // HEAD_DIM
    x = x.reshape(n, HEAD_DIM, x.shape[1])
    x1, x2 = x[:, :ROT_HALF], x[:, ROT_HALF:ROT_DIM]
    y = jnp.concatenate([x1 * cos - x2 * sin, x1 * sin + x2 * cos, x[:, ROT_DIM:]], axis=1)
    return y.reshape(n * HEAD_DIM, y.shape[2])


def _nsa_rows_kernel(h_ref, g_ref, w_ref, cos_ref, slo_ref, shi_ref, cmp_ref, sel_ref, win_ref, z_ref, ks_ref, kw_ref):
    xn = _rms(h_ref[0], g_ref[...])
    p = _bdot(xn, w_ref[...])
    cos, slo, shi = cos_ref[0], slo_ref[0], shi_ref[0]
    cmp_ref[0] = p[:, :2 * KV_W]
    ks = _rope_rows(p[:, 2 * KV_W:3 * KV_W], cos, slo, shi)
    kw = _rope_rows(p[:, 4 * KV_W:5 * KV_W], cos, slo, shi)
    sel_ref[0] = jnp.concatenate([ks, p[:, 3 * KV_W:4 * KV_W]], axis=1)
    win_ref[0] = jnp.concatenate([kw, p[:, 5 * KV_W:6 * KV_W]], axis=1)
    z_ref[0] = p[:, 6 * KV_W:]
    ks_ref[0] = ks.astype(BF16)
    kw_ref[0] = kw.astype(BF16)


def _nsa_cols_kernel(h_ref, g_ref, wT_ref, cos_ref, sin_ref, qT_ref, qrT_ref, vsT_ref, vwT_ref, gT_ref):
    xn = _rms(h_ref[0], g_ref[...]).astype(BF16)
    pT = lax.dot_general(wT_ref[...], xn, (((1,), (1,)), ((), ())), preferred_element_type=F32)
    q = pT[:INNER]
    qT_ref[0] = q.astype(BF16)
    qrT_ref[0] = (_rope_cols(q, cos_ref[...], sin_ref[...]) * LOG2_E).astype(BF16)
    vsT_ref[0] = pT[INNER:INNER + KV_W].astype(BF16)
    vwT_ref[0] = pT[INNER + KV_W:INNER + 2 * KV_W].astype(BF16)
    gT_ref[0] = _sigmoid(pT[INNER + 2 * KV_W:])


def _rope_tables(pos):
    inv = ROPE_THETA ** (-jnp.arange(ROT_HALF, dtype=F32) / ROT_HALF)
    ang = pos.astype(F32)[:, None] * inv[None, :]
    cos, sin = jnp.cos(ang), jnp.sin(ang)
    lane = jnp.arange(LANES) % HEAD_DIM
    f = lane % ROT_HALF
    cos_l = jnp.where(lane[None, :] < ROT_DIM, cos[:, f], 1.0)
    slo_l = jnp.where(lane[None, :] < ROT_HALF, -sin[:, f], 0.0)
    shi_l = jnp.where((lane[None, :] >= ROT_HALF) & (lane[None, :] < ROT_DIM), sin[:, f], 0.0)
    return cos_l, slo_l, shi_l, cos.T, sin.T


def nsa_pre(h, norm_g, w_in, pos):
    B, T, D = h.shape
    tm = TOKEN_TILE if T % TOKEN_TILE == 0 else T
    cos_l, slo_l, shi_l, cosT, sinT = _rope_tables(pos)
    scale = HEAD_DIM ** -0.5
    g_row = norm_g.reshape(1, D)
    w_rows = jnp.concatenate([w_in[:, Q_END:KV_END], w_in[:, G_END:]], axis=1).astype(BF16)
    w_g = w_in[:, KV_END:G_END].reshape(D, N_KV, HPG, 3).transpose(0, 1, 3, 2).reshape(D, 3 * N_HEADS)
    w_cols = jnp.concatenate([w_in[:, :Q_END] * scale, w_in[:, Q_END + 3 * KV_W:Q_END + 4 * KV_W],
                              w_in[:, Q_END + 5 * KV_W:Q_END + 6 * KV_W], w_g], axis=1).T.astype(BF16)
    tile = lambda w: pl.BlockSpec((1, tm, w), lambda b, t: (b, t, 0))
    whole = lambda a: pl.BlockSpec(a.shape, lambda b, t, n=a.ndim: (0,) * n)
    tab = pl.BlockSpec((1, tm, LANES), lambda b, t: (0, t, 0))
    params = pltpu.CompilerParams(dimension_semantics=("arbitrary", "arbitrary"), vmem_limit_bytes=VMEM_LIMIT_BYTES)
    cmp_rows, sel_rows, win_rows, z, ks, kw = pl.pallas_call(
        _nsa_rows_kernel,
        grid=(B, T // tm),
        in_specs=[tile(D), whole(g_row), whole(w_rows), tab, tab, tab],
        out_specs=[tile(2 * KV_W), tile(2 * KV_W), tile(2 * KV_W), tile(INNER), tile(KV_W), tile(KV_W)],
        out_shape=[jax.ShapeDtypeStruct((B, T, 2 * KV_W), F32)] * 3 + [jax.ShapeDtypeStruct((B, T, INNER), F32)]
        + [jax.ShapeDtypeStruct((B, T, KV_W), BF16)] * 2,
        compiler_params=params, name="nsa_rows",
    )(h, g_row, w_rows, cos_l[None], slo_l[None], shi_l[None])
    colt = lambda r: pl.BlockSpec((1, r, tm), lambda b, t: (b, 0, t))
    tabT = pl.BlockSpec((ROT_HALF, tm), lambda b, t: (0, t))
    qT, qrT, vsT, vwT, gT = pl.pallas_call(
        _nsa_cols_kernel,
        grid=(B, T // tm),
        in_specs=[tile(D), whole(g_row), whole(w_cols), tabT, tabT],
        out_specs=[colt(INNER), colt(INNER), colt(KV_W), colt(KV_W), colt(3 * N_HEADS)],
        out_shape=[jax.ShapeDtypeStruct((B, INNER, T), BF16)] * 2 + [jax.ShapeDtypeStruct((B, KV_W, T), BF16)] * 2
        + [jax.ShapeDtypeStruct((B, 3 * N_HEADS, T), F32)],
        compiler_params=params, name="nsa_cols",
    )(h, g_row, w_cols, cosT, sinT)
    return cmp_rows, sel_rows, win_rows, z, ks, kw, qT, qrT, vsT, vwT, gT


PAGE = 128
SAMPLE_VMEM_LIMIT_BYTES = 56 * 1024 * 1024


def _gather_pages(pt_ref, b, pool_hbm, buf, sem, n_pages):
    rows = pool_hbm.shape[1]
    copies = [pltpu.make_async_copy(pool_hbm.at[pt_ref[b, p]], buf.at[pl.ds(p * rows, rows)], sem)
              for p in range(n_pages)]
    for cp in copies:
        cp.start()
    for cp in copies:
        cp.wait()


def _sample_compress_kernel(pt_ref, pool_hbm, w1_hbm, pe_ref, w2_ref, out_ref, buf, w1_vmem, sem, wsem, *, n_pages):
    b = pl.program_id(0)

    @pl.when(b == 0)
    def _():
        cp = pltpu.make_async_copy(w1_hbm, w1_vmem, wsem)
        cp.start()
        cp.wait()

    _gather_pages(pt_ref, b, pool_hbm, buf, sem, n_pages)
    n_chunks = n_pages * PAGE // CMP_STRIDE
    row = lax.broadcasted_iota(jnp.int32, (n_chunks, 1), 0)
    for kv in range(2):
        lanes = slice(kv * KV_W, (kv + 1) * KV_W)
        parts = []
        for m in range(CMP_RATIO):
            acc = jnp.zeros((n_chunks, w1_vmem.shape[3]), F32)
            for s in range(CMP_STRIDE):
                at = s * 2 * KV_W + kv * KV_W
                x = buf[:, at:at + KV_W] + pe_ref[m * CMP_STRIDE + s:m * CMP_STRIDE + s + 1, :]
                acc = acc + jnp.dot(x.astype(BF16), w1_vmem[kv, m * CMP_STRIDE + s], preferred_element_type=F32)
            parts.append(acc)
        pre = parts[0] + pltpu.roll(parts[1], n_chunks - 1, 0)
        hid = pre * _sigmoid(pre)
        ck = jnp.dot(hid.astype(BF16), w2_ref[kv], preferred_element_type=F32)
        out_ref[0, :, lanes] = jnp.where(row < n_chunks - 1, ck, 0.0)


def _block_diag(w):
    eye = jnp.eye(N_KV, dtype=w.dtype)
    out = eye[:, None, :, None] * w[..., None, :, None, :]
    return out.reshape(w.shape[:-2] + (N_KV * w.shape[-2], N_KV * w.shape[-1]))


def sample_compress(pool, page_table, pe, w1, w2):
    B, n_pages = page_table.shape
    n_chunks = n_pages * PAGE // CMP_STRIDE
    pool2 = pool.reshape(pool.shape[0], PAGE // CMP_STRIDE, CMP_STRIDE * 2 * KV_W)
    w1_bd = _block_diag(w1).astype(BF16)
    w2_bd = _block_diag(w2).astype(BF16)
    pe_t = jnp.tile(pe, (1, N_KV))
    return pl.pallas_call(
        functools.partial(_sample_compress_kernel, n_pages=n_pages),
        grid_spec=pltpu.PrefetchScalarGridSpec(
            num_scalar_prefetch=1,
            grid=(B,),
            in_specs=[pl.BlockSpec(memory_space=pl.ANY), pl.BlockSpec(memory_space=pl.ANY),
                      pl.BlockSpec(pe_t.shape, lambda b, pt: (0, 0)),
                      pl.BlockSpec(w2_bd.shape, lambda b, pt: (0, 0, 0))],
            out_specs=pl.BlockSpec((1, n_chunks, 2 * KV_W), lambda b, pt: (b, 0, 0)),
            scratch_shapes=[pltpu.VMEM((n_chunks, CMP_STRIDE * 2 * KV_W), F32),
                            pltpu.VMEM(w1_bd.shape, BF16),
                            pltpu.SemaphoreType.DMA(()), pltpu.SemaphoreType.DMA(())]),
        out_shape=jax.ShapeDtypeStruct((B, n_chunks, 2 * KV_W), F32),
        compiler_params=pltpu.CompilerParams(
            dimension_semantics=("arbitrary",), vmem_limit_bytes=SAMPLE_VMEM_LIMIT_BYTES),
        name="sample_compress",
    )(page_table, pool2, w1_bd, pe_t, w2_bd)


SAMPLE_TILE = 512


def _online_step(s, v, m_ref, l_ref, acc_ref):
    m_old = m_ref[...]
    m_new = jnp.maximum(m_old, jnp.max(s, axis=0, keepdims=True))
    alpha = jnp.exp2(m_old - m_new)
    p = jnp.exp2(s - m_new)
    l_ref[...] = alpha * l_ref[...] + jnp.sum(p, axis=0, keepdims=True)
    acc_ref[...] = alpha * acc_ref[...] + lax.dot_general(
        v.astype(BF16), p.astype(BF16), (((0,), (0,)), ((), ())), preferred_element_type=F32)
    m_ref[...] = m_new


def _split_dot_left(m, x):
    hi = x.astype(BF16)
    lo = (x - hi.astype(F32)).astype(BF16)
    return jnp.dot(m, hi, preferred_element_type=F32) + jnp.dot(m, lo, preferred_element_type=F32)


def _sample_attend_kernel(pt_ref, pool_hbm, ckv_ref, qraw_ref, qrot_ref, g_ref, selnew_ref, win_ref, winnew_ref,
                          ovT_ref, fold_ref, o_ref, buf, sel_scr, m_scr, l_scr, acc_scr, sem,
                          *, n_pages, n_cmp, n_sel, past, t_new):
    b = pl.program_id(0)
    _gather_pages(pt_ref, b, pool_hbm, buf, sem, n_pages)
    width = qraw_ref.shape[2]
    q_idx = lax.broadcasted_iota(jnp.int32, (1, width), 1) % t_new
    qpos = past + q_idx

    n_rows = ckv_ref.shape[1]
    n_iota = lax.broadcasted_iota(jnp.int32, (n_rows, width), 0)
    c_valid = (n_iota < n_cmp) & ((n_iota * CMP_STRIDE + (CMP_BLOCK - 1)) <= qpos)
    s = jnp.dot(ckv_ref[0, :, :KV_W].astype(BF16), qraw_ref[0], preferred_element_type=F32)
    s = jnp.where(c_valid, s, NEG)
    mx = jnp.max(s, axis=0, keepdims=True)
    e = jnp.where(c_valid, jnp.exp(s - mx), 0.0)
    l = jnp.sum(e, axis=0, keepdims=True)
    p = e * (1.0 / jnp.where(l > 0.0, l, 1.0))
    o_ref[0] = g_ref[0, 0:1, :] * lax.dot_general(
        ckv_ref[0, :, KV_W:].astype(BF16), p.astype(BF16), (((0,), (0,)), ((), ())), preferred_element_type=F32)

    fold = fold_ref[...]
    p_sum = _split_dot(p, fold, ((1,), (0,)))
    imp = _split_dot_left(ovT_ref[...], p_sum)
    j_iota = lax.broadcasted_iota(jnp.int32, imp.shape, 0)
    cur = (past + lax.broadcasted_iota(jnp.int32, (1, imp.shape[1]), 1) % t_new) // SEL_BLOCK
    forced = (j_iota == 0) | (j_iota == cur) | (j_iota == cur - 1)
    score = jnp.where(j_iota <= cur, imp + jnp.where(forced, FORCE_BONUS, 0.0), NEG)
    sel = jnp.zeros(imp.shape, F32)
    for _ in range(TOP_N):
        mx = jnp.max(score, axis=0, keepdims=True)
        first = jnp.min(jnp.where(score == mx, j_iota, n_sel), axis=0, keepdims=True)
        pick = j_iota == first
        sel = jnp.where(pick & (mx > NEG / 2), 1.0, sel)
        score = jnp.where(pick, PICKED, score)
    sel_scr[...] = lax.dot_general(sel.astype(BF16), fold, (((1,), (1,)), ((), ())), preferred_element_type=F32)

    def reset():
        m_scr[...] = jnp.full(m_scr.shape, NEG, F32)
        l_scr[...] = jnp.zeros(l_scr.shape, F32)
        acc_scr[...] = jnp.zeros(acc_scr.shape, F32)

    qrot = qrot_ref[0]
    new_rows = lax.broadcasted_iota(jnp.int32, (t_new, width), 0)
    causal_new = new_rows <= q_idx

    reset()
    blocks_per_tile = SAMPLE_TILE // SEL_BLOCK
    for kt in range(n_pages * PAGE // SAMPLE_TILE):
        chosen = sel_scr[kt * blocks_per_tile:(kt + 1) * blocks_per_tile, :]
        chosen = jnp.concatenate(
            [jnp.broadcast_to(chosen[i:i + 1, :], (SEL_BLOCK, width)) for i in range(blocks_per_tile)], axis=0)
        rows = buf[kt * SAMPLE_TILE:(kt + 1) * SAMPLE_TILE, :]
        s = jnp.dot(rows[:, :KV_W].astype(BF16), qrot, preferred_element_type=F32)
        _online_step(s + jnp.where(chosen > 0.5, 0.0, NEG), rows[:, KV_W:], m_scr, l_scr, acc_scr)
    last = past // SEL_BLOCK
    allowed = causal_new & (sel_scr[last:last + 1, :] > 0.5)
    s = jnp.dot(selnew_ref[0, :, :KV_W].astype(BF16), qrot, preferred_element_type=F32)
    _online_step(s + jnp.where(allowed, 0.0, NEG), selnew_ref[0, :, KV_W:], m_scr, l_scr, acc_scr)
    o_ref[0] = o_ref[0] + g_ref[0, 1:2, :] * acc_scr[...] * (1.0 / l_scr[...])

    reset()
    n_buf = win_ref.shape[1]
    w_pos = past - n_buf + lax.broadcasted_iota(jnp.int32, (n_buf, width), 0)
    allowed = (w_pos >= qpos - WINDOW) & (w_pos >= 0)
    s = jnp.dot(win_ref[0, :, :KV_W].astype(BF16), qrot, preferred_element_type=F32)
    _online_step(s + jnp.where(allowed, 0.0, NEG), win_ref[0, :, KV_W:], m_scr, l_scr, acc_scr)
    s = jnp.dot(winnew_ref[0, :, :KV_W].astype(BF16), qrot, preferred_element_type=F32)
    _online_step(s + jnp.where(causal_new, 0.0, NEG), winnew_ref[0, :, KV_W:], m_scr, l_scr, acc_scr)
    o_ref[0] = o_ref[0] + g_ref[0, 2:3, :] * acc_scr[...] * (1.0 / l_scr[...])


def sample_attend(pool, page_table, ckv, qT, qrT, gT, sel_new, win_buf, win_new):
    B, n_pages = page_table.shape
    t_new = sel_new.shape[1]
    past = n_pages * PAGE
    width = N_HEADS * t_new
    assert t_new < CMP_STRIDE and t_new <= SEL_BLOCK and past % SAMPLE_TILE == 0 and win_buf.shape[1] <= past
    assert t_new <= win_buf.shape[1] == min(WINDOW, past)
    n_cmp = past // CMP_STRIDE - CMP_RATIO + 1
    n_sel = -(-(past // SEL_BLOCK + 1) // SUBLANES) * SUBLANES

    def block_q(x):
        x = x.reshape(N_KV, HPG, HEAD_DIM, B, t_new).transpose(3, 0, 2, 1, 4)
        eye = jnp.eye(N_KV, dtype=x.dtype)
        x = x[:, :, :, None, :, :] * eye[None, :, None, :, None, None]
        return x.reshape(B, KV_W, width)

    gates = gT.reshape(N_KV, 3, HPG, B, t_new).transpose(3, 1, 0, 2, 4).reshape(B, 3, width)
    cs = jnp.arange(ckv.shape[1]) * CMP_STRIDE
    ss = jnp.arange(n_sel) * SEL_BLOCK
    ovT = (jnp.clip(jnp.minimum(cs[None, :] + CMP_BLOCK, ss[:, None] + SEL_BLOCK)
                    - jnp.maximum(cs[None, :], ss[:, None]), 0, None).astype(F32) / CMP_BLOCK)
    ovT = jnp.where(jnp.arange(ckv.shape[1])[None, :] < n_cmp, ovT, 0.0).astype(BF16)
    lane = jnp.arange(width)
    col = (lane // (HPG * t_new)) * t_new + lane % t_new
    fold = (col[:, None] == jnp.arange(LANES)[None, :]).astype(BF16)
    per_b = lambda a: pl.BlockSpec((1,) + a.shape[1:], lambda b, pt, n=a.ndim: (b,) + (0,) * (n - 1))
    whole = lambda a: pl.BlockSpec(a.shape, lambda b, pt, n=a.ndim: (0,) * n)
    pool2 = pool.reshape(pool.shape[0], PAGE, 2 * KV_W)
    qraw, qrot = block_q(qT), block_q(qrT)
    o_bd = pl.pallas_call(
        functools.partial(_sample_attend_kernel, n_pages=n_pages, n_cmp=n_cmp, n_sel=n_sel, past=past, t_new=t_new),
        grid_spec=pltpu.PrefetchScalarGridSpec(
            num_scalar_prefetch=1,
            grid=(B,),
            in_specs=[pl.BlockSpec(memory_space=pl.ANY), per_b(ckv), per_b(qraw), per_b(qrot), per_b(gates),
                      per_b(sel_new), per_b(win_buf), per_b(win_new), whole(ovT), whole(fold)],
            out_specs=pl.BlockSpec((1, KV_W, width), lambda b, pt: (b, 0, 0)),
            scratch_shapes=[pltpu.VMEM((past, 2 * KV_W), F32),
                            pltpu.VMEM((n_sel, width), F32),
                            pltpu.VMEM((1, width), F32), pltpu.VMEM((1, width), F32),
                            pltpu.VMEM((KV_W, width), F32),
                            pltpu.SemaphoreType.DMA(())]),
        out_shape=jax.ShapeDtypeStruct((B, KV_W, width), F32),
        compiler_params=pltpu.CompilerParams(dimension_semantics=("arbitrary",), vmem_limit_bytes=VMEM_LIMIT_BYTES),
        name="sample_attend",
    )(page_table, pool2, ckv, qraw, qrot, gates, sel_new, win_buf, win_new, ovT, fold)
    o = o_bd.reshape(B, N_KV, HEAD_DIM, N_KV, HPG, t_new)
    o = jnp.stack([o[:, g, :, g] for g in range(N_KV)], axis=1)
    return o.transpose(0, 4, 1, 3, 2).reshape(B, t_new, INNER)


def compress(k, v, pe, w1, w2):
    def phi(t, a, b):
        B, T = t.shape[:2]
        n_chunk = T // CMP_STRIDE
        nc = n_chunk - CMP_RATIO + 1
        chunks = t[:, :n_chunk * CMP_STRIDE].reshape(B, n_chunk, CMP_STRIDE, N_KV, HEAD_DIM)
        parts = []
        for m in range(CMP_RATIO):
            sl = slice(m * CMP_STRIDE, (m + 1) * CMP_STRIDE)
            part = jnp.einsum('bcsgd,sdh->bcgh', chunks + pe[None, None, sl, None, :], a[sl])
            parts.append(part[:, m:m + nc])
        h = jax.nn.silu(sum(parts))
        return jnp.einsum('bngh,hd->bngd', h, b)
    ck = phi(k, w1[0], w2[0])
    cv = phi(v, w1[1], w2[1])
    c_end = jnp.arange(ck.shape[1]) * CMP_STRIDE + CMP_BLOCK - 1
    return ck, cv, c_end


def _residual_matmul_kernel(x_ref, h_ref, w_ref, o_ref):
    o_ref[0] = h_ref[0] + jnp.dot(x_ref[0], w_ref[...], preferred_element_type=F32)


def nsa_prompt(h, norm_g, w_in, w_out, pe, cw1, cw2):
    B, T, _ = h.shape
    cmp_rows, sel_rows, win_rows, z, ks, kw, qT, qrT, vsT, vwT, gT = nsa_pre(h, norm_g, w_in, jnp.arange(T))
    heads = lambda t: t.reshape(B, T, N_KV, HEAD_DIM)
    ck, cv, _ = compress(heads(cmp_rows[..., :KV_W]), heads(cmp_rows[..., KV_W:]), pe, cw1, cw2)
    gated = nsa_prompt_attend(qT, qrT, gT, z, ck, cv, ks, vsT, kw, vwT)
    (h_new,) = _token_call(_residual_matmul_kernel, [('tile', gated), ('tile', h), ('const', w_out.astype(BF16))],
                           [D_MODEL], "nsa_out")
    n_keep = min(WINDOW, T)
    rows = lambda t: t.reshape(B, t.shape[1], 2, N_KV, HEAD_DIM)
    return h_new, rows(cmp_rows), rows(sel_rows), rows(win_rows[:, -n_keep:])


def nsa_sample(h, norm_g, pool_cmp, pool_sel, win_buf, page_table, w_in, w_out, pe, cw1, cw2):
    B, T, D = h.shape
    past = page_table.shape[1] * pool_cmp.shape[1]
    pos = past + jnp.arange(B * T) % T
    cmp_rows, sel_rows, win_rows, z, _, _, qT, qrT, _, _, gT = nsa_pre(h.reshape(1, B * T, D), norm_g, w_in, pos)
    rows = lambda t: t.reshape(B, T, 2 * KV_W)
    ckv = sample_compress(pool_cmp, page_table, pe, cw1, cw2)
    n_buf = win_buf.shape[1]
    win_flat = win_buf.reshape(B, n_buf, 2 * KV_W)
    o = sample_attend(pool_sel, page_table, ckv, qT[0], qrT[0], gT[0], rows(sel_rows), win_flat, rows(win_rows))
    gated = (o * jax.nn.silu(z.reshape(B, T, INNER))).astype(BF16)
    (h_new,) = _token_call(_residual_matmul_kernel, [('tile', gated), ('tile', h), ('const', w_out.astype(BF16))],
                           [D_MODEL], "nsa_out")
    n_keep = min(WINDOW, n_buf + T)
    win_out = jnp.concatenate([win_flat, rows(win_rows)], axis=1)[:, -n_keep:]
    split = lambda t: t.reshape(B, t.shape[1], 2, N_KV, HEAD_DIM)
    return h_new, split(rows(cmp_rows)), split(rows(sel_rows)), split(win_out)


def kernel(x_prompt, x_sample, state_rwkv_wkv, state_rwkv_shift, cache_nsa_cmp, cache_nsa_sel, cache_nsa_win, page_table, norm_g, final_norm_g, rwkv_mu, rwkv_w_in, rwkv_w_out, rwkv_w0, rwkv_w1, rwkv_w2, rwkv_a0, rwkv_a1, rwkv_a2, rwkv_v0, rwkv_v1, rwkv_v2, rwkv_k_k, rwkv_k_a, rwkv_r_k, rwkv_lnx_w, rwkv_lnx_b, nsa_w_in, nsa_w_out, nsa_cmp_pe, nsa_cmp_w1, nsa_cmp_w2):

    def trunk(x, wkv0, shift0, sample):
        h = x
        v_first = None
        wkv, shift, cmp_rows, sel_rows, win_rows = [], [], [], [], []
        for layer in range(DEPTH):
            j = layer // N_MIXERS
            if layer % N_MIXERS == 0:
                vres = None if j == 0 else (rwkv_v0[j - 1], rwkv_v1[j - 1], rwkv_v2[j - 1])
                shift.append(rms_norm(h[:, -1], norm_g[layer]))
                h, v_first, s_T = rwkv_layer(
                    h, shift0[j], wkv0[j], v_first, norm_g[layer], rwkv_mu[j], rwkv_w_in[j], rwkv_w_out[j],
                    rwkv_w0[j], rwkv_w1[j], rwkv_w2[j], rwkv_a0[j], rwkv_a1[j], rwkv_a2[j],
                    rwkv_k_k[j], rwkv_k_a[j], rwkv_r_k[j], rwkv_lnx_w[j], rwkv_lnx_b[j], vres)
                wkv.append(s_T)
            else:
                if sample:
                    h, c, s, w = nsa_sample(h, norm_g[layer], cache_nsa_cmp[j], cache_nsa_sel[j], cache_nsa_win[j],
                                            page_table, nsa_w_in[j], nsa_w_out[j], nsa_cmp_pe[j],
                                            nsa_cmp_w1[j], nsa_cmp_w2[j])
                else:
                    h, c, s, w = nsa_prompt(h, norm_g[layer], nsa_w_in[j], nsa_w_out[j], nsa_cmp_pe[j],
                                            nsa_cmp_w1[j], nsa_cmp_w2[j])
                cmp_rows.append(c)
                sel_rows.append(s)
                win_rows.append(w)
        return (rms_norm(h, final_norm_g), jnp.stack(wkv), jnp.stack(shift),
                jnp.stack(cmp_rows), jnp.stack(sel_rows), jnp.stack(win_rows))

    zeros_wkv = jnp.zeros((rwkv_mu.shape[0], x_prompt.shape[0], RW_HEADS, RW_HEAD, RW_HEAD), x_prompt.dtype)
    zeros_shift = jnp.zeros((rwkv_mu.shape[0], x_prompt.shape[0], D_MODEL), x_prompt.dtype)
    y_prompt, wkv_p, shift_p, cmp_p, sel_p, win_p = trunk(x_prompt, zeros_wkv, zeros_shift, False)
    y_sample, wkv_s, shift_s, cmp_s, sel_s, win_s = trunk(x_sample, state_rwkv_wkv, state_rwkv_shift, True)
    return (y_prompt, y_sample, wkv_p, wkv_s, shift_p, shift_s, cmp_p, cmp_s, sel_p, sel_s, win_p, win_s)
```

```python
import functools

import jax
import jax.numpy as jnp
from jax import lax
from jax.experimental import pallas as pl
from jax.experimental.pallas import tpu as pltpu

D_MODEL = 1024
DEPTH = 4
N_MIXERS = 2
INNER = 2 * D_MODEL
NORM_EPS = 1e-6
RW_HEAD = 64
RW_HEADS = INNER // RW_HEAD
LNX_EPS = 64e-5
HEAD_DIM = 64
N_HEADS = INNER // HEAD_DIM
N_KV = 4
HPG = N_HEADS // N_KV
KV_W = N_KV * HEAD_DIM
ROT_DIM = HEAD_DIM // 4
ROPE_THETA = 500000.0
CMP_BLOCK = 32
CMP_STRIDE = 16
CMP_RATIO = CMP_BLOCK // CMP_STRIDE
SEL_BLOCK = 64
TOP_N = 16
WINDOW = 512
Q_BLOCK = 128
NEG = -1e30
FORCE_BONUS = 1e4
Q_END = INNER
KV_END = Q_END + 6 * KV_W
G_END = KV_END + 3 * N_HEADS
NSA_IN = G_END + INNER

F32 = jnp.float32
BF16 = jnp.bfloat16
VMEM_LIMIT_BYTES = 48 * 1024 * 1024


LANES = 128
WKV_CHUNK = 64
WKV_T_BLOCK = 512
WKV_PAIRS = 8


def _dot(a, b):
    return jnp.dot(a.astype(BF16), b.astype(BF16), preferred_element_type=F32)


def _dot_nt(a, b):
    return lax.dot_general(a.astype(BF16), b.astype(BF16), (((1,), (1,)), ((), ())), preferred_element_type=F32)


def _dot_tn(a, b):
    return lax.dot_general(a.astype(BF16), b.astype(BF16), (((0,), (0,)), ((), ())), preferred_element_type=F32)


def _wkv_kernel(r_ref, ld_ref, k_ref, v_ref, kk_ref, a_ref, s0_ref, y_ref, st_ref, s_scr, *, chunk, n_chunks, n_pairs):
    C = chunk
    R = 2 * C
    tb = pl.program_id(2)

    @pl.when(tb == 0)
    def _():
        z = jnp.zeros((RW_HEAD, RW_HEAD), F32)
        for g in range(n_pairs):
            s_scr[g] = jnp.concatenate(
                [jnp.concatenate([s0_ref[0, 2 * g], z], axis=1),
                 jnp.concatenate([z, s0_ref[0, 2 * g + 1]], axis=1)], axis=0)

    lane = lax.broadcasted_iota(jnp.int32, (1, LANES), 1)
    head_a = lane < RW_HEAD
    row = lax.broadcasted_iota(jnp.int32, (R, R), 0)
    col = lax.broadcasted_iota(jnp.int32, (R, R), 1)
    same = (row // C) == (col // C)
    strict = same & ((col % C) < (row % C))
    incl = same & ((col % C) <= (row % C))
    eye = (row == col).astype(F32)
    tr = lax.broadcasted_iota(jnp.int32, (C, C), 0)
    tc = lax.broadcasted_iota(jnp.int32, (C, C), 1)
    tri = (tc <= tr).astype(BF16)

    def stack(z):
        return jnp.concatenate([jnp.where(head_a, z, 0.0), jnp.where(head_a, 0.0, z)], axis=0)

    def body(ci, carry):
        sl = pl.ds(pl.multiple_of(ci * C, C), C)
        G = range(n_pairs)
        ld = [ld_ref[0, sl, g * LANES:(g + 1) * LANES] for g in G]
        ld_hi = [x.astype(BF16) for x in ld]
        ld_lo = [(x - h.astype(F32)).astype(BF16) for x, h in zip(ld, ld_hi)]
        cum = [jnp.dot(tri, h, preferred_element_type=F32) + jnp.dot(tri, l, preferred_element_type=F32)
               for h, l in zip(ld_hi, ld_lo)]
        e_neg = [jnp.exp(-c) for c in cum]
        kk = [kk_ref[0, sl, g * LANES:(g + 1) * LANES] for g in G]
        a_s = [stack(-kk[g] * jnp.exp(cum[g] - ld[g])) for g in G]
        r_s = [stack(r_ref[0, sl, g * LANES:(g + 1) * LANES] * jnp.exp(cum[g])) for g in G]
        b_s = [stack(kk[g] * a_ref[0, sl, g * LANES:(g + 1) * LANES] * e_neg[g]) for g in G]
        k_s = [stack(k_ref[0, sl, g * LANES:(g + 1) * LANES] * e_neg[g]) for g in G]
        v_s = [stack(v_ref[0, sl, g * LANES:(g + 1) * LANES]) for g in G]
        s2 = [s_scr[g] for g in G]
        ab = [jnp.where(strict, _dot_nt(a_s[g], b_s[g]), 0.0) for g in G]
        ak = [jnp.where(strict, _dot_nt(a_s[g], k_s[g]), 0.0) for g in G]
        rb = [jnp.where(incl, _dot_nt(r_s[g], b_s[g]), 0.0) for g in G]
        rk = [jnp.where(incl, _dot_nt(r_s[g], k_s[g]), 0.0) for g in G]
        rhs = [_dot_nt(a_s[g], s2[g]) + _dot(ak[g], v_s[g]) for g in G]
        y0 = [_dot_nt(r_s[g], s2[g]) + _dot(rk[g], v_s[g]) for g in G]
        tm = [eye + ab[g] for g in G]
        p = ab
        n = 2
        while n < C:
            p = [_dot(p[g], p[g]) for g in G]
            tm = [tm[g] + _dot(tm[g], p[g]) for g in G]
            n *= 2
        u_s = [_dot(tm[g], rhs[g]) for g in G]
        y_s = [y0[g] + _dot(rb[g], u_s[g]) for g in G]
        for g in G:
            y_ref[0, sl, g * LANES:(g + 1) * LANES] = y_s[g][:C] + y_s[g][C:]
            gamma = jnp.exp(cum[g][C - 1:C, :])
            s_scr[g] = (s2[g] + _dot_tn(u_s[g], b_s[g]) + _dot_tn(v_s[g], k_s[g])) * gamma
        return carry

    lax.fori_loop(0, n_chunks, body, 0)

    @pl.when(tb == pl.num_programs(2) - 1)
    def _():
        for g in range(n_pairs):
            st_ref[0, 2 * g] = s_scr[g, :RW_HEAD, :RW_HEAD]
            st_ref[0, 2 * g + 1] = s_scr[g, RW_HEAD:, RW_HEAD:]


def wkv_scan(r, logd, k, v, kk, a, s0, *, chunk, t_block):
    B, T, inner = r.shape
    width = WKV_PAIRS * LANES
    assert T % t_block == 0 and t_block % chunk == 0 and inner % width == 0
    seq_spec = pl.BlockSpec((1, t_block, width), lambda b, p, t: (b, t, p))
    st_spec = pl.BlockSpec((1, 2 * WKV_PAIRS, RW_HEAD, RW_HEAD), lambda b, p, t: (b, p, 0, 0))
    return pl.pallas_call(
        functools.partial(_wkv_kernel, chunk=chunk, n_chunks=t_block // chunk, n_pairs=WKV_PAIRS),
        grid=(B, inner // width, T // t_block),
        in_specs=[seq_spec] * 6 + [st_spec],
        out_specs=[seq_spec, st_spec],
        out_shape=[jax.ShapeDtypeStruct((B, T, inner), F32), jax.ShapeDtypeStruct(s0.shape, F32)],
        scratch_shapes=[pltpu.VMEM((WKV_PAIRS, LANES, LANES), F32)],
        compiler_params=pltpu.CompilerParams(
            dimension_semantics=("arbitrary", "arbitrary", "arbitrary"), vmem_limit_bytes=VMEM_LIMIT_BYTES),
        name="wkv_scan",
    )(r, logd, k, v, kk, a, s0)


KEY_TILE = 512
assert WINDOW <= KEY_TILE and KEY_TILE % Q_BLOCK == 0
BLOCKS_PER_TILE = KEY_TILE // SEL_BLOCK
MASK_COLS = 16
ONES_ROWS = 16
LOG2_E = 1.4426950408889634
PICKED = -3e38


def _nsa_prompt_kernel(qT_ref, qrT_ref, gT_ref, z_ref, ck_ref, cvT_ref, ovT_ref, ks_ref, vsT_ref, kw_ref, vwT_ref,
                       o_ref, sel_scr, m_scr, l_scr, acc_scr, o_scr, s_scr, cm_scr, *, n_cmp, n_sel):
    i = pl.program_id(2)
    q0 = i * Q_BLOCK
    qpos = q0 + lax.broadcasted_iota(jnp.int32, (1, Q_BLOCK), 1)

    def heads_on_lanes(ref):
        return jnp.concatenate([ref[0, h * HEAD_DIM:(h + 1) * HEAD_DIM, :] for h in range(HPG)], axis=1)

    def gate_row(br):
        return jnp.concatenate([gT_ref[0, br * HPG + h:br * HPG + h + 1, :] for h in range(HPG)], axis=1)

    def per_head(x):
        return jnp.concatenate([x] * HPG, axis=1)

    n_iota = lax.broadcasted_iota(jnp.int32, (n_cmp, Q_BLOCK), 0)
    c_valid = per_head((n_iota * CMP_STRIDE + (CMP_BLOCK - 1)) <= qpos)
    s = jnp.dot(ck_ref[0, 0], heads_on_lanes(qT_ref), preferred_element_type=F32)
    s = jnp.where(c_valid, s, NEG)
    mx = jnp.max(s, axis=0, keepdims=True)
    e = jnp.where(c_valid, jnp.exp(s - mx), 0.0)
    l = jnp.sum(e, axis=0, keepdims=True)
    p = e * (1.0 / jnp.where(l > 0.0, l, 1.0))
    o_scr[...] = gate_row(0) * jnp.dot(cvT_ref[0, 0], p.astype(BF16), preferred_element_type=F32)
    p_sum = p[:, :Q_BLOCK]
    for h in range(1, HPG):
        p_sum = p_sum + p[:, h * Q_BLOCK:(h + 1) * Q_BLOCK]

    ps_hi = p_sum.astype(BF16)
    ps_lo = (p_sum - ps_hi.astype(F32)).astype(BF16)
    ovT = ovT_ref[...]
    imp = jnp.dot(ovT, ps_hi, preferred_element_type=F32) + jnp.dot(ovT, ps_lo, preferred_element_type=F32)
    j_iota = lax.broadcasted_iota(jnp.int32, (n_sel, Q_BLOCK), 0)
    cur = qpos // SEL_BLOCK
    forced = (j_iota == 0) | (j_iota == cur) | (j_iota == cur - 1)
    score = jnp.where(j_iota <= cur, imp + jnp.where(forced, FORCE_BONUS, 0.0), NEG)
    sel = jnp.zeros((n_sel, Q_BLOCK), F32)
    for _ in range(TOP_N):
        mx = jnp.max(score, axis=0, keepdims=True)
        first = jnp.min(jnp.where(score == mx, j_iota, n_sel), axis=0, keepdims=True)
        pick = j_iota == first
        sel = jnp.where(pick & (mx > NEG / 2), 1.0, sel)
        score = jnp.where(pick, PICKED, score)
    sel_scr[...] = sel

    def reset():
        m_scr[...] = jnp.full(m_scr.shape, NEG, F32)
        l_scr[...] = jnp.zeros(l_scr.shape, F32)
        acc_scr[...] = jnp.zeros(acc_scr.shape, F32)

    qrT = heads_on_lanes(qrT_ref)
    rows = lax.broadcasted_iota(jnp.int32, (KEY_TILE, Q_BLOCK), 0)
    last = (q0 + Q_BLOCK - 1) // KEY_TILE

    def stage_a(slot, s):
        s_scr[slot] = s
        cm_scr[slot] = jnp.max(s, axis=0, keepdims=True)

    def stage_b(slot, vT_tile):
        m_old = m_scr[...]
        m_new = jnp.maximum(m_old, cm_scr[slot])
        alpha = jnp.exp2(m_old - m_new)
        pv = jnp.dot(vT_tile, jnp.exp2(s_scr[slot] - m_new).astype(BF16), preferred_element_type=F32)
        l_scr[...] = alpha * l_scr[...] + pv[HEAD_DIM:HEAD_DIM + 1, :]
        acc_scr[...] = alpha * acc_scr[...] + pv[:HEAD_DIM, :]
        m_scr[...] = m_new

    def causal(kt):
        return per_head(jnp.where((kt * KEY_TILE + rows) <= qpos, 0.0, NEG))

    reset()
    pad_rows = jnp.zeros((MASK_COLS - BLOCKS_PER_TILE, HPG * Q_BLOCK), BF16)

    def sel_scores(kt):
        chosen = sel_scr[pl.ds(pl.multiple_of(kt * BLOCKS_PER_TILE, BLOCKS_PER_TILE), BLOCKS_PER_TILE), :]
        mask_rows = per_head(jnp.where(chosen > 0.5, 0.0, NEG)).astype(BF16)
        return jnp.dot(ks_ref[0, 0, kt], jnp.concatenate([qrT, mask_rows, pad_rows], axis=0),
                       preferred_element_type=F32)

    stage_a(0, sel_scores(0) + causal(0))

    def sel_body(j, carry):
        for slot, kt in ((1, 2 * j + 1), (0, 2 * j + 2)):
            @pl.when(kt < last)
            def _():
                stage_a(slot, sel_scores(kt))
                stage_b(1 - slot, vsT_ref[0, 0, kt - 1])
        return carry

    lax.fori_loop(0, last // 2, sel_body, 0)

    for slot in (0, 1):
        @pl.when((last > 0) & (last % 2 == slot))
        def _():
            stage_a(slot, sel_scores(last) + causal(last))
            stage_b(1 - slot, vsT_ref[0, 0, last - 1])
            stage_b(slot, vsT_ref[0, 0, last])

    @pl.when(last == 0)
    def _():
        stage_b(0, vsT_ref[0, 0, 0])

    o_scr[...] = o_scr[...] + gate_row(1) * acc_scr[...] * (1.0 / l_scr[...])

    reset()

    def win_scores(kt):
        kp = kt * KEY_TILE + rows
        bias = per_head(jnp.where((kp <= qpos) & (kp >= qpos - WINDOW), 0.0, NEG))
        return jnp.dot(kw_ref[0, 0, kt], qrT, preferred_element_type=F32) + bias

    @pl.when(last > 0)
    def _():
        stage_a(0, win_scores(last - 1))
        stage_a(1, win_scores(last))
        stage_b(0, vwT_ref[0, 0, last - 1])
        stage_b(1, vwT_ref[0, 0, last])

    @pl.when(last == 0)
    def _():
        stage_a(0, win_scores(0))
        stage_b(0, vwT_ref[0, 0, 0])

    o = o_scr[...] + gate_row(2) * acc_scr[...] * (1.0 / l_scr[...])
    cols = []
    for hp in range(HPG // 2):
        pair = jnp.concatenate([o[:, (2 * hp) * Q_BLOCK:(2 * hp + 1) * Q_BLOCK],
                                o[:, (2 * hp + 1) * Q_BLOCK:(2 * hp + 2) * Q_BLOCK]], axis=0)
        cols.append(pair.T)
    z = z_ref[0]
    o_ref[0] = (jnp.concatenate(cols, axis=1) * (z * _sigmoid(z))).astype(o_ref.dtype)


def nsa_prompt_attend(qT, qrT, gT, z, ck, cv, ks, vsT, kw, vwT):
    B, inner, T = qT.shape
    n_sel = T // SEL_BLOCK
    nc = ck.shape[1]
    n_cmp = -(-nc // LANES) * LANES
    grp = HPG * HEAD_DIM
    pad_c = ((0, 0), (0, n_cmp - nc), (0, 0), (0, 0))
    ck_p = jnp.pad(ck, pad_c).transpose(0, 2, 1, 3).astype(BF16)
    cvT = jnp.pad(cv, pad_c).transpose(0, 2, 3, 1).astype(BF16)
    cs = jnp.arange(n_cmp) * CMP_STRIDE
    ss = jnp.arange(n_sel) * SEL_BLOCK
    ovT = (jnp.clip(jnp.minimum(cs[None, :] + CMP_BLOCK, ss[:, None] + SEL_BLOCK)
                    - jnp.maximum(cs[None, :], ss[:, None]), 0, None).astype(F32) / CMP_BLOCK)
    ovT = jnp.where(jnp.arange(n_cmp)[None, :] < nc, ovT, 0.0).astype(BF16)

    n_tiles = T // KEY_TILE

    def key_tiles(t):
        return t.reshape(B, n_tiles, KEY_TILE, N_KV, HEAD_DIM).transpose(0, 3, 1, 2, 4)

    def with_block_columns(t):
        cols = (jnp.arange(KEY_TILE)[:, None] // SEL_BLOCK == jnp.arange(MASK_COLS)[None, :]).astype(t.dtype)
        return jnp.concatenate([t, jnp.broadcast_to(cols, t.shape[:-1] + (MASK_COLS,))], axis=-1)

    def val_tiles(t):
        t = t.reshape(B, N_KV, HEAD_DIM, n_tiles, KEY_TILE).transpose(0, 1, 3, 2, 4)
        ones = jnp.ones(t.shape[:3] + (ONES_ROWS, KEY_TILE), t.dtype)
        return jnp.concatenate([t, ones], axis=3)

    q_spec = pl.BlockSpec((1, grp, Q_BLOCK), lambda b, g, i: (b, g, i))
    row_spec = pl.BlockSpec((1, Q_BLOCK, grp), lambda b, g, i: (b, i, g))
    per_group = lambda shape: pl.BlockSpec((1, 1) + shape, lambda b, g, i: (b, g) + (0,) * len(shape))
    return pl.pallas_call(
        functools.partial(_nsa_prompt_kernel, n_cmp=n_cmp, n_sel=n_sel),
        grid=(B, N_KV, T // Q_BLOCK),
        in_specs=[q_spec, q_spec,
                  pl.BlockSpec((1, 3 * HPG, Q_BLOCK), lambda b, g, i: (b, g, i)),
                  row_spec,
                  per_group((n_cmp, HEAD_DIM)), per_group((HEAD_DIM, n_cmp)),
                  pl.BlockSpec((n_sel, n_cmp), lambda b, g, i: (0, 0)),
                  per_group((n_tiles, KEY_TILE, HEAD_DIM + MASK_COLS)),
                  per_group((n_tiles, HEAD_DIM + ONES_ROWS, KEY_TILE)),
                  per_group((n_tiles, KEY_TILE, HEAD_DIM)), per_group((n_tiles, HEAD_DIM + ONES_ROWS, KEY_TILE))],
        out_specs=row_spec,
        out_shape=jax.ShapeDtypeStruct((B, T, inner), BF16),
        scratch_shapes=[pltpu.VMEM((n_sel, Q_BLOCK), F32),
                        pltpu.VMEM((1, HPG * Q_BLOCK), F32),
                        pltpu.VMEM((1, HPG * Q_BLOCK), F32),
                        pltpu.VMEM((HEAD_DIM, HPG * Q_BLOCK), F32),
                        pltpu.VMEM((HEAD_DIM, HPG * Q_BLOCK), F32),
                        pltpu.VMEM((2, KEY_TILE, HPG * Q_BLOCK), F32),
                        pltpu.VMEM((2, 1, HPG * Q_BLOCK), F32)],
        compiler_params=pltpu.CompilerParams(
            dimension_semantics=("arbitrary", "arbitrary", "arbitrary"), vmem_limit_bytes=VMEM_LIMIT_BYTES),
        name="nsa_prompt_attend",
    )(qT, qrT, gT, z, ck_p, cvT, ovT, with_block_columns(key_tiles(ks)), val_tiles(vsT), key_tiles(kw), val_tiles(vwT))


def rms_norm(x, g):
    y = x * lax.rsqrt(jnp.mean(x * x, -1, keepdims=True) + NORM_EPS)
    return y * g


TOKEN_TILE = 256
SUBLANES = 8


def _bdot(a, b):
    return jnp.dot(a.astype(BF16), b.astype(BF16), preferred_element_type=F32)


def _split_dot(x, m, dims):
    hi = x.astype(BF16)
    lo = (x - hi.astype(F32)).astype(BF16)
    return (lax.dot_general(hi, m, (dims, ((), ())), preferred_element_type=F32)
            + lax.dot_general(lo, m, (dims, ((), ())), preferred_element_type=F32))


def _head_sum(x, seg):
    sums = _split_dot(x, seg, ((1,), (0,)))
    return _split_dot(sums, seg, ((1,), (1,)))


def _rms(x, g):
    return x * lax.rsqrt(jnp.mean(x * x, axis=-1, keepdims=True) + NORM_EPS) * g


def _normed_and_prev(h_ref, hprev_ref, shift_ref, g_ref):
    g = g_ref[...]
    xn = _rms(h_ref[0], g)
    prev_last = _rms(hprev_ref[0], g)[SUBLANES - 1:SUBLANES, :]
    first = jnp.where(pl.program_id(1) == 0, shift_ref[0], prev_last)
    row = lax.broadcasted_iota(jnp.int32, (xn.shape[0], 1), 0)
    return xn, jnp.where(row == 0, first, pltpu.roll(xn, 1, 0))


def _softplus(u):
    return jnp.maximum(u, 0.0) + jnp.log(1.0 + jnp.exp(-jnp.abs(u)))


def _sigmoid(u):
    return 1.0 / (1.0 + jnp.exp(-u))


def _rwkv_r_kernel(h_ref, hprev_ref, shift_ref, g_ref, mu_ref, w_ref, w0_ref, w1_ref, w2_ref, r_ref, ld_ref):
    xn, xp = _normed_and_prev(h_ref, hprev_ref, shift_ref, g_ref)
    dx = xp - xn
    r_ref[0] = _bdot(xn + dx * mu_ref[0:1, :], w_ref[...])
    lora = _bdot(jnp.tanh(_bdot(xn + dx * mu_ref[1:2, :], w1_ref[...])), w2_ref[...])
    w_log = -_softplus(-(w0_ref[...] + lora)) - 0.5
    ld_ref[0] = -jnp.exp(w_log)


def _rwkv_k_kernel(h_ref, hprev_ref, shift_ref, g_ref, mu_ref, w_ref, a0_ref, a1_ref, a2_ref, kk_w_ref, ka_ref,
                   seg_ref, k_ref, kk_ref, a_ref):
    xn, xp = _normed_and_prev(h_ref, hprev_ref, shift_ref, g_ref)
    dx = xp - xn
    k = _bdot(xn + dx * mu_ref[0:1, :], w_ref[...])
    a = _sigmoid(a0_ref[...] + _bdot(_bdot(xn + dx * mu_ref[1:2, :], a1_ref[...]), a2_ref[...]))
    kk = k * kk_w_ref[...]
    kk_ref[0] = kk * lax.rsqrt(jnp.maximum(_head_sum(kk * kk, seg_ref[...]), 1e-24))
    k_ref[0] = k * (1.0 + (a - 1.0) * ka_ref[...])
    a_ref[0] = a


def _rwkv_v_kernel(h_ref, hprev_ref, shift_ref, g_ref, mu_ref, w_ref, *rest, residual):
    xn, xp = _normed_and_prev(h_ref, hprev_ref, shift_ref, g_ref)
    mix = xn + (xp - xn) * mu_ref[0:1, :]
    v = _bdot(mix, w_ref[...])
    if residual:
        vfirst_ref, v0_ref, v1_ref, v2_ref, v_ref = rest
        v = v + (vfirst_ref[0] - v) * _sigmoid(v0_ref[...] + _bdot(_bdot(mix, v1_ref[...]), v2_ref[...]))
    else:
        (v_ref,) = rest
    v_ref[0] = v


def _rwkv_z_kernel(h_ref, hprev_ref, shift_ref, g_ref, mu_ref, w_ref, z_ref):
    xn, xp = _normed_and_prev(h_ref, hprev_ref, shift_ref, g_ref)
    z_ref[0] = _bdot(xn + (xp - xn) * mu_ref[0:1, :], w_ref[...])


def _rwkv_post_kernel(y_ref, r_ref, k_ref, v_ref, z_ref, h_ref, lnw_ref, lnb_ref, rk_ref, seg_ref, wout_ref, o_ref):
    seg = seg_ref[...]
    y = y_ref[0]
    d = y - _head_sum(y, seg) * (1.0 / RW_HEAD)
    var = _head_sum(d * d, seg) * (1.0 / RW_HEAD)
    yn = d * lax.rsqrt(var + LNX_EPS) * lnw_ref[...] + lnb_ref[...]
    yn = yn + _head_sum(r_ref[0] * k_ref[0] * rk_ref[...], seg) * v_ref[0]
    z = z_ref[0]
    o_ref[0] = h_ref[0] + _bdot(yn * (z * _sigmoid(z)), wout_ref[...])


def _head_indicator():
    return (jnp.arange(INNER)[:, None] // RW_HEAD == jnp.arange(LANES)[None, :]).astype(BF16)


def _token_call(kernel, operands, out_widths, name, out_dtype=F32):
    B, T = next(a.shape[:2] for kind, a in operands if kind == 'tile')
    tm = TOKEN_TILE if T % TOKEN_TILE == 0 else T
    per_tile = tm // SUBLANES
    specs = []
    for kind, a in operands:
        if kind == 'tile':
            specs.append(pl.BlockSpec((1, tm, a.shape[2]), lambda b, t: (b, t, 0)))
        elif kind == 'prev':
            specs.append(pl.BlockSpec((1, SUBLANES, a.shape[2]), lambda b, t: (b, jnp.maximum(t * per_tile - 1, 0), 0)))
        elif kind == 'batch':
            specs.append(pl.BlockSpec((1, 1, a.shape[2]), lambda b, t: (b, 0, 0)))
        else:
            specs.append(pl.BlockSpec(a.shape, lambda b, t, n=a.ndim: (0,) * n))
    return pl.pallas_call(
        kernel,
        grid=(B, T // tm),
        in_specs=specs,
        out_specs=[pl.BlockSpec((1, tm, w), lambda b, t: (b, t, 0)) for w in out_widths],
        out_shape=[jax.ShapeDtypeStruct((B, T, w), out_dtype) for w in out_widths],
        compiler_params=pltpu.CompilerParams(
            dimension_semantics=("arbitrary", "arbitrary"), vmem_limit_bytes=VMEM_LIMIT_BYTES),
        name=name,
    )(*[a for _, a in operands])


def rwkv_layer(h, shift_prev, s0, v_first, norm_g, mu, w_in, w_out, w0, w1, w2, a0, a1, a2, k_k, k_a, r_k,
               lnx_w, lnx_b, vres):
    B, T, D = h.shape
    row = lambda x: x.reshape(1, -1)
    bf = lambda x: x.astype(BF16)
    seg = _head_indicator()
    common = [('tile', h), ('prev', h), ('batch', shift_prev.reshape(B, 1, D)), ('const', row(norm_g))]
    r, logd = _token_call(
        _rwkv_r_kernel, common + [('const', mu[jnp.array([0, 4])]), ('const', bf(w_in[0])), ('const', row(w0)),
                                  ('const', bf(w1)), ('const', bf(w2))], [INNER, INNER], "rwkv_r")
    k, kk, a = _token_call(
        _rwkv_k_kernel, common + [('const', mu[jnp.array([1, 5])]), ('const', bf(w_in[1])), ('const', row(a0)),
                                  ('const', bf(a1)), ('const', bf(a2)), ('const', row(k_k)), ('const', row(k_a)),
                                  ('const', seg)], [INNER] * 3, "rwkv_k")
    if vres is None:
        (v,) = _token_call(functools.partial(_rwkv_v_kernel, residual=False),
                           common + [('const', mu[2:3]), ('const', bf(w_in[2]))], [INNER], "rwkv_v")
        v_first = v
    else:
        v0, v1, v2 = vres
        (v,) = _token_call(functools.partial(_rwkv_v_kernel, residual=True),
                           common + [('const', mu[2:3]), ('const', bf(w_in[2])), ('tile', v_first), ('const', row(v0)),
                                     ('const', bf(v1)), ('const', bf(v2))], [INNER], "rwkv_v")
    (z,) = _token_call(_rwkv_z_kernel, common + [('const', mu[3:4]), ('const', bf(w_in[3]))], [INNER], "rwkv_z")
    y, s_T = wkv_scan(r, logd, k, v, kk, a, s0, chunk=WKV_CHUNK if T % WKV_CHUNK == 0 else T,
                      t_block=WKV_T_BLOCK if T % WKV_T_BLOCK == 0 else T)
    (h_new,) = _token_call(
        _rwkv_post_kernel,
        [('tile', y), ('tile', r), ('tile', k), ('tile', v), ('tile', z), ('tile', h), ('const', row(lnx_w)),
         ('const', row(lnx_b)), ('const', r_k.reshape(1, INNER)), ('const', seg), ('const', bf(w_out))],
        [D], "rwkv_post")
    return h_new, v_first, s_T


ROT_HALF = ROT_DIM // 2


def _rope_rows(x, cos, sin_lo, sin_hi):
    out = []
    for c in range(x.shape[1] // LANES):
        xc = x[:, c * LANES:(c + 1) * LANES]
        out.append(xc * cos + pltpu.roll(xc, LANES - ROT_HALF, 1) * sin_lo + pltpu.roll(xc, ROT_HALF, 1) * sin_hi)
    return jnp.concatenate(out, axis=1)


def _rope_cols(x, cos, sin):
    n = x.shape[0] // HEAD_DIM
    x = x.reshape(n, HEAD_DIM, x.shape[1])
    x1, x2 = x[:, :ROT_HALF], x[:, ROT_HALF:ROT_DIM]
    y = jnp.concatenate([x1 * cos - x2 * sin, x1 * sin + x2 * cos, x[:, ROT_DIM:]], axis=1)
    return y.reshape(n * HEAD_DIM, y.shape[2])


def _nsa_rows_kernel(h_ref, g_ref, w_ref, cos_ref, slo_ref, shi_ref, cmp_ref, sel_ref, win_ref, z_ref, ks_ref, kw_ref):
    xn = _rms(h_ref[0], g_ref[...])
    p = _bdot(xn, w_ref[...])
    cos, slo, shi = cos_ref[0], slo_ref[0], shi_ref[0]
    cmp_ref[0] = p[:, :2 * KV_W]
    ks = _rope_rows(p[:, 2 * KV_W:3 * KV_W], cos, slo, shi)
    kw = _rope_rows(p[:, 4 * KV_W:5 * KV_W], cos, slo, shi)
    sel_ref[0] = jnp.concatenate([ks, p[:, 3 * KV_W:4 * KV_W]], axis=1)
    win_ref[0] = jnp.concatenate([kw, p[:, 5 * KV_W:6 * KV_W]], axis=1)
    z_ref[0] = p[:, 6 * KV_W:]
    ks_ref[0] = ks.astype(BF16)
    kw_ref[0] = kw.astype(BF16)


def _nsa_cols_kernel(h_ref, g_ref, wT_ref, cos_ref, sin_ref, qT_ref, qrT_ref, vsT_ref, vwT_ref, gT_ref):
    xn = _rms(h_ref[0], g_ref[...]).astype(BF16)
    pT = lax.dot_general(wT_ref[...], xn, (((1,), (1,)), ((), ())), preferred_element_type=F32)
    q = pT[:INNER]
    qT_ref[0] = q.astype(BF16)
    qrT_ref[0] = (_rope_cols(q, cos_ref[...], sin_ref[...]) * LOG2_E).astype(BF16)
    vsT_ref[0] = pT[INNER:INNER + KV_W].astype(BF16)
    vwT_ref[0] = pT[INNER + KV_W:INNER + 2 * KV_W].astype(BF16)
    gT_ref[0] = _sigmoid(pT[INNER + 2 * KV_W:])


def _rope_tables(pos):
    inv = ROPE_THETA ** (-jnp.arange(ROT_HALF, dtype=F32) / ROT_HALF)
    ang = pos.astype(F32)[:, None] * inv[None, :]
    cos, sin = jnp.cos(ang), jnp.sin(ang)
    lane = jnp.arange(LANES) % HEAD_DIM
    f = lane % ROT_HALF
    cos_l = jnp.where(lane[None, :] < ROT_DIM, cos[:, f], 1.0)
    slo_l = jnp.where(lane[None, :] < ROT_HALF, -sin[:, f], 0.0)
    shi_l = jnp.where((lane[None, :] >= ROT_HALF) & (lane[None, :] < ROT_DIM), sin[:, f], 0.0)
    return cos_l, slo_l, shi_l, cos.T, sin.T


def nsa_pre(h, norm_g, w_in, pos):
    B, T, D = h.shape
    tm = TOKEN_TILE if T % TOKEN_TILE == 0 else T
    cos_l, slo_l, shi_l, cosT, sinT = _rope_tables(pos)
    scale = HEAD_DIM ** -0.5
    g_row = norm_g.reshape(1, D)
    w_rows = jnp.concatenate([w_in[:, Q_END:KV_END], w_in[:, G_END:]], axis=1).astype(BF16)
    w_g = w_in[:, KV_END:G_END].reshape(D, N_KV, HPG, 3).transpose(0, 1, 3, 2).reshape(D, 3 * N_HEADS)
    w_cols = jnp.concatenate([w_in[:, :Q_END] * scale, w_in[:, Q_END + 3 * KV_W:Q_END + 4 * KV_W],
                              w_in[:, Q_END + 5 * KV_W:Q_END + 6 * KV_W], w_g], axis=1).T.astype(BF16)
    tile = lambda w: pl.BlockSpec((1, tm, w), lambda b, t: (b, t, 0))
    whole = lambda a: pl.BlockSpec(a.shape, lambda b, t, n=a.ndim: (0,) * n)
    tab = pl.BlockSpec((1, tm, LANES), lambda b, t: (0, t, 0))
    params = pltpu.CompilerParams(dimension_semantics=("arbitrary", "arbitrary"), vmem_limit_bytes=VMEM_LIMIT_BYTES)
    cmp_rows, sel_rows, win_rows, z, ks, kw = pl.pallas_call(
        _nsa_rows_kernel,
        grid=(B, T // tm),
        in_specs=[tile(D), whole(g_row), whole(w_rows), tab, tab, tab],
        out_specs=[tile(2 * KV_W), tile(2 * KV_W), tile(2 * KV_W), tile(INNER), tile(KV_W), tile(KV_W)],
        out_shape=[jax.ShapeDtypeStruct((B, T, 2 * KV_W), F32)] * 3 + [jax.ShapeDtypeStruct((B, T, INNER), F32)]
        + [jax.ShapeDtypeStruct((B, T, KV_W), BF16)] * 2,
        compiler_params=params, name="nsa_rows",
    )(h, g_row, w_rows, cos_l[None], slo_l[None], shi_l[None])
    colt = lambda r: pl.BlockSpec((1, r, tm), lambda b, t: (b, 0, t))
    tabT = pl.BlockSpec((ROT_HALF, tm), lambda b, t: (0, t))
    qT, qrT, vsT, vwT, gT = pl.pallas_call(
        _nsa_cols_kernel,
        grid=(B, T // tm),
        in_specs=[tile(D), whole(g_row), whole(w_cols), tabT, tabT],
        out_specs=[colt(INNER), colt(INNER), colt(KV_W), colt(KV_W), colt(3 * N_HEADS)],
        out_shape=[jax.ShapeDtypeStruct((B, INNER, T), BF16)] * 2 + [jax.ShapeDtypeStruct((B, KV_W, T), BF16)] * 2
        + [jax.ShapeDtypeStruct((B, 3 * N_HEADS, T), F32)],
        compiler_params=params, name="nsa_cols",
    )(h, g_row, w_cols, cosT, sinT)
    return cmp_rows, sel_rows, win_rows, z, ks, kw, qT, qrT, vsT, vwT, gT


PAGE = 128
SAMPLE_VMEM_LIMIT_BYTES = 56 * 1024 * 1024


def _gather_pages(pt_ref, b, pool_hbm, buf, sem, n_pages):
    rows = pool_hbm.shape[1]
    copies = [pltpu.make_async_copy(pool_hbm.at[pt_ref[b, p]], buf.at[pl.ds(p * rows, rows)], sem)
              for p in range(n_pages)]
    for cp in copies:
        cp.start()
    for cp in copies:
        cp.wait()


def _sample_compress_kernel(pt_ref, pool_hbm, w1_hbm, pe_ref, w2_ref, out_ref, buf, w1_vmem, sem, wsem, *, n_pages):
    b = pl.program_id(0)

    @pl.when(b == 0)
    def _():
        cp = pltpu.make_async_copy(w1_hbm, w1_vmem, wsem)
        cp.start()
        cp.wait()

    _gather_pages(pt_ref, b, pool_hbm, buf, sem, n_pages)
    n_chunks = n_pages * PAGE // CMP_STRIDE
    row = lax.broadcasted_iota(jnp.int32, (n_chunks, 1), 0)
    for kv in range(2):
        lanes = slice(kv * KV_W, (kv + 1) * KV_W)
        parts = []
        for m in range(CMP_RATIO):
            acc = jnp.zeros((n_chunks, w1_vmem.shape[3]), F32)
            for s in range(CMP_STRIDE):
                at = s * 2 * KV_W + kv * KV_W
                x = buf[:, at:at + KV_W] + pe_ref[m * CMP_STRIDE + s:m * CMP_STRIDE + s + 1, :]
                acc = acc + jnp.dot(x.astype(BF16), w1_vmem[kv, m * CMP_STRIDE + s], preferred_element_type=F32)
            parts.append(acc)
        pre = parts[0] + pltpu.roll(parts[1], n_chunks - 1, 0)
        hid = pre * _sigmoid(pre)
        ck = jnp.dot(hid.astype(BF16), w2_ref[kv], preferred_element_type=F32)
        out_ref[0, :, lanes] = jnp.where(row < n_chunks - 1, ck, 0.0)


def _block_diag(w):
    eye = jnp.eye(N_KV, dtype=w.dtype)
    out = eye[:, None, :, None] * w[..., None, :, None, :]
    return out.reshape(w.shape[:-2] + (N_KV * w.shape[-2], N_KV * w.shape[-1]))


def sample_compress(pool, page_table, pe, w1, w2):
    B, n_pages = page_table.shape
    n_chunks = n_pages * PAGE // CMP_STRIDE
    pool2 = pool.reshape(pool.shape[0], PAGE // CMP_STRIDE, CMP_STRIDE * 2 * KV_W)
    w1_bd = _block_diag(w1).astype(BF16)
    w2_bd = _block_diag(w2).astype(BF16)
    pe_t = jnp.tile(pe, (1, N_KV))
    return pl.pallas_call(
        functools.partial(_sample_compress_kernel, n_pages=n_pages),
        grid_spec=pltpu.PrefetchScalarGridSpec(
            num_scalar_prefetch=1,
            grid=(B,),
            in_specs=[pl.BlockSpec(memory_space=pl.ANY), pl.BlockSpec(memory_space=pl.ANY),
                      pl.BlockSpec(pe_t.shape, lambda b, pt: (0, 0)),
                      pl.BlockSpec(w2_bd.shape, lambda b, pt: (0, 0, 0))],
            out_specs=pl.BlockSpec((1, n_chunks, 2 * KV_W), lambda b, pt: (b, 0, 0)),
            scratch_shapes=[pltpu.VMEM((n_chunks, CMP_STRIDE * 2 * KV_W), F32),
                            pltpu.VMEM(w1_bd.shape, BF16),
                            pltpu.SemaphoreType.DMA(()), pltpu.SemaphoreType.DMA(())]),
        out_shape=jax.ShapeDtypeStruct((B, n_chunks, 2 * KV_W), F32),
        compiler_params=pltpu.CompilerParams(
            dimension_semantics=("arbitrary",), vmem_limit_bytes=SAMPLE_VMEM_LIMIT_BYTES),
        name="sample_compress",
    )(page_table, pool2, w1_bd, pe_t, w2_bd)


SAMPLE_TILE = 512


def _online_step(s, v, m_ref, l_ref, acc_ref):
    m_old = m_ref[...]
    m_new = jnp.maximum(m_old, jnp.max(s, axis=0, keepdims=True))
    alpha = jnp.exp2(m_old - m_new)
    p = jnp.exp2(s - m_new)
    l_ref[...] = alpha * l_ref[...] + jnp.sum(p, axis=0, keepdims=True)
    acc_ref[...] = alpha * acc_ref[...] + lax.dot_general(
        v.astype(BF16), p.astype(BF16), (((0,), (0,)), ((), ())), preferred_element_type=F32)
    m_ref[...] = m_new


def _split_dot_left(m, x):
    hi = x.astype(BF16)
    lo = (x - hi.astype(F32)).astype(BF16)
    return jnp.dot(m, hi, preferred_element_type=F32) + jnp.dot(m, lo, preferred_element_type=F32)


def _sample_attend_kernel(pt_ref, pool_hbm, ckv_ref, qraw_ref, qrot_ref, g_ref, selnew_ref, win_ref, winnew_ref,
                          ovT_ref, fold_ref, o_ref, buf, sel_scr, m_scr, l_scr, acc_scr, sem,
                          *, n_pages, n_cmp, n_sel, past, t_new):
    b = pl.program_id(0)
    _gather_pages(pt_ref, b, pool_hbm, buf, sem, n_pages)
    width = qraw_ref.shape[2]
    q_idx = lax.broadcasted_iota(jnp.int32, (1, width), 1) % t_new
    qpos = past + q_idx

    n_rows = ckv_ref.shape[1]
    n_iota = lax.broadcasted_iota(jnp.int32, (n_rows, width), 0)
    c_valid = (n_iota < n_cmp) & ((n_iota * CMP_STRIDE + (CMP_BLOCK - 1)) <= qpos)
    s = jnp.dot(ckv_ref[0, :, :KV_W].astype(BF16), qraw_ref[0], preferred_element_type=F32)
    s = jnp.where(c_valid, s, NEG)
    mx = jnp.max(s, axis=0, keepdims=True)
    e = jnp.where(c_valid, jnp.exp(s - mx), 0.0)
    l = jnp.sum(e, axis=0, keepdims=True)
    p = e * (1.0 / jnp.where(l > 0.0, l, 1.0))
    o_ref[0] = g_ref[0, 0:1, :] * lax.dot_general(
        ckv_ref[0, :, KV_W:].astype(BF16), p.astype(BF16), (((0,), (0,)), ((), ())), preferred_element_type=F32)

    fold = fold_ref[...]
    p_sum = _split_dot(p, fold, ((1,), (0,)))
    imp = _split_dot_left(ovT_ref[...], p_sum)
    j_iota = lax.broadcasted_iota(jnp.int32, imp.shape, 0)
    cur = (past + lax.broadcasted_iota(jnp.int32, (1, imp.shape[1]), 1) % t_new) // SEL_BLOCK
    forced = (j_iota == 0) | (j_iota == cur) | (j_iota == cur - 1)
    score = jnp.where(j_iota <= cur, imp + jnp.where(forced, FORCE_BONUS, 0.0), NEG)
    sel = jnp.zeros(imp.shape, F32)
    for _ in range(TOP_N):
        mx = jnp.max(score, axis=0, keepdims=True)
        first = jnp.min(jnp.where(score == mx, j_iota, n_sel), axis=0, keepdims=True)
        pick = j_iota == first
        sel = jnp.where(pick & (mx > NEG / 2), 1.0, sel)
        score = jnp.where(pick, PICKED, score)
    sel_scr[...] = lax.dot_general(sel.astype(BF16), fold, (((1,), (1,)), ((), ())), preferred_element_type=F32)

    def reset():
        m_scr[...] = jnp.full(m_scr.shape, NEG, F32)
        l_scr[...] = jnp.zeros(l_scr.shape, F32)
        acc_scr[...] = jnp.zeros(acc_scr.shape, F32)

    qrot = qrot_ref[0]
    new_rows = lax.broadcasted_iota(jnp.int32, (t_new, width), 0)
    causal_new = new_rows <= q_idx

    reset()
    blocks_per_tile = SAMPLE_TILE // SEL_BLOCK
    for kt in range(n_pages * PAGE // SAMPLE_TILE):
        chosen = sel_scr[kt * blocks_per_tile:(kt + 1) * blocks_per_tile, :]
        chosen = jnp.concatenate(
            [jnp.broadcast_to(chosen[i:i + 1, :], (SEL_BLOCK, width)) for i in range(blocks_per_tile)], axis=0)
        rows = buf[kt * SAMPLE_TILE:(kt + 1) * SAMPLE_TILE, :]
        s = jnp.dot(rows[:, :KV_W].astype(BF16), qrot, preferred_element_type=F32)
        _online_step(s + jnp.where(chosen > 0.5, 0.0, NEG), rows[:, KV_W:], m_scr, l_scr, acc_scr)
    last = past // SEL_BLOCK
    allowed = causal_new & (sel_scr[last:last + 1, :] > 0.5)
    s = jnp.dot(selnew_ref[0, :, :KV_W].astype(BF16), qrot, preferred_element_type=F32)
    _online_step(s + jnp.where(allowed, 0.0, NEG), selnew_ref[0, :, KV_W:], m_scr, l_scr, acc_scr)
    o_ref[0] = o_ref[0] + g_ref[0, 1:2, :] * acc_scr[...] * (1.0 / l_scr[...])

    reset()
    n_buf = win_ref.shape[1]
    w_pos = past - n_buf + lax.broadcasted_iota(jnp.int32, (n_buf, width), 0)
    allowed = (w_pos >= qpos - WINDOW) & (w_pos >= 0)
    s = jnp.dot(win_ref[0, :, :KV_W].astype(BF16), qrot, preferred_element_type=F32)
    _online_step(s + jnp.where(allowed, 0.0, NEG), win_ref[0, :, KV_W:], m_scr, l_scr, acc_scr)
    s = jnp.dot(winnew_ref[0, :, :KV_W].astype(BF16), qrot, preferred_element_type=F32)
    _online_step(s + jnp.where(causal_new, 0.0, NEG), winnew_ref[0, :, KV_W:], m_scr, l_scr, acc_scr)
    o_ref[0] = o_ref[0] + g_ref[0, 2:3, :] * acc_scr[...] * (1.0 / l_scr[...])


def sample_attend(pool, page_table, ckv, qT, qrT, gT, sel_new, win_buf, win_new):
    B, n_pages = page_table.shape
    t_new = sel_new.shape[1]
    past = n_pages * PAGE
    width = N_HEADS * t_new
    assert t_new < CMP_STRIDE and t_new <= SEL_BLOCK and past % SAMPLE_TILE == 0 and win_buf.shape[1] <= past
    assert t_new <= win_buf.shape[1] == min(WINDOW, past)
    n_cmp = past // CMP_STRIDE - CMP_RATIO + 1
    n_sel = -(-(past // SEL_BLOCK + 1) // SUBLANES) * SUBLANES

    def block_q(x):
        x = x.reshape(N_KV, HPG, HEAD_DIM, B, t_new).transpose(3, 0, 2, 1, 4)
        eye = jnp.eye(N_KV, dtype=x.dtype)
        x = x[:, :, :, None, :, :] * eye[None, :, None, :, None, None]
        return x.reshape(B, KV_W, width)

    gates = gT.reshape(N_KV, 3, HPG, B, t_new).transpose(3, 1, 0, 2, 4).reshape(B, 3, width)
    cs = jnp.arange(ckv.shape[1]) * CMP_STRIDE
    ss = jnp.arange(n_sel) * SEL_BLOCK
    ovT = (jnp.clip(jnp.minimum(cs[None, :] + CMP_BLOCK, ss[:, None] + SEL_BLOCK)
                    - jnp.maximum(cs[None, :], ss[:, None]), 0, None).astype(F32) / CMP_BLOCK)
    ovT = jnp.where(jnp.arange(ckv.shape[1])[None, :] < n_cmp, ovT, 0.0).astype(BF16)
    lane = jnp.arange(width)
    col = (lane // (HPG * t_new)) * t_new + lane % t_new
    fold = (col[:, None] == jnp.arange(LANES)[None, :]).astype(BF16)
    per_b = lambda a: pl.BlockSpec((1,) + a.shape[1:], lambda b, pt, n=a.ndim: (b,) + (0,) * (n - 1))
    whole = lambda a: pl.BlockSpec(a.shape, lambda b, pt, n=a.ndim: (0,) * n)
    pool2 = pool.reshape(pool.shape[0], PAGE, 2 * KV_W)
    qraw, qrot = block_q(qT), block_q(qrT)
    o_bd = pl.pallas_call(
        functools.partial(_sample_attend_kernel, n_pages=n_pages, n_cmp=n_cmp, n_sel=n_sel, past=past, t_new=t_new),
        grid_spec=pltpu.PrefetchScalarGridSpec(
            num_scalar_prefetch=1,
            grid=(B,),
            in_specs=[pl.BlockSpec(memory_space=pl.ANY), per_b(ckv), per_b(qraw), per_b(qrot), per_b(gates),
                      per_b(sel_new), per_b(win_buf), per_b(win_new), whole(ovT), whole(fold)],
            out_specs=pl.BlockSpec((1, KV_W, width), lambda b, pt: (b, 0, 0)),
            scratch_shapes=[pltpu.VMEM((past, 2 * KV_W), F32),
                            pltpu.VMEM((n_sel, width), F32),
                            pltpu.VMEM((1, width), F32), pltpu.VMEM((1, width), F32),
                            pltpu.VMEM((KV_W, width), F32),
                            pltpu.SemaphoreType.DMA(())]),
        out_shape=jax.ShapeDtypeStruct((B, KV_W, width), F32),
        compiler_params=pltpu.CompilerParams(dimension_semantics=("arbitrary",), vmem_limit_bytes=VMEM_LIMIT_BYTES),
        name="sample_attend",
    )(page_table, pool2, ckv, qraw, qrot, gates, sel_new, win_buf, win_new, ovT, fold)
    o = o_bd.reshape(B, N_KV, HEAD_DIM, N_KV, HPG, t_new)
    o = jnp.stack([o[:, g, :, g] for g in range(N_KV)], axis=1)
    return o.transpose(0, 4, 1, 3, 2).reshape(B, t_new, INNER)


def compress(k, v, pe, w1, w2):
    def phi(t, a, b):
        B, T = t.shape[:2]
        n_chunk = T // CMP_STRIDE
        nc = n_chunk - CMP_RATIO + 1
        chunks = t[:, :n_chunk * CMP_STRIDE].reshape(B, n_chunk, CMP_STRIDE, N_KV, HEAD_DIM)
        parts = []
        for m in range(CMP_RATIO):
            sl = slice(m * CMP_STRIDE, (m + 1) * CMP_STRIDE)
            part = jnp.einsum('bcsgd,sdh->bcgh', chunks + pe[None, None, sl, None, :], a[sl])
            parts.append(part[:, m:m + nc])
        h = jax.nn.silu(sum(parts))
        return jnp.einsum('bngh,hd->bngd', h, b)
    ck = phi(k, w1[0], w2[0])
    cv = phi(v, w1[1], w2[1])
    c_end = jnp.arange(ck.shape[1]) * CMP_STRIDE + CMP_BLOCK - 1
    return ck, cv, c_end


def _residual_matmul_kernel(x_ref, h_ref, w_ref, o_ref):
    o_ref[0] = h_ref[0] + jnp.dot(x_ref[0], w_ref[...], preferred_element_type=F32)


def nsa_prompt(h, norm_g, w_in, w_out, pe, cw1, cw2):
    B, T, _ = h.shape
    cmp_rows, sel_rows, win_rows, z, ks, kw, qT, qrT, vsT, vwT, gT = nsa_pre(h, norm_g, w_in, jnp.arange(T))
    heads = lambda t: t.reshape(B, T, N_KV, HEAD_DIM)
    ck, cv, _ = compress(heads(cmp_rows[..., :KV_W]), heads(cmp_rows[..., KV_W:]), pe, cw1, cw2)
    gated = nsa_prompt_attend(qT, qrT, gT, z, ck, cv, ks, vsT, kw, vwT)
    (h_new,) = _token_call(_residual_matmul_kernel, [('tile', gated), ('tile', h), ('const', w_out.astype(BF16))],
                           [D_MODEL], "nsa_out")
    n_keep = min(WINDOW, T)
    rows = lambda t: t.reshape(B, t.shape[1], 2, N_KV, HEAD_DIM)
    return h_new, rows(cmp_rows), rows(sel_rows), rows(win_rows[:, -n_keep:])


def nsa_sample(h, norm_g, pool_cmp, pool_sel, win_buf, page_table, w_in, w_out, pe, cw1, cw2):
    B, T, D = h.shape
    past = page_table.shape[1] * pool_cmp.shape[1]
    pos = past + jnp.arange(B * T) % T
    cmp_rows, sel_rows, win_rows, z, _, _, qT, qrT, _, _, gT = nsa_pre(h.reshape(1, B * T, D), norm_g, w_in, pos)
    rows = lambda t: t.reshape(B, T, 2 * KV_W)
    ckv = sample_compress(pool_cmp, page_table, pe, cw1, cw2)
    n_buf = win_buf.shape[1]
    win_flat = win_buf.reshape(B, n_buf, 2 * KV_W)
    o = sample_attend(pool_sel, page_table, ckv, qT[0], qrT[0], gT[0], rows(sel_rows), win_flat, rows(win_rows))
    gated = (o * jax.nn.silu(z.reshape(B, T, INNER))).astype(BF16)
    (h_new,) = _token_call(_residual_matmul_kernel, [('tile', gated), ('tile', h), ('const', w_out.astype(BF16))],
                           [D_MODEL], "nsa_out")
    n_keep = min(WINDOW, n_buf + T)
    win_out = jnp.concatenate([win_flat, rows(win_rows)], axis=1)[:, -n_keep:]
    split = lambda t: t.reshape(B, t.shape[1], 2, N_KV, HEAD_DIM)
    return h_new, split(rows(cmp_rows)), split(rows(sel_rows)), split(win_out)


def kernel(x_prompt, x_sample, state_rwkv_wkv, state_rwkv_shift, cache_nsa_cmp, cache_nsa_sel, cache_nsa_win, page_table, norm_g, final_norm_g, rwkv_mu, rwkv_w_in, rwkv_w_out, rwkv_w0, rwkv_w1, rwkv_w2, rwkv_a0, rwkv_a1, rwkv_a2, rwkv_v0, rwkv_v1, rwkv_v2, rwkv_k_k, rwkv_k_a, rwkv_r_k, rwkv_lnx_w, rwkv_lnx_b, nsa_w_in, nsa_w_out, nsa_cmp_pe, nsa_cmp_w1, nsa_cmp_w2):

    def trunk(x, wkv0, shift0, sample):
        h = x
        v_first = None
        wkv, shift, cmp_rows, sel_rows, win_rows = [], [], [], [], []
        for layer in range(DEPTH):
            j = layer // N_MIXERS
            if layer % N_MIXERS == 0:
                vres = None if j == 0 else (rwkv_v0[j - 1], rwkv_v1[j - 1], rwkv_v2[j - 1])
                shift.append(rms_norm(h[:, -1], norm_g[layer]))
                h, v_first, s_T = rwkv_layer(
                    h, shift0[j], wkv0[j], v_first, norm_g[layer], rwkv_mu[j], rwkv_w_in[j], rwkv_w_out[j],
                    rwkv_w0[j], rwkv_w1[j], rwkv_w2[j], rwkv_a0[j], rwkv_a1[j], rwkv_a2[j],
                    rwkv_k_k[j], rwkv_k_a[j], rwkv_r_k[j], rwkv_lnx_w[j], rwkv_lnx_b[j], vres)
                wkv.append(s_T)
            else:
                if sample:
                    h, c, s, w = nsa_sample(h, norm_g[layer], cache_nsa_cmp[j], cache_nsa_sel[j], cache_nsa_win[j],
                                            page_table, nsa_w_in[j], nsa_w_out[j], nsa_cmp_pe[j],
                                            nsa_cmp_w1[j], nsa_cmp_w2[j])
                else:
                    h, c, s, w = nsa_prompt(h, norm_g[layer], nsa_w_in[j], nsa_w_out[j], nsa_cmp_pe[j],
                                            nsa_cmp_w1[j], nsa_cmp_w2[j])
                cmp_rows.append(c)
                sel_rows.append(s)
                win_rows.append(w)
        return (rms_norm(h, final_norm_g), jnp.stack(wkv), jnp.stack(shift),
                jnp.stack(cmp_rows), jnp.stack(sel_rows), jnp.stack(win_rows))

    zeros_wkv = jnp.zeros((rwkv_mu.shape[0], x_prompt.shape[0], RW_HEADS, RW_HEAD, RW_HEAD), x_prompt.dtype)
    zeros_shift = jnp.zeros((rwkv_mu.shape[0], x_prompt.shape[0], D_MODEL), x_prompt.dtype)
    y_prompt, wkv_p, shift_p, cmp_p, sel_p, win_p = trunk(x_prompt, zeros_wkv, zeros_shift, False)
    y_sample, wkv_s, shift_s, cmp_s, sel_s, win_s = trunk(x_sample, state_rwkv_wkv, state_rwkv_shift, True)
    return (y_prompt, y_sample, wkv_p, wkv_s, shift_p, shift_s, cmp_p, cmp_s, sel_p, sel_s, win_p, win_s)
```

```python
import functools

import jax
import jax.numpy as jnp
from jax import lax
from jax.experimental import pallas as pl
from jax.experimental.pallas import tpu as pltpu

D_MODEL = 1024
DEPTH = 4
N_MIXERS = 2
INNER = 2 * D_MODEL
NORM_EPS = 1e-6
RW_HEAD = 64
RW_HEADS = INNER // RW_HEAD
LNX_EPS = 64e-5
HEAD_DIM = 64
N_HEADS = INNER // HEAD_DIM
N_KV = 4
HPG = N_HEADS // N_KV
KV_W = N_KV * HEAD_DIM
ROT_DIM = HEAD_DIM // 4
ROPE_THETA = 500000.0
CMP_BLOCK = 32
CMP_STRIDE = 16
CMP_RATIO = CMP_BLOCK // CMP_STRIDE
SEL_BLOCK = 64
TOP_N = 16
WINDOW = 512
Q_BLOCK = 128
NEG = -1e30
FORCE_BONUS = 1e4
Q_END = INNER
KV_END = Q_END + 6 * KV_W
G_END = KV_END + 3 * N_HEADS
NSA_IN = G_END + INNER

F32 = jnp.float32
BF16 = jnp.bfloat16
VMEM_LIMIT_BYTES = 48 * 1024 * 1024


LANES = 128
WKV_CHUNK = 64
WKV_T_BLOCK = 256
WKV_PAIRS = 16


def _dot(a, b):
    return jnp.dot(a.astype(BF16), b.astype(BF16), preferred_element_type=F32)


def _dot_nt(a, b):
    return lax.dot_general(a.astype(BF16), b.astype(BF16), (((1,), (1,)), ((), ())), preferred_element_type=F32)


def _dot_tn(a, b):
    return lax.dot_general(a.astype(BF16), b.astype(BF16), (((0,), (0,)), ((), ())), preferred_element_type=F32)


def _wkv_kernel(r_ref, ld_ref, k_ref, v_ref, kk_ref, a_ref, s0_ref, y_ref, st_ref, s_scr, *, chunk, n_chunks, n_pairs):
    C = chunk
    R = 2 * C
    tb = pl.program_id(2)

    @pl.when(tb == 0)
    def _():
        z = jnp.zeros((RW_HEAD, RW_HEAD), F32)
        for g in range(n_pairs):
            s_scr[g] = jnp.concatenate(
                [jnp.concatenate([s0_ref[0, 2 * g], z], axis=1),
                 jnp.concatenate([z, s0_ref[0, 2 * g + 1]], axis=1)], axis=0)

    lane = lax.broadcasted_iota(jnp.int32, (1, LANES), 1)
    head_a = lane < RW_HEAD
    row = lax.broadcasted_iota(jnp.int32, (R, R), 0)
    col = lax.broadcasted_iota(jnp.int32, (R, R), 1)
    same = (row // C) == (col // C)
    strict = same & ((col % C) < (row % C))
    incl = same & ((col % C) <= (row % C))
    eye = (row == col).astype(F32)
    tr = lax.broadcasted_iota(jnp.int32, (C, C), 0)
    tc = lax.broadcasted_iota(jnp.int32, (C, C), 1)
    tri = (tc <= tr).astype(BF16)

    def stack(z):
        return jnp.concatenate([jnp.where(head_a, z, 0.0), jnp.where(head_a, 0.0, z)], axis=0)

    def body(ci, carry):
        sl = pl.ds(pl.multiple_of(ci * C, C), C)
        G = range(n_pairs)
        ld = [ld_ref[0, sl, g * LANES:(g + 1) * LANES] for g in G]
        ld_hi = [x.astype(BF16) for x in ld]
        ld_lo = [(x - h.astype(F32)).astype(BF16) for x, h in zip(ld, ld_hi)]
        cum = [jnp.dot(tri, h, preferred_element_type=F32) + jnp.dot(tri, l, preferred_element_type=F32)
               for h, l in zip(ld_hi, ld_lo)]
        e_neg = [jnp.exp(-c) for c in cum]
        kk = [kk_ref[0, sl, g * LANES:(g + 1) * LANES] for g in G]
        a_s = [stack(-kk[g] * jnp.exp(cum[g] - ld[g])) for g in G]
        r_s = [stack(r_ref[0, sl, g * LANES:(g + 1) * LANES] * jnp.exp(cum[g])) for g in G]
        b_s = [stack(kk[g] * a_ref[0, sl, g * LANES:(g + 1) * LANES] * e_neg[g]) for g in G]
        k_s = [stack(k_ref[0, sl, g * LANES:(g + 1) * LANES] * e_neg[g]) for g in G]
        v_s = [stack(v_ref[0, sl, g * LANES:(g + 1) * LANES]) for g in G]
        s2 = [s_scr[g] for g in G]
        ab = [jnp.where(strict, _dot_nt(a_s[g], b_s[g]), 0.0) for g in G]
        ak = [jnp.where(strict, _dot_nt(a_s[g], k_s[g]), 0.0) for g in G]
        rb = [jnp.where(incl, _dot_nt(r_s[g], b_s[g]), 0.0) for g in G]
        rk = [jnp.where(incl, _dot_nt(r_s[g], k_s[g]), 0.0) for g in G]
        rhs = [_dot_nt(a_s[g], s2[g]) + _dot(ak[g], v_s[g]) for g in G]
        y0 = [_dot_nt(r_s[g], s2[g]) + _dot(rk[g], v_s[g]) for g in G]
        tm = [eye + ab[g] for g in G]
        p = ab
        n = 2
        while n < C:
            p = [_dot(p[g], p[g]) for g in G]
            tm = [tm[g] + _dot(tm[g], p[g]) for g in G]
            n *= 2
        u_s = [_dot(tm[g], rhs[g]) for g in G]
        y_s = [y0[g] + _dot(rb[g], u_s[g]) for g in G]
        for g in G:
            y_ref[0, sl, g * LANES:(g + 1) * LANES] = y_s[g][:C] + y_s[g][C:]
            gamma = jnp.exp(cum[g][C - 1:C, :])
            s_scr[g] = (s2[g] + _dot_tn(u_s[g], b_s[g]) + _dot_tn(v_s[g], k_s[g])) * gamma
        return carry

    lax.fori_loop(0, n_chunks, body, 0)

    @pl.when(tb == pl.num_programs(2) - 1)
    def _():
        for g in range(n_pairs):
            st_ref[0, 2 * g] = s_scr[g, :RW_HEAD, :RW_HEAD]
            st_ref[0, 2 * g + 1] = s_scr[g, RW_HEAD:, RW_HEAD:]


def wkv_scan(r, logd, k, v, kk, a, s0, *, chunk, t_block):
    B, T, inner = r.shape
    width = WKV_PAIRS * LANES
    assert T % t_block == 0 and t_block % chunk == 0 and inner % width == 0
    seq_spec = pl.BlockSpec((1, t_block, width), lambda b, p, t: (b, t, p))
    st_spec = pl.BlockSpec((1, 2 * WKV_PAIRS, RW_HEAD, RW_HEAD), lambda b, p, t: (b, p, 0, 0))
    return pl.pallas_call(
        functools.partial(_wkv_kernel, chunk=chunk, n_chunks=t_block // chunk, n_pairs=WKV_PAIRS),
        grid=(B, inner // width, T // t_block),
        in_specs=[seq_spec] * 6 + [st_spec],
        out_specs=[seq_spec, st_spec],
        out_shape=[jax.ShapeDtypeStruct((B, T, inner), F32), jax.ShapeDtypeStruct(s0.shape, F32)],
        scratch_shapes=[pltpu.VMEM((WKV_PAIRS, LANES, LANES), F32)],
        compiler_params=pltpu.CompilerParams(
            dimension_semantics=("arbitrary", "arbitrary", "arbitrary"), vmem_limit_bytes=VMEM_LIMIT_BYTES),
        name="wkv_scan",
    )(r, logd, k, v, kk, a, s0)


KEY_TILE = 512
assert WINDOW <= KEY_TILE and KEY_TILE % Q_BLOCK == 0
BLOCKS_PER_TILE = KEY_TILE // SEL_BLOCK
MASK_COLS = 16
ONES_ROWS = 16
LOG2_E = 1.4426950408889634
PICKED = -3e38


def _nsa_prompt_kernel(qT_ref, qrT_ref, gT_ref, z_ref, ck_ref, cvT_ref, ovT_ref, ks_ref, vsT_ref, kw_ref, vwT_ref,
                       o_ref, sel_scr, m_scr, l_scr, acc_scr, o_scr, s_scr, cm_scr, *, n_cmp, n_sel):
    i = pl.program_id(2)
    q0 = i * Q_BLOCK
    qpos = q0 + lax.broadcasted_iota(jnp.int32, (1, Q_BLOCK), 1)

    def heads_on_lanes(ref):
        return jnp.concatenate([ref[0, h * HEAD_DIM:(h + 1) * HEAD_DIM, :] for h in range(HPG)], axis=1)

    def gate_row(br):
        return jnp.concatenate([gT_ref[0, br * HPG + h:br * HPG + h + 1, :] for h in range(HPG)], axis=1)

    def per_head(x):
        return jnp.concatenate([x] * HPG, axis=1)

    n_iota = lax.broadcasted_iota(jnp.int32, (n_cmp, Q_BLOCK), 0)
    c_valid = per_head((n_iota * CMP_STRIDE + (CMP_BLOCK - 1)) <= qpos)
    s = jnp.dot(ck_ref[0, 0], heads_on_lanes(qT_ref), preferred_element_type=F32)
    s = jnp.where(c_valid, s, NEG)
    mx = jnp.max(s, axis=0, keepdims=True)
    e = jnp.where(c_valid, jnp.exp(s - mx), 0.0)
    l = jnp.sum(e, axis=0, keepdims=True)
    p = e * (1.0 / jnp.where(l > 0.0, l, 1.0))
    o_scr[...] = gate_row(0) * jnp.dot(cvT_ref[0, 0], p.astype(BF16), preferred_element_type=F32)
    p_sum = p[:, :Q_BLOCK]
    for h in range(1, HPG):
        p_sum = p_sum + p[:, h * Q_BLOCK:(h + 1) * Q_BLOCK]

    ps_hi = p_sum.astype(BF16)
    ps_lo = (p_sum - ps_hi.astype(F32)).astype(BF16)
    ovT = ovT_ref[...]
    imp = jnp.dot(ovT, ps_hi, preferred_element_type=F32) + jnp.dot(ovT, ps_lo, preferred_element_type=F32)
    j_iota = lax.broadcasted_iota(jnp.int32, (n_sel, Q_BLOCK), 0)
    cur = qpos // SEL_BLOCK
    forced = (j_iota == 0) | (j_iota == cur) | (j_iota == cur - 1)
    score = jnp.where(j_iota <= cur, imp + jnp.where(forced, FORCE_BONUS, 0.0), NEG)
    sel = jnp.zeros((n_sel, Q_BLOCK), F32)
    for _ in range(TOP_N):
        mx = jnp.max(score, axis=0, keepdims=True)
        first = jnp.min(jnp.where(score == mx, j_iota, n_sel), axis=0, keepdims=True)
        pick = j_iota == first
        sel = jnp.where(pick & (mx > NEG / 2), 1.0, sel)
        score = jnp.where(pick, PICKED, score)
    sel_scr[...] = sel

    def reset():
        m_scr[...] = jnp.full(m_scr.shape, NEG, F32)
        l_scr[...] = jnp.zeros(l_scr.shape, F32)
        acc_scr[...] = jnp.zeros(acc_scr.shape, F32)

    qrT = heads_on_lanes(qrT_ref)
    rows = lax.broadcasted_iota(jnp.int32, (KEY_TILE, Q_BLOCK), 0)
    last = (q0 + Q_BLOCK - 1) // KEY_TILE

    def stage_a(slot, s):
        s_scr[slot] = s
        cm_scr[slot] = jnp.max(s, axis=0, keepdims=True)

    def stage_b(slot, vT_tile):
        m_old = m_scr[...]
        m_new = jnp.maximum(m_old, cm_scr[slot])
        alpha = jnp.exp2(m_old - m_new)
        pv = jnp.dot(vT_tile, jnp.exp2(s_scr[slot] - m_new).astype(BF16), preferred_element_type=F32)
        l_scr[...] = alpha * l_scr[...] + pv[HEAD_DIM:HEAD_DIM + 1, :]
        acc_scr[...] = alpha * acc_scr[...] + pv[:HEAD_DIM, :]
        m_scr[...] = m_new

    def causal(kt):
        return per_head(jnp.where((kt * KEY_TILE + rows) <= qpos, 0.0, NEG))

    reset()
    pad_rows = jnp.zeros((MASK_COLS - BLOCKS_PER_TILE, HPG * Q_BLOCK), BF16)

    def sel_scores(kt):
        chosen = sel_scr[pl.ds(pl.multiple_of(kt * BLOCKS_PER_TILE, BLOCKS_PER_TILE), BLOCKS_PER_TILE), :]
        mask_rows = per_head(jnp.where(chosen > 0.5, 0.0, NEG)).astype(BF16)
        return jnp.dot(ks_ref[0, 0, kt], jnp.concatenate([qrT, mask_rows, pad_rows], axis=0),
                       preferred_element_type=F32)

    stage_a(0, sel_scores(0) + causal(0))

    def sel_body(j, carry):
        for slot, kt in ((1, 2 * j + 1), (0, 2 * j + 2)):
            @pl.when(kt < last)
            def _():
                stage_a(slot, sel_scores(kt))
                stage_b(1 - slot, vsT_ref[0, 0, kt - 1])
        return carry

    lax.fori_loop(0, last // 2, sel_body, 0)

    for slot in (0, 1):
        @pl.when((last > 0) & (last % 2 == slot))
        def _():
            stage_a(slot, sel_scores(last) + causal(last))
            stage_b(1 - slot, vsT_ref[0, 0, last - 1])
            stage_b(slot, vsT_ref[0, 0, last])

    @pl.when(last == 0)
    def _():
        stage_b(0, vsT_ref[0, 0, 0])

    o_scr[...] = o_scr[...] + gate_row(1) * acc_scr[...] * (1.0 / l_scr[...])

    reset()

    def win_scores(kt):
        kp = kt * KEY_TILE + rows
        bias = per_head(jnp.where((kp <= qpos) & (kp >= qpos - WINDOW), 0.0, NEG))
        return jnp.dot(kw_ref[0, 0, kt], qrT, preferred_element_type=F32) + bias

    @pl.when(last > 0)
    def _():
        stage_a(0, win_scores(last - 1))
        stage_a(1, win_scores(last))
        stage_b(0, vwT_ref[0, 0, last - 1])
        stage_b(1, vwT_ref[0, 0, last])

    @pl.when(last == 0)
    def _():
        stage_a(0, win_scores(0))
        stage_b(0, vwT_ref[0, 0, 0])

    o = o_scr[...] + gate_row(2) * acc_scr[...] * (1.0 / l_scr[...])
    cols = []
    for hp in range(HPG // 2):
        pair = jnp.concatenate([o[:, (2 * hp) * Q_BLOCK:(2 * hp + 1) * Q_BLOCK],
                                o[:, (2 * hp + 1) * Q_BLOCK:(2 * hp + 2) * Q_BLOCK]], axis=0)
        cols.append(pair.T)
    z = z_ref[0]
    o_ref[0] = (jnp.concatenate(cols, axis=1) * (z * _sigmoid(z))).astype(o_ref.dtype)


def nsa_prompt_attend(qT, qrT, gT, z, ck, cv, ks, vsT, kw, vwT):
    B, inner, T = qT.shape
    n_sel = T // SEL_BLOCK
    nc = ck.shape[1]
    n_cmp = -(-nc // LANES) * LANES
    grp = HPG * HEAD_DIM
    pad_c = ((0, 0), (0, n_cmp - nc), (0, 0), (0, 0))
    ck_p = jnp.pad(ck, pad_c).transpose(0, 2, 1, 3).astype(BF16)
    cvT = jnp.pad(cv, pad_c).transpose(0, 2, 3, 1).astype(BF16)
    cs = jnp.arange(n_cmp) * CMP_STRIDE
    ss = jnp.arange(n_sel) * SEL_BLOCK
    ovT = (jnp.clip(jnp.minimum(cs[None, :] + CMP_BLOCK, ss[:, None] + SEL_BLOCK)
                    - jnp.maximum(cs[None, :], ss[:, None]), 0, None).astype(F32) / CMP_BLOCK)
    ovT = jnp.where(jnp.arange(n_cmp)[None, :] < nc, ovT, 0.0).astype(BF16)

    n_tiles = T // KEY_TILE

    def key_tiles(t):
        return t.reshape(B, n_tiles, KEY_TILE, N_KV, HEAD_DIM).transpose(0, 3, 1, 2, 4)

    def with_block_columns(t):
        cols = (jnp.arange(KEY_TILE)[:, None] // SEL_BLOCK == jnp.arange(MASK_COLS)[None, :]).astype(t.dtype)
        return jnp.concatenate([t, jnp.broadcast_to(cols, t.shape[:-1] + (MASK_COLS,))], axis=-1)

    def val_tiles(t):
        t = t.reshape(B, N_KV, HEAD_DIM, n_tiles, KEY_TILE).transpose(0, 1, 3, 2, 4)
        ones = jnp.ones(t.shape[:3] + (ONES_ROWS, KEY_TILE), t.dtype)
        return jnp.concatenate([t, ones], axis=3)

    q_spec = pl.BlockSpec((1, grp, Q_BLOCK), lambda b, g, i: (b, g, i))
    row_spec = pl.BlockSpec((1, Q_BLOCK, grp), lambda b, g, i: (b, i, g))
    per_group = lambda shape: pl.BlockSpec((1, 1) + shape, lambda b, g, i: (b, g) + (0,) * len(shape))
    return pl.pallas_call(
        functools.partial(_nsa_prompt_kernel, n_cmp=n_cmp, n_sel=n_sel),
        grid=(B, N_KV, T // Q_BLOCK),
        in_specs=[q_spec, q_spec,
                  pl.BlockSpec((1, 3 * HPG, Q_BLOCK), lambda b, g, i: (b, g, i)),
                  row_spec,
                  per_group((n_cmp, HEAD_DIM)), per_group((HEAD_DIM, n_cmp)),
                  pl.BlockSpec((n_sel, n_cmp), lambda b, g, i: (0, 0)),
                  per_group((n_tiles, KEY_TILE, HEAD_DIM + MASK_COLS)),
                  per_group((n_tiles, HEAD_DIM + ONES_ROWS, KEY_TILE)),
                  per_group((n_tiles, KEY_TILE, HEAD_DIM)), per_group((n_tiles, HEAD_DIM + ONES_ROWS, KEY_TILE))],
        out_specs=row_spec,
        out_shape=jax.ShapeDtypeStruct((B, T, inner), BF16),
        scratch_shapes=[pltpu.VMEM((n_sel, Q_BLOCK), F32),
                        pltpu.VMEM((1, HPG * Q_BLOCK), F32),
                        pltpu.VMEM((1, HPG * Q_BLOCK), F32),
                        pltpu.VMEM((HEAD_DIM, HPG * Q_BLOCK), F32),
                        pltpu.VMEM((HEAD_DIM, HPG * Q_BLOCK), F32),
                        pltpu.VMEM((2, KEY_TILE, HPG * Q_BLOCK), F32),
                        pltpu.VMEM((2, 1, HPG * Q_BLOCK), F32)],
        compiler_params=pltpu.CompilerParams(
            dimension_semantics=("arbitrary", "arbitrary", "arbitrary"), vmem_limit_bytes=VMEM_LIMIT_BYTES),
        name="nsa_prompt_attend",
    )(qT, qrT, gT, z, ck_p, cvT, ovT, with_block_columns(key_tiles(ks)), val_tiles(vsT), key_tiles(kw), val_tiles(vwT))


def rms_norm(x, g):
    y = x * lax.rsqrt(jnp.mean(x * x, -1, keepdims=True) + NORM_EPS)
    return y * g


TOKEN_TILE = 256
SUBLANES = 8


def _bdot(a, b):
    return jnp.dot(a.astype(BF16), b.astype(BF16), preferred_element_type=F32)


def _split_dot(x, m, dims):
    hi = x.astype(BF16)
    lo = (x - hi.astype(F32)).astype(BF16)
    return (lax.dot_general(hi, m, (dims, ((), ())), preferred_element_type=F32)
            + lax.dot_general(lo, m, (dims, ((), ())), preferred_element_type=F32))


def _head_sum(x, seg):
    sums = _split_dot(x, seg, ((1,), (0,)))
    return _split_dot(sums, seg, ((1,), (1,)))


def _rms(x, g):
    return x * lax.rsqrt(jnp.mean(x * x, axis=-1, keepdims=True) + NORM_EPS) * g


def _normed_and_prev(h_ref, hprev_ref, shift_ref, g_ref):
    g = g_ref[...]
    xn = _rms(h_ref[0], g)
    prev_last = _rms(hprev_ref[0], g)[SUBLANES - 1:SUBLANES, :]
    first = jnp.where(pl.program_id(1) == 0, shift_ref[0], prev_last)
    row = lax.broadcasted_iota(jnp.int32, (xn.shape[0], 1), 0)
    return xn, jnp.where(row == 0, first, pltpu.roll(xn, 1, 0))


def _softplus(u):
    return jnp.maximum(u, 0.0) + jnp.log(1.0 + jnp.exp(-jnp.abs(u)))


def _sigmoid(u):
    return 1.0 / (1.0 + jnp.exp(-u))


def _rwkv_r_kernel(h_ref, hprev_ref, shift_ref, g_ref, mu_ref, w_ref, w0_ref, w1_ref, w2_ref, r_ref, ld_ref):
    xn, xp = _normed_and_prev(h_ref, hprev_ref, shift_ref, g_ref)
    dx = xp - xn
    r_ref[0] = _bdot(xn + dx * mu_ref[0:1, :], w_ref[...])
    lora = _bdot(jnp.tanh(_bdot(xn + dx * mu_ref[1:2, :], w1_ref[...])), w2_ref[...])
    w_log = -_softplus(-(w0_ref[...] + lora)) - 0.5
    ld_ref[0] = -jnp.exp(w_log)


def _rwkv_k_kernel(h_ref, hprev_ref, shift_ref, g_ref, mu_ref, w_ref, a0_ref, a1_ref, a2_ref, kk_w_ref, ka_ref,
                   seg_ref, k_ref, kk_ref, a_ref):
    xn, xp = _normed_and_prev(h_ref, hprev_ref, shift_ref, g_ref)
    dx = xp - xn
    k = _bdot(xn + dx * mu_ref[0:1, :], w_ref[...])
    a = _sigmoid(a0_ref[...] + _bdot(_bdot(xn + dx * mu_ref[1:2, :], a1_ref[...]), a2_ref[...]))
    kk = k * kk_w_ref[...]
    kk_ref[0] = kk * lax.rsqrt(jnp.maximum(_head_sum(kk * kk, seg_ref[...]), 1e-24))
    k_ref[0] = k * (1.0 + (a - 1.0) * ka_ref[...])
    a_ref[0] = a


def _rwkv_v_kernel(h_ref, hprev_ref, shift_ref, g_ref, mu_ref, w_ref, *rest, residual):
    xn, xp = _normed_and_prev(h_ref, hprev_ref, shift_ref, g_ref)
    mix = xn + (xp - xn) * mu_ref[0:1, :]
    v = _bdot(mix, w_ref[...])
    if residual:
        vfirst_ref, v0_ref, v1_ref, v2_ref, v_ref = rest
        v = v + (vfirst_ref[0] - v) * _sigmoid(v0_ref[...] + _bdot(_bdot(mix, v1_ref[...]), v2_ref[...]))
    else:
        (v_ref,) = rest
    v_ref[0] = v


def _rwkv_z_kernel(h_ref, hprev_ref, shift_ref, g_ref, mu_ref, w_ref, z_ref):
    xn, xp = _normed_and_prev(h_ref, hprev_ref, shift_ref, g_ref)
    z_ref[0] = _bdot(xn + (xp - xn) * mu_ref[0:1, :], w_ref[...])


def _rwkv_post_kernel(y_ref, r_ref, k_ref, v_ref, z_ref, h_ref, lnw_ref, lnb_ref, rk_ref, seg_ref, wout_ref, o_ref):
    seg = seg_ref[...]
    y = y_ref[0]
    d = y - _head_sum(y, seg) * (1.0 / RW_HEAD)
    var = _head_sum(d * d, seg) * (1.0 / RW_HEAD)
    yn = d * lax.rsqrt(var + LNX_EPS) * lnw_ref[...] + lnb_ref[...]
    yn = yn + _head_sum(r_ref[0] * k_ref[0] * rk_ref[...], seg) * v_ref[0]
    z = z_ref[0]
    o_ref[0] = h_ref[0] + _bdot(yn * (z * _sigmoid(z)), wout_ref[...])


def _head_indicator():
    return (jnp.arange(INNER)[:, None] // RW_HEAD == jnp.arange(LANES)[None, :]).astype(BF16)


def _token_call(kernel, operands, out_widths, name, out_dtype=F32):
    B, T = next(a.shape[:2] for kind, a in operands if kind == 'tile')
    tm = TOKEN_TILE if T % TOKEN_TILE == 0 else T
    per_tile = tm // SUBLANES
    specs = []
    for kind, a in operands:
        if kind == 'tile':
            specs.append(pl.BlockSpec((1, tm, a.shape[2]), lambda b, t: (b, t, 0)))
        elif kind == 'prev':
            specs.append(pl.BlockSpec((1, SUBLANES, a.shape[2]), lambda b, t: (b, jnp.maximum(t * per_tile - 1, 0), 0)))
        elif kind == 'batch':
            specs.append(pl.BlockSpec((1, 1, a.shape[2]), lambda b, t: (b, 0, 0)))
        else:
            specs.append(pl.BlockSpec(a.shape, lambda b, t, n=a.ndim: (0,) * n))
    return pl.pallas_call(
        kernel,
        grid=(B, T // tm),
        in_specs=specs,
        out_specs=[pl.BlockSpec((1, tm, w), lambda b, t: (b, t, 0)) for w in out_widths],
        out_shape=[jax.ShapeDtypeStruct((B, T, w), out_dtype) for w in out_widths],
        compiler_params=pltpu.CompilerParams(
            dimension_semantics=("arbitrary", "arbitrary"), vmem_limit_bytes=VMEM_LIMIT_BYTES),
        name=name,
    )(*[a for _, a in operands])


def rwkv_layer(h, shift_prev, s0, v_first, norm_g, mu, w_in, w_out, w0, w1, w2, a0, a1, a2, k_k, k_a, r_k,
               lnx_w, lnx_b, vres):
    B, T, D = h.shape
    row = lambda x: x.reshape(1, -1)
    bf = lambda x: x.astype(BF16)
    seg = _head_indicator()
    common = [('tile', h), ('prev', h), ('batch', shift_prev.reshape(B, 1, D)), ('const', row(norm_g))]
    r, logd = _token_call(
        _rwkv_r_kernel, common + [('const', mu[jnp.array([0, 4])]), ('const', bf(w_in[0])), ('const', row(w0)),
                                  ('const', bf(w1)), ('const', bf(w2))], [INNER, INNER], "rwkv_r")
    k, kk, a = _token_call(
        _rwkv_k_kernel, common + [('const', mu[jnp.array([1, 5])]), ('const', bf(w_in[1])), ('const', row(a0)),
                                  ('const', bf(a1)), ('const', bf(a2)), ('const', row(k_k)), ('const', row(k_a)),
                                  ('const', seg)], [INNER] * 3, "rwkv_k")
    if vres is None:
        (v,) = _token_call(functools.partial(_rwkv_v_kernel, residual=False),
                           common + [('const', mu[2:3]), ('const', bf(w_in[2]))], [INNER], "rwkv_v")
        v_first = v
    else:
        v0, v1, v2 = vres
        (v,) = _token_call(functools.partial(_rwkv_v_kernel, residual=True),
                           common + [('const', mu[2:3]), ('const', bf(w_in[2])), ('tile', v_first), ('const', row(v0)),
                                     ('const', bf(v1)), ('const', bf(v2))], [INNER], "rwkv_v")
    (z,) = _token_call(_rwkv_z_kernel, common + [('const', mu[3:4]), ('const', bf(w_in[3]))], [INNER], "rwkv_z")
    y, s_T = wkv_scan(r, logd, k, v, kk, a, s0, chunk=WKV_CHUNK if T % WKV_CHUNK == 0 else T,
                      t_block=WKV_T_BLOCK if T % WKV_T_BLOCK == 0 else T)
    (h_new,) = _token_call(
        _rwkv_post_kernel,
        [('tile', y), ('tile', r), ('tile', k), ('tile', v), ('tile', z), ('tile', h), ('const', row(lnx_w)),
         ('const', row(lnx_b)), ('const', r_k.reshape(1, INNER)), ('const', seg), ('const', bf(w_out))],
        [D], "rwkv_post")
    return h_new, v_first, s_T


ROT_HALF = ROT_DIM // 2


def _rope_rows(x, cos, sin_lo, sin_hi):
    out = []
    for c in range(x.shape[1] // LANES):
        xc = x[:, c * LANES:(c + 1) * LANES]
        out.append(xc * cos + pltpu.roll(xc, LANES - ROT_HALF, 1) * sin_lo + pltpu.roll(xc, ROT_HALF, 1) * sin_hi)
    return jnp.concatenate(out, axis=1)


def _rope_cols(x, cos, sin):
    n = x.shape[0] // HEAD_DIM
    x = x.reshape(n, HEAD_DIM, x.shape[1])
    x1, x2 = x[:, :ROT_HALF], x[:, ROT_HALF:ROT_DIM]
    y = jnp.concatenate([x1 * cos - x2 * sin, x1 * sin + x2 * cos, x[:, ROT_DIM:]], axis=1)
    return y.reshape(n * HEAD_DIM, y.shape[2])


def _nsa_rows_kernel(h_ref, g_ref, w_ref, cos_ref, slo_ref, shi_ref, cmp_ref, sel_ref, win_ref, z_ref, ks_ref, kw_ref):
    xn = _rms(h_ref[0], g_ref[...])
    p = _bdot(xn, w_ref[...])
    cos, slo, shi = cos_ref[0], slo_ref[0], shi_ref[0]
    cmp_ref[0] = p[:, :2 * KV_W]
    ks = _rope_rows(p[:, 2 * KV_W:3 * KV_W], cos, slo, shi)
    kw = _rope_rows(p[:, 4 * KV_W:5 * KV_W], cos, slo, shi)
    sel_ref[0] = jnp.concatenate([ks, p[:, 3 * KV_W:4 * KV_W]], axis=1)
    win_ref[0] = jnp.concatenate([kw, p[:, 5 * KV_W:6 * KV_W]], axis=1)
    z_ref[0] = p[:, 6 * KV_W:]
    ks_ref[0] = ks.astype(BF16)
    kw_ref[0] = kw.astype(BF16)


def _nsa_cols_kernel(h_ref, g_ref, wT_ref, cos_ref, sin_ref, qT_ref, qrT_ref, vsT_ref, vwT_ref, gT_ref):
    xn = _rms(h_ref[0], g_ref[...]).astype(BF16)
    pT = lax.dot_general(wT_ref[...], xn, (((1,), (1,)), ((), ())), preferred_element_type=F32)
    q = pT[:INNER]
    qT_ref[0] = q.astype(BF16)
    qrT_ref[0] = (_rope_cols(q, cos_ref[...], sin_ref[...]) * LOG2_E).astype(BF16)
    vsT_ref[0] = pT[INNER:INNER + KV_W].astype(BF16)
    vwT_ref[0] = pT[INNER + KV_W:INNER + 2 * KV_W].astype(BF16)
    gT_ref[0] = _sigmoid(pT[INNER + 2 * KV_W:])


def _rope_tables(pos):
    inv = ROPE_THETA ** (-jnp.arange(ROT_HALF, dtype=F32) / ROT_HALF)
    ang = pos.astype(F32)[:, None] * inv[None, :]
    cos, sin = jnp.cos(ang), jnp.sin(ang)
    lane = jnp.arange(LANES) % HEAD_DIM
    f = lane % ROT_HALF
    cos_l = jnp.where(lane[None, :] < ROT_DIM, cos[:, f], 1.0)
    slo_l = jnp.where(lane[None, :] < ROT_HALF, -sin[:, f], 0.0)
    shi_l = jnp.where((lane[None, :] >= ROT_HALF) & (lane[None, :] < ROT_DIM), sin[:, f], 0.0)
    return cos_l, slo_l, shi_l, cos.T, sin.T


def nsa_pre(h, norm_g, w_in, pos):
    B, T, D = h.shape
    tm = TOKEN_TILE if T % TOKEN_TILE == 0 else T
    cos_l, slo_l, shi_l, cosT, sinT = _rope_tables(pos)
    scale = HEAD_DIM ** -0.5
    g_row = norm_g.reshape(1, D)
    w_rows = jnp.concatenate([w_in[:, Q_END:KV_END], w_in[:, G_END:]], axis=1).astype(BF16)
    w_g = w_in[:, KV_END:G_END].reshape(D, N_KV, HPG, 3).transpose(0, 1, 3, 2).reshape(D, 3 * N_HEADS)
    w_cols = jnp.concatenate([w_in[:, :Q_END] * scale, w_in[:, Q_END + 3 * KV_W:Q_END + 4 * KV_W],
                              w_in[:, Q_END + 5 * KV_W:Q_END + 6 * KV_W], w_g], axis=1).T.astype(BF16)
    tile = lambda w: pl.BlockSpec((1, tm, w), lambda b, t: (b, t, 0))
    whole = lambda a: pl.BlockSpec(a.shape, lambda b, t, n=a.ndim: (0,) * n)
    tab = pl.BlockSpec((1, tm, LANES), lambda b, t: (0, t, 0))
    params = pltpu.CompilerParams(dimension_semantics=("arbitrary", "arbitrary"), vmem_limit_bytes=VMEM_LIMIT_BYTES)
    cmp_rows, sel_rows, win_rows, z, ks, kw = pl.pallas_call(
        _nsa_rows_kernel,
        grid=(B, T // tm),
        in_specs=[tile(D), whole(g_row), whole(w_rows), tab, tab, tab],
        out_specs=[tile(2 * KV_W), tile(2 * KV_W), tile(2 * KV_W), tile(INNER), tile(KV_W), tile(KV_W)],
        out_shape=[jax.ShapeDtypeStruct((B, T, 2 * KV_W), F32)] * 3 + [jax.ShapeDtypeStruct((B, T, INNER), F32)]
        + [jax.ShapeDtypeStruct((B, T, KV_W), BF16)] * 2,
        compiler_params=params, name="nsa_rows",
    )(h, g_row, w_rows, cos_l[None], slo_l[None], shi_l[None])
    colt = lambda r: pl.BlockSpec((1, r, tm), lambda b, t: (b, 0, t))
    tabT = pl.BlockSpec((ROT_HALF, tm), lambda b, t: (0, t))
    qT, qrT, vsT, vwT, gT = pl.pallas_call(
        _nsa_cols_kernel,
        grid=(B, T // tm),
        in_specs=[tile(D), whole(g_row), whole(w_cols), tabT, tabT],
        out_specs=[colt(INNER), colt(INNER), colt(KV_W), colt(KV_W), colt(3 * N_HEADS)],
        out_shape=[jax.ShapeDtypeStruct((B, INNER, T), BF16)] * 2 + [jax.ShapeDtypeStruct((B, KV_W, T), BF16)] * 2
        + [jax.ShapeDtypeStruct((B, 3 * N_HEADS, T), F32)],
        compiler_params=params, name="nsa_cols",
    )(h, g_row, w_cols, cosT, sinT)
    return cmp_rows, sel_rows, win_rows, z, ks, kw, qT, qrT, vsT, vwT, gT


PAGE = 128
SAMPLE_VMEM_LIMIT_BYTES = 56 * 1024 * 1024


def _gather_pages(pt_ref, b, pool_hbm, buf, sem, n_pages):
    rows = pool_hbm.shape[1]
    copies = [pltpu.make_async_copy(pool_hbm.at[pt_ref[b, p]], buf.at[pl.ds(p * rows, rows)], sem)
              for p in range(n_pages)]
    for cp in copies:
        cp.start()
    for cp in copies:
        cp.wait()


def _sample_compress_kernel(pt_ref, pool_hbm, w1_hbm, pe_ref, w2_ref, out_ref, buf, w1_vmem, sem, wsem, *, n_pages):
    b = pl.program_id(0)

    @pl.when(b == 0)
    def _():
        cp = pltpu.make_async_copy(w1_hbm, w1_vmem, wsem)
        cp.start()
        cp.wait()

    _gather_pages(pt_ref, b, pool_hbm, buf, sem, n_pages)
    n_chunks = n_pages * PAGE // CMP_STRIDE
    row = lax.broadcasted_iota(jnp.int32, (n_chunks, 1), 0)
    for kv in range(2):
        lanes = slice(kv * KV_W, (kv + 1) * KV_W)
        parts = []
        for m in range(CMP_RATIO):
            acc = jnp.zeros((n_chunks, w1_vmem.shape[3]), F32)
            for s in range(CMP_STRIDE):
                at = s * 2 * KV_W + kv * KV_W
                x = buf[:, at:at + KV_W] + pe_ref[m * CMP_STRIDE + s:m * CMP_STRIDE + s + 1, :]
                acc = acc + jnp.dot(x.astype(BF16), w1_vmem[kv, m * CMP_STRIDE + s], preferred_element_type=F32)
            parts.append(acc)
        pre = parts[0] + pltpu.roll(parts[1], n_chunks - 1, 0)
        hid = pre * _sigmoid(pre)
        ck = jnp.dot(hid.astype(BF16), w2_ref[kv], preferred_element_type=F32)
        out_ref[0, :, lanes] = jnp.where(row < n_chunks - 1, ck, 0.0)


def _block_diag(w):
    eye = jnp.eye(N_KV, dtype=w.dtype)
    out = eye[:, None, :, None] * w[..., None, :, None, :]
    return out.reshape(w.shape[:-2] + (N_KV * w.shape[-2], N_KV * w.shape[-1]))


def sample_compress(pool, page_table, pe, w1, w2):
    B, n_pages = page_table.shape
    n_chunks = n_pages * PAGE // CMP_STRIDE
    pool2 = pool.reshape(pool.shape[0], PAGE // CMP_STRIDE, CMP_STRIDE * 2 * KV_W)
    w1_bd = _block_diag(w1).astype(BF16)
    w2_bd = _block_diag(w2).astype(BF16)
    pe_t = jnp.tile(pe, (1, N_KV))
    return pl.pallas_call(
        functools.partial(_sample_compress_kernel, n_pages=n_pages),
        grid_spec=pltpu.PrefetchScalarGridSpec(
            num_scalar_prefetch=1,
            grid=(B,),
            in_specs=[pl.BlockSpec(memory_space=pl.ANY), pl.BlockSpec(memory_space=pl.ANY),
                      pl.BlockSpec(pe_t.shape, lambda b, pt: (0, 0)),
                      pl.BlockSpec(w2_bd.shape, lambda b, pt: (0, 0, 0))],
            out_specs=pl.BlockSpec((1, n_chunks, 2 * KV_W), lambda b, pt: (b, 0, 0)),
            scratch_shapes=[pltpu.VMEM((n_chunks, CMP_STRIDE * 2 * KV_W), F32),
                            pltpu.VMEM(w1_bd.shape, BF16),
                            pltpu.SemaphoreType.DMA(()), pltpu.SemaphoreType.DMA(())]),
        out_shape=jax.ShapeDtypeStruct((B, n_chunks, 2 * KV_W), F32),
        compiler_params=pltpu.CompilerParams(
            dimension_semantics=("arbitrary",), vmem_limit_bytes=SAMPLE_VMEM_LIMIT_BYTES),
        name="sample_compress",
    )(page_table, pool2, w1_bd, pe_t, w2_bd)


SAMPLE_TILE = 512


def _online_step(s, v, m_ref, l_ref, acc_ref):
    m_old = m_ref[...]
    m_new = jnp.maximum(m_old, jnp.max(s, axis=0, keepdims=True))
    alpha = jnp.exp2(m_old - m_new)
    p = jnp.exp2(s - m_new)
    l_ref[...] = alpha * l_ref[...] + jnp.sum(p, axis=0, keepdims=True)
    acc_ref[...] = alpha * acc_ref[...] + lax.dot_general(
        v.astype(BF16), p.astype(BF16), (((0,), (0,)), ((), ())), preferred_element_type=F32)
    m_ref[...] = m_new


def _split_dot_left(m, x):
    hi = x.astype(BF16)
    lo = (x - hi.astype(F32)).astype(BF16)
    return jnp.dot(m, hi, preferred_element_type=F32) + jnp.dot(m, lo, preferred_element_type=F32)


def _sample_attend_kernel(pt_ref, pool_hbm, ckv_ref, qraw_ref, qrot_ref, g_ref, selnew_ref, win_ref, winnew_ref,
                          ovT_ref, fold_ref, o_ref, buf, sel_scr, m_scr, l_scr, acc_scr, sem,
                          *, n_pages, n_cmp, n_sel, past, t_new):
    b = pl.program_id(0)
    _gather_pages(pt_ref, b, pool_hbm, buf, sem, n_pages)
    width = qraw_ref.shape[2]
    q_idx = lax.broadcasted_iota(jnp.int32, (1, width), 1) % t_new
    qpos = past + q_idx

    n_rows = ckv_ref.shape[1]
    n_iota = lax.broadcasted_iota(jnp.int32, (n_rows, width), 0)
    c_valid = (n_iota < n_cmp) & ((n_iota * CMP_STRIDE + (CMP_BLOCK - 1)) <= qpos)
    s = jnp.dot(ckv_ref[0, :, :KV_W].astype(BF16), qraw_ref[0], preferred_element_type=F32)
    s = jnp.where(c_valid, s, NEG)
    mx = jnp.max(s, axis=0, keepdims=True)
    e = jnp.where(c_valid, jnp.exp(s - mx), 0.0)
    l = jnp.sum(e, axis=0, keepdims=True)
    p = e * (1.0 / jnp.where(l > 0.0, l, 1.0))
    o_ref[0] = g_ref[0, 0:1, :] * lax.dot_general(
        ckv_ref[0, :, KV_W:].astype(BF16), p.astype(BF16), (((0,), (0,)), ((), ())), preferred_element_type=F32)

    fold = fold_ref[...]
    p_sum = _split_dot(p, fold, ((1,), (0,)))
    imp = _split_dot_left(ovT_ref[...], p_sum)
    j_iota = lax.broadcasted_iota(jnp.int32, imp.shape, 0)
    cur = (past + lax.broadcasted_iota(jnp.int32, (1, imp.shape[1]), 1) % t_new) // SEL_BLOCK
    forced = (j_iota == 0) | (j_iota == cur) | (j_iota == cur - 1)
    score = jnp.where(j_iota <= cur, imp + jnp.where(forced, FORCE_BONUS, 0.0), NEG)
    sel = jnp.zeros(imp.shape, F32)
    for _ in range(TOP_N):
        mx = jnp.max(score, axis=0, keepdims=True)
        first = jnp.min(jnp.where(score == mx, j_iota, n_sel), axis=0, keepdims=True)
        pick = j_iota == first
        sel = jnp.where(pick & (mx > NEG / 2), 1.0, sel)
        score = jnp.where(pick, PICKED, score)
    sel_scr[...] = lax.dot_general(sel.astype(BF16), fold, (((1,), (1,)), ((), ())), preferred_element_type=F32)

    def reset():
        m_scr[...] = jnp.full(m_scr.shape, NEG, F32)
        l_scr[...] = jnp.zeros(l_scr.shape, F32)
        acc_scr[...] = jnp.zeros(acc_scr.shape, F32)

    qrot = qrot_ref[0]
    new_rows = lax.broadcasted_iota(jnp.int32, (t_new, width), 0)
    causal_new = new_rows <= q_idx

    reset()
    blocks_per_tile = SAMPLE_TILE // SEL_BLOCK
    for kt in range(n_pages * PAGE // SAMPLE_TILE):
        chosen = sel_scr[kt * blocks_per_tile:(kt + 1) * blocks_per_tile, :]
        chosen = jnp.concatenate(
            [jnp.broadcast_to(chosen[i:i + 1, :], (SEL_BLOCK, width)) for i in range(blocks_per_tile)], axis=0)
        rows = buf[kt * SAMPLE_TILE:(kt + 1) * SAMPLE_TILE, :]
        s = jnp.dot(rows[:, :KV_W].astype(BF16), qrot, preferred_element_type=F32)
        _online_step(s + jnp.where(chosen > 0.5, 0.0, NEG), rows[:, KV_W:], m_scr, l_scr, acc_scr)
    last = past // SEL_BLOCK
    allowed = causal_new & (sel_scr[last:last + 1, :] > 0.5)
    s = jnp.dot(selnew_ref[0, :, :KV_W].astype(BF16), qrot, preferred_element_type=F32)
    _online_step(s + jnp.where(allowed, 0.0, NEG), selnew_ref[0, :, KV_W:], m_scr, l_scr, acc_scr)
    o_ref[0] = o_ref[0] + g_ref[0, 1:2, :] * acc_scr[...] * (1.0 / l_scr[...])

    reset()
    n_buf = win_ref.shape[1]
    w_pos = past - n_buf + lax.broadcasted_iota(jnp.int32, (n_buf, width), 0)
    allowed = (w_pos >= qpos - WINDOW) & (w_pos >= 0)
    s = jnp.dot(win_ref[0, :, :KV_W].astype(BF16), qrot, preferred_element_type=F32)
    _online_step(s + jnp.where(allowed, 0.0, NEG), win_ref[0, :, KV_W:], m_scr, l_scr, acc_scr)
    s = jnp.dot(winnew_ref[0, :, :KV_W].astype(BF16), qrot, preferred_element_type=F32)
    _online_step(s + jnp.where(causal_new, 0.0, NEG), winnew_ref[0, :, KV_W:], m_scr, l_scr, acc_scr)
    o_ref[0] = o_ref[0] + g_ref[0, 2:3, :] * acc_scr[...] * (1.0 / l_scr[...])


def sample_attend(pool, page_table, ckv, qT, qrT, gT, sel_new, win_buf, win_new):
    B, n_pages = page_table.shape
    t_new = sel_new.shape[1]
    past = n_pages * PAGE
    width = N_HEADS * t_new
    assert t_new < CMP_STRIDE and t_new <= SEL_BLOCK and past % SAMPLE_TILE == 0 and win_buf.shape[1] <= past
    assert t_new <= win_buf.shape[1] == min(WINDOW, past)
    n_cmp = past // CMP_STRIDE - CMP_RATIO + 1
    n_sel = -(-(past // SEL_BLOCK + 1) // SUBLANES) * SUBLANES

    def block_q(x):
        x = x.reshape(N_KV, HPG, HEAD_DIM, B, t_new).transpose(3, 0, 2, 1, 4)
        eye = jnp.eye(N_KV, dtype=x.dtype)
        x = x[:, :, :, None, :, :] * eye[None, :, None, :, None, None]
        return x.reshape(B, KV_W, width)

    gates = gT.reshape(N_KV, 3, HPG, B, t_new).transpose(3, 1, 0, 2, 4).reshape(B, 3, width)
    cs = jnp.arange(ckv.shape[1]) * CMP_STRIDE
    ss = jnp.arange(n_sel) * SEL_BLOCK
    ovT = (jnp.clip(jnp.minimum(cs[None, :] + CMP_BLOCK, ss[:, None] + SEL_BLOCK)
                    - jnp.maximum(cs[None, :], ss[:, None]), 0, None).astype(F32) / CMP_BLOCK)
    ovT = jnp.where(jnp.arange(ckv.shape[1])[None, :] < n_cmp, ovT, 0.0).astype(BF16)
    lane = jnp.arange(width)
    col = (lane // (HPG * t_new)) * t_new + lane % t_new
    fold = (col[:, None] == jnp.arange(LANES)[None, :]).astype(BF16)
    per_b = lambda a: pl.BlockSpec((1,) + a.shape[1:], lambda b, pt, n=a.ndim: (b,) + (0,) * (n - 1))
    whole = lambda a: pl.BlockSpec(a.shape, lambda b, pt, n=a.ndim: (0,) * n)
    pool2 = pool.reshape(pool.shape[0], PAGE, 2 * KV_W)
    qraw, qrot = block_q(qT), block_q(qrT)
    o_bd = pl.pallas_call(
        functools.partial(_sample_attend_kernel, n_pages=n_pages, n_cmp=n_cmp, n_sel=n_sel, past=past, t_new=t_new),
        grid_spec=pltpu.PrefetchScalarGridSpec(
            num_scalar_prefetch=1,
            grid=(B,),
            in_specs=[pl.BlockSpec(memory_space=pl.ANY), per_b(ckv), per_b(qraw), per_b(qrot), per_b(gates),
                      per_b(sel_new), per_b(win_buf), per_b(win_new), whole(ovT), whole(fold)],
            out_specs=pl.BlockSpec((1, KV_W, width), lambda b, pt: (b, 0, 0)),
            scratch_shapes=[pltpu.VMEM((past, 2 * KV_W), F32),
                            pltpu.VMEM((n_sel, width), F32),
                            pltpu.VMEM((1, width), F32), pltpu.VMEM((1, width), F32),
                            pltpu.VMEM((KV_W, width), F32),
                            pltpu.SemaphoreType.DMA(())]),
        out_shape=jax.ShapeDtypeStruct((B, KV_W, width), F32),
        compiler_params=pltpu.CompilerParams(dimension_semantics=("arbitrary",), vmem_limit_bytes=VMEM_LIMIT_BYTES),
        name="sample_attend",
    )(page_table, pool2, ckv, qraw, qrot, gates, sel_new, win_buf, win_new, ovT, fold)
    o = o_bd.reshape(B, N_KV, HEAD_DIM, N_KV, HPG, t_new)
    o = jnp.stack([o[:, g, :, g] for g in range(N_KV)], axis=1)
    return o.transpose(0, 4, 1, 3, 2).reshape(B, t_new, INNER)


def compress(k, v, pe, w1, w2):
    def phi(t, a, b):
        B, T = t.shape[:2]
        n_chunk = T // CMP_STRIDE
        nc = n_chunk - CMP_RATIO + 1
        chunks = t[:, :n_chunk * CMP_STRIDE].reshape(B, n_chunk, CMP_STRIDE, N_KV, HEAD_DIM)
        parts = []
        for m in range(CMP_RATIO):
            sl = slice(m * CMP_STRIDE, (m + 1) * CMP_STRIDE)
            part = jnp.einsum('bcsgd,sdh->bcgh', chunks + pe[None, None, sl, None, :], a[sl])
            parts.append(part[:, m:m + nc])
        h = jax.nn.silu(sum(parts))
        return jnp.einsum('bngh,hd->bngd', h, b)
    ck = phi(k, w1[0], w2[0])
    cv = phi(v, w1[1], w2[1])
    c_end = jnp.arange(ck.shape[1]) * CMP_STRIDE + CMP_BLOCK - 1
    return ck, cv, c_end


def _residual_matmul_kernel(x_ref, h_ref, w_ref, o_ref):
    o_ref[0] = h_ref[0] + jnp.dot(x_ref[0], w_ref[...], preferred_element_type=F32)


def nsa_prompt(h, norm_g, w_in, w_out, pe, cw1, cw2):
    B, T, _ = h.shape
    cmp_rows, sel_rows, win_rows, z, ks, kw, qT, qrT, vsT, vwT, gT = nsa_pre(h, norm_g, w_in, jnp.arange(T))
    heads = lambda t: t.reshape(B, T, N_KV, HEAD_DIM)
    ck, cv, _ = compress(heads(cmp_rows[..., :KV_W]), heads(cmp_rows[..., KV_W:]), pe, cw1, cw2)
    gated = nsa_prompt_attend(qT, qrT, gT, z, ck, cv, ks, vsT, kw, vwT)
    (h_new,) = _token_call(_residual_matmul_kernel, [('tile', gated), ('tile', h), ('const', w_out.astype(BF16))],
                           [D_MODEL], "nsa_out")
    n_keep = min(WINDOW, T)
    return h_new, cmp_rows, sel_rows, win_rows[:, -n_keep:]


def nsa_sample(h, norm_g, pool_cmp, pool_sel, win_buf, page_table, w_in, w_out, pe, cw1, cw2):
    B, T, D = h.shape
    past = page_table.shape[1] * pool_cmp.shape[1]
    pos = past + jnp.arange(B * T) % T
    cmp_rows, sel_rows, win_rows, z, _, _, qT, qrT, _, _, gT = nsa_pre(h.reshape(1, B * T, D), norm_g, w_in, pos)
    rows = lambda t: t.reshape(B, T, 2 * KV_W)
    ckv = sample_compress(pool_cmp, page_table, pe, cw1, cw2)
    n_buf = win_buf.shape[1]
    win_flat = win_buf.reshape(B, n_buf, 2 * KV_W)
    o = sample_attend(pool_sel, page_table, ckv, qT[0], qrT[0], gT[0], rows(sel_rows), win_flat, rows(win_rows))
    gated = (o * jax.nn.silu(z.reshape(B, T, INNER))).astype(BF16)
    (h_new,) = _token_call(_residual_matmul_kernel, [('tile', gated), ('tile', h), ('const', w_out.astype(BF16))],
                           [D_MODEL], "nsa_out")
    n_keep = min(WINDOW, n_buf + T)
    win_out = jnp.concatenate([win_flat, rows(win_rows)], axis=1)[:, -n_keep:]
    return h_new, rows(cmp_rows), rows(sel_rows), win_out


def kernel(x_prompt, x_sample, state_rwkv_wkv, state_rwkv_shift, cache_nsa_cmp, cache_nsa_sel, cache_nsa_win, page_table, norm_g, final_norm_g, rwkv_mu, rwkv_w_in, rwkv_w_out, rwkv_w0, rwkv_w1, rwkv_w2, rwkv_a0, rwkv_a1, rwkv_a2, rwkv_v0, rwkv_v1, rwkv_v2, rwkv_k_k, rwkv_k_a, rwkv_r_k, rwkv_lnx_w, rwkv_lnx_b, nsa_w_in, nsa_w_out, nsa_cmp_pe, nsa_cmp_w1, nsa_cmp_w2):

    def trunk(x, wkv0, shift0, sample):
        h = x
        v_first = None
        wkv, shift, cmp_rows, sel_rows, win_rows = [], [], [], [], []
        for layer in range(DEPTH):
            j = layer // N_MIXERS
            if layer % N_MIXERS == 0:
                vres = None if j == 0 else (rwkv_v0[j - 1], rwkv_v1[j - 1], rwkv_v2[j - 1])
                shift.append(rms_norm(h[:, -1], norm_g[layer]))
                h, v_first, s_T = rwkv_layer(
                    h, shift0[j], wkv0[j], v_first, norm_g[layer], rwkv_mu[j], rwkv_w_in[j], rwkv_w_out[j],
                    rwkv_w0[j], rwkv_w1[j], rwkv_w2[j], rwkv_a0[j], rwkv_a1[j], rwkv_a2[j],
                    rwkv_k_k[j], rwkv_k_a[j], rwkv_r_k[j], rwkv_lnx_w[j], rwkv_lnx_b[j], vres)
                wkv.append(s_T)
            else:
                if sample:
                    h, c, s, w = nsa_sample(h, norm_g[layer], cache_nsa_cmp[j], cache_nsa_sel[j], cache_nsa_win[j],
                                            page_table, nsa_w_in[j], nsa_w_out[j], nsa_cmp_pe[j],
                                            nsa_cmp_w1[j], nsa_cmp_w2[j])
                else:
                    h, c, s, w = nsa_prompt(h, norm_g[layer], nsa_w_in[j], nsa_w_out[j], nsa_cmp_pe[j],
                                            nsa_cmp_w1[j], nsa_cmp_w2[j])
                cmp_rows.append(c)
                sel_rows.append(s)
                win_rows.append(w)
        def stacked(parts):
            t = jnp.stack(parts)
            return t.reshape(t.shape[:3] + (2, N_KV, HEAD_DIM))

        return (rms_norm(h, final_norm_g), jnp.stack(wkv), jnp.stack(shift),
                stacked(cmp_rows), stacked(sel_rows), stacked(win_rows))

    zeros_wkv = jnp.zeros((rwkv_mu.shape[0], x_prompt.shape[0], RW_HEADS, RW_HEAD, RW_HEAD), x_prompt.dtype)
    zeros_shift = jnp.zeros((rwkv_mu.shape[0], x_prompt.shape[0], D_MODEL), x_prompt.dtype)
    y_prompt, wkv_p, shift_p, cmp_p, sel_p, win_p = trunk(x_prompt, zeros_wkv, zeros_shift, False)
    y_sample, wkv_s, shift_s, cmp_s, sel_s, win_s = trunk(x_sample, state_rwkv_wkv, state_rwkv_shift, True)
    return (y_prompt, y_sample, wkv_p, wkv_s, shift_p, shift_s, cmp_p, cmp_s, sel_p, sel_s, win_p, win_s)
```

```python
import functools

import jax
import jax.numpy as jnp
from jax import lax
from jax.experimental import pallas as pl
from jax.experimental.pallas import tpu as pltpu

D_MODEL = 1024
DEPTH = 4
N_MIXERS = 2
INNER = 2 * D_MODEL
NORM_EPS = 1e-6
RW_HEAD = 64
RW_HEADS = INNER // RW_HEAD
LNX_EPS = 64e-5
HEAD_DIM = 64
N_HEADS = INNER // HEAD_DIM
N_KV = 4
HPG = N_HEADS // N_KV
KV_W = N_KV * HEAD_DIM
ROT_DIM = HEAD_DIM // 4
ROPE_THETA = 500000.0
CMP_BLOCK = 32
CMP_STRIDE = 16
CMP_RATIO = CMP_BLOCK // CMP_STRIDE
SEL_BLOCK = 64
TOP_N = 16
WINDOW = 512
Q_BLOCK = 128
NEG = -1e30
FORCE_BONUS = 1e4
Q_END = INNER
KV_END = Q_END + 6 * KV_W
G_END = KV_END + 3 * N_HEADS
NSA_IN = G_END + INNER

F32 = jnp.float32
BF16 = jnp.bfloat16
VMEM_LIMIT_BYTES = 48 * 1024 * 1024


LANES = 128
WKV_CHUNK = 64
WKV_T_BLOCK = 256
WKV_PAIRS = 16


def _dot(a, b):
    return jnp.dot(a.astype(BF16), b.astype(BF16), preferred_element_type=F32)


def _dot_nt(a, b):
    return lax.dot_general(a.astype(BF16), b.astype(BF16), (((1,), (1,)), ((), ())), preferred_element_type=F32)


def _dot_tn(a, b):
    return lax.dot_general(a.astype(BF16), b.astype(BF16), (((0,), (0,)), ((), ())), preferred_element_type=F32)


def _wkv_kernel(r_ref, ld_ref, k_ref, v_ref, kk_ref, a_ref, s0_ref, y_ref, st_ref, s_scr, *, chunk, n_chunks, n_pairs):
    C = chunk
    R = 2 * C
    tb = pl.program_id(2)

    @pl.when(tb == 0)
    def _():
        z = jnp.zeros((RW_HEAD, RW_HEAD), F32)
        for g in range(n_pairs):
            s_scr[g] = jnp.concatenate(
                [jnp.concatenate([s0_ref[0, 2 * g], z], axis=1),
                 jnp.concatenate([z, s0_ref[0, 2 * g + 1]], axis=1)], axis=0)

    lane = lax.broadcasted_iota(jnp.int32, (1, LANES), 1)
    head_a = lane < RW_HEAD
    row = lax.broadcasted_iota(jnp.int32, (R, R), 0)
    col = lax.broadcasted_iota(jnp.int32, (R, R), 1)
    same = (row // C) == (col // C)
    strict = same & ((col % C) < (row % C))
    incl = same & ((col % C) <= (row % C))
    eye = (row == col).astype(F32)
    tr = lax.broadcasted_iota(jnp.int32, (C, C), 0)
    tc = lax.broadcasted_iota(jnp.int32, (C, C), 1)
    tri = (tc <= tr).astype(BF16)

    def stack(z):
        return jnp.concatenate([jnp.where(head_a, z, 0.0), jnp.where(head_a, 0.0, z)], axis=0)

    def body(ci, carry):
        sl = pl.ds(pl.multiple_of(ci * C, C), C)
        G = range(n_pairs)
        ld = [ld_ref[0, sl, g * LANES:(g + 1) * LANES] for g in G]
        ld_hi = [x.astype(BF16) for x in ld]
        ld_lo = [(x - h.astype(F32)).astype(BF16) for x, h in zip(ld, ld_hi)]
        cum = [jnp.dot(tri, h, preferred_element_type=F32) + jnp.dot(tri, l, preferred_element_type=F32)
               for h, l in zip(ld_hi, ld_lo)]
        e_neg = [jnp.exp(-c) for c in cum]
        kk = [kk_ref[0, sl, g * LANES:(g + 1) * LANES] for g in G]
        a_s = [stack(-kk[g] * jnp.exp(cum[g] - ld[g])) for g in G]
        r_s = [stack(r_ref[0, sl, g * LANES:(g + 1) * LANES] * jnp.exp(cum[g])) for g in G]
        b_s = [stack(kk[g] * a_ref[0, sl, g * LANES:(g + 1) * LANES] * e_neg[g]) for g in G]
        k_s = [stack(k_ref[0, sl, g * LANES:(g + 1) * LANES] * e_neg[g]) for g in G]
        v_s = [stack(v_ref[0, sl, g * LANES:(g + 1) * LANES]) for g in G]
        s2 = [s_scr[g] for g in G]
        ab = [jnp.where(strict, _dot_nt(a_s[g], b_s[g]), 0.0) for g in G]
        ak = [jnp.where(strict, _dot_nt(a_s[g], k_s[g]), 0.0) for g in G]
        rb = [jnp.where(incl, _dot_nt(r_s[g], b_s[g]), 0.0) for g in G]
        rk = [jnp.where(incl, _dot_nt(r_s[g], k_s[g]), 0.0) for g in G]
        rhs = [_dot_nt(a_s[g], s2[g]) + _dot(ak[g], v_s[g]) for g in G]
        y0 = [_dot_nt(r_s[g], s2[g]) + _dot(rk[g], v_s[g]) for g in G]
        tm = [eye + ab[g] for g in G]
        p = ab
        n = 2
        while n < C:
            p = [_dot(p[g], p[g]) for g in G]
            tm = [tm[g] + _dot(tm[g], p[g]) for g in G]
            n *= 2
        u_s = [_dot(tm[g], rhs[g]) for g in G]
        y_s = [y0[g] + _dot(rb[g], u_s[g]) for g in G]
        for g in G:
            y_ref[0, sl, g * LANES:(g + 1) * LANES] = y_s[g][:C] + y_s[g][C:]
            gamma = jnp.exp(cum[g][C - 1:C, :])
            s_scr[g] = (s2[g] + _dot_tn(u_s[g], b_s[g]) + _dot_tn(v_s[g], k_s[g])) * gamma
        return carry

    lax.fori_loop(0, n_chunks, body, 0)

    @pl.when(tb == pl.num_programs(2) - 1)
    def _():
        for g in range(n_pairs):
            st_ref[0, 2 * g] = s_scr[g, :RW_HEAD, :RW_HEAD]
            st_ref[0, 2 * g + 1] = s_scr[g, RW_HEAD:, RW_HEAD:]


def wkv_scan(r, logd, k, v, kk, a, s0, *, chunk, t_block):
    B, T, inner = r.shape
    width = WKV_PAIRS * LANES
    assert T % t_block == 0 and t_block % chunk == 0 and inner % width == 0
    seq_spec = pl.BlockSpec((1, t_block, width), lambda b, p, t: (b, t, p))
    st_spec = pl.BlockSpec((1, 2 * WKV_PAIRS, RW_HEAD, RW_HEAD), lambda b, p, t: (b, p, 0, 0))
    return pl.pallas_call(
        functools.partial(_wkv_kernel, chunk=chunk, n_chunks=t_block // chunk, n_pairs=WKV_PAIRS),
        grid=(B, inner // width, T // t_block),
        in_specs=[seq_spec] * 6 + [st_spec],
        out_specs=[seq_spec, st_spec],
        out_shape=[jax.ShapeDtypeStruct((B, T, inner), F32), jax.ShapeDtypeStruct(s0.shape, F32)],
        scratch_shapes=[pltpu.VMEM((WKV_PAIRS, LANES, LANES), F32)],
        compiler_params=pltpu.CompilerParams(
            dimension_semantics=("arbitrary", "arbitrary", "arbitrary"), vmem_limit_bytes=VMEM_LIMIT_BYTES),
        name="wkv_scan",
    )(r, logd, k, v, kk, a, s0)


KEY_TILE = 512
assert WINDOW <= KEY_TILE and KEY_TILE % Q_BLOCK == 0
BLOCKS_PER_TILE = KEY_TILE // SEL_BLOCK
MASK_COLS = 16
ONES_ROWS = 16
LOG2_E = 1.4426950408889634
PICKED = -3e38


def _nsa_prompt_kernel(qT_ref, qrT_ref, gT_ref, z_ref, ck_ref, cvT_ref, ovT_ref, ks_ref, vsT_ref, kw_ref, vwT_ref,
                       o_ref, sel_scr, m_scr, l_scr, acc_scr, o_scr, s_scr, cm_scr, *, n_cmp, n_sel):
    i = pl.program_id(2)
    q0 = i * Q_BLOCK
    qpos = q0 + lax.broadcasted_iota(jnp.int32, (1, Q_BLOCK), 1)

    def heads_on_lanes(ref):
        return jnp.concatenate([ref[0, h * HEAD_DIM:(h + 1) * HEAD_DIM, :] for h in range(HPG)], axis=1)

    def gate_row(br):
        return jnp.concatenate([gT_ref[0, br * HPG + h:br * HPG + h + 1, :] for h in range(HPG)], axis=1)

    def per_head(x):
        return jnp.concatenate([x] * HPG, axis=1)

    n_iota = lax.broadcasted_iota(jnp.int32, (n_cmp, Q_BLOCK), 0)
    c_valid = per_head((n_iota * CMP_STRIDE + (CMP_BLOCK - 1)) <= qpos)
    s = jnp.dot(ck_ref[0, 0], heads_on_lanes(qT_ref), preferred_element_type=F32)
    s = jnp.where(c_valid, s, NEG)
    mx = jnp.max(s, axis=0, keepdims=True)
    e = jnp.where(c_valid, jnp.exp(s - mx), 0.0)
    l = jnp.sum(e, axis=0, keepdims=True)
    p = e * (1.0 / jnp.where(l > 0.0, l, 1.0))
    o_scr[...] = gate_row(0) * jnp.dot(cvT_ref[0, 0], p.astype(BF16), preferred_element_type=F32)
    p_sum = p[:, :Q_BLOCK]
    for h in range(1, HPG):
        p_sum = p_sum + p[:, h * Q_BLOCK:(h + 1) * Q_BLOCK]

    ps_hi = p_sum.astype(BF16)
    ps_lo = (p_sum - ps_hi.astype(F32)).astype(BF16)
    ovT = ovT_ref[...]
    imp = jnp.dot(ovT, ps_hi, preferred_element_type=F32) + jnp.dot(ovT, ps_lo, preferred_element_type=F32)
    j_iota = lax.broadcasted_iota(jnp.int32, (n_sel, Q_BLOCK), 0)
    cur = qpos // SEL_BLOCK
    forced = (j_iota == 0) | (j_iota == cur) | (j_iota == cur - 1)
    score = jnp.where(j_iota <= cur, imp + jnp.where(forced, FORCE_BONUS, 0.0), NEG)
    sel = jnp.zeros((n_sel, Q_BLOCK), F32)
    for _ in range(TOP_N):
        mx = jnp.max(score, axis=0, keepdims=True)
        first = jnp.min(jnp.where(score == mx, j_iota, n_sel), axis=0, keepdims=True)
        pick = j_iota == first
        sel = jnp.where(pick & (mx > NEG / 2), 1.0, sel)
        score = jnp.where(pick, PICKED, score)
    sel_scr[...] = sel

    def reset():
        m_scr[...] = jnp.full(m_scr.shape, NEG, F32)
        l_scr[...] = jnp.zeros(l_scr.shape, F32)
        acc_scr[...] = jnp.zeros(acc_scr.shape, F32)

    qrT = heads_on_lanes(qrT_ref)
    rows = lax.broadcasted_iota(jnp.int32, (KEY_TILE, Q_BLOCK), 0)
    last = (q0 + Q_BLOCK - 1) // KEY_TILE

    def stage_a(slot, s):
        s_scr[slot] = s
        cm_scr[slot] = jnp.max(s, axis=0, keepdims=True)

    def stage_b(slot, vT_tile):
        m_old = m_scr[...]
        m_new = jnp.maximum(m_old, cm_scr[slot])
        alpha = jnp.exp2(m_old - m_new)
        pv = jnp.dot(vT_tile, jnp.exp2(s_scr[slot] - m_new).astype(BF16), preferred_element_type=F32)
        l_scr[...] = alpha * l_scr[...] + pv[HEAD_DIM:HEAD_DIM + 1, :]
        acc_scr[...] = alpha * acc_scr[...] + pv[:HEAD_DIM, :]
        m_scr[...] = m_new

    def causal(kt):
        return per_head(jnp.where((kt * KEY_TILE + rows) <= qpos, 0.0, NEG))

    reset()
    pad_rows = jnp.zeros((MASK_COLS - BLOCKS_PER_TILE, HPG * Q_BLOCK), BF16)

    def sel_scores(kt):
        chosen = sel_scr[pl.ds(pl.multiple_of(kt * BLOCKS_PER_TILE, BLOCKS_PER_TILE), BLOCKS_PER_TILE), :]
        mask_rows = per_head(jnp.where(chosen > 0.5, 0.0, NEG)).astype(BF16)
        return jnp.dot(ks_ref[0, 0, kt], jnp.concatenate([qrT, mask_rows, pad_rows], axis=0),
                       preferred_element_type=F32)

    stage_a(0, sel_scores(0) + causal(0))

    def sel_body(j, carry):
        for slot, kt in ((1, 2 * j + 1), (0, 2 * j + 2)):
            @pl.when(kt < last)
            def _():
                stage_a(slot, sel_scores(kt))
                stage_b(1 - slot, vsT_ref[0, 0, kt - 1])
        return carry

    lax.fori_loop(0, last // 2, sel_body, 0)

    for slot in (0, 1):
        @pl.when((last > 0) & (last % 2 == slot))
        def _():
            stage_a(slot, sel_scores(last) + causal(last))
            stage_b(1 - slot, vsT_ref[0, 0, last - 1])
            stage_b(slot, vsT_ref[0, 0, last])

    @pl.when(last == 0)
    def _():
        stage_b(0, vsT_ref[0, 0, 0])

    o_scr[...] = o_scr[...] + gate_row(1) * acc_scr[...] * (1.0 / l_scr[...])

    reset()

    def win_scores(kt):
        kp = kt * KEY_TILE + rows
        bias = per_head(jnp.where((kp <= qpos) & (kp >= qpos - WINDOW), 0.0, NEG))
        return jnp.dot(kw_ref[0, 0, kt], qrT, preferred_element_type=F32) + bias

    @pl.when(last > 0)
    def _():
        stage_a(0, win_scores(last - 1))
        stage_a(1, win_scores(last))
        stage_b(0, vwT_ref[0, 0, last - 1])
        stage_b(1, vwT_ref[0, 0, last])

    @pl.when(last == 0)
    def _():
        stage_a(0, win_scores(0))
        stage_b(0, vwT_ref[0, 0, 0])

    o = o_scr[...] + gate_row(2) * acc_scr[...] * (1.0 / l_scr[...])
    cols = []
    for hp in range(HPG // 2):
        pair = jnp.concatenate([o[:, (2 * hp) * Q_BLOCK:(2 * hp + 1) * Q_BLOCK],
                                o[:, (2 * hp + 1) * Q_BLOCK:(2 * hp + 2) * Q_BLOCK]], axis=0)
        cols.append(pair.T)
    z = z_ref[0]
    o_ref[0] = (jnp.concatenate(cols, axis=1) * (z * _sigmoid(z))).astype(o_ref.dtype)


def nsa_prompt_attend(qT, qrT, gT, z, ck, cv, ks, vsT, kw, vwT):
    B, inner, T = qT.shape
    n_sel = T // SEL_BLOCK
    nc = ck.shape[1]
    n_cmp = -(-nc // LANES) * LANES
    grp = HPG * HEAD_DIM
    pad_c = ((0, 0), (0, n_cmp - nc), (0, 0), (0, 0))
    ck_p = jnp.pad(ck, pad_c).transpose(0, 2, 1, 3).astype(BF16)
    cvT = jnp.pad(cv, pad_c).transpose(0, 2, 3, 1).astype(BF16)
    cs = jnp.arange(n_cmp) * CMP_STRIDE
    ss = jnp.arange(n_sel) * SEL_BLOCK
    ovT = (jnp.clip(jnp.minimum(cs[None, :] + CMP_BLOCK, ss[:, None] + SEL_BLOCK)
                    - jnp.maximum(cs[None, :], ss[:, None]), 0, None).astype(F32) / CMP_BLOCK)
    ovT = jnp.where(jnp.arange(n_cmp)[None, :] < nc, ovT, 0.0).astype(BF16)

    n_tiles = T // KEY_TILE

    def key_tiles(t):
        return t.reshape(B, n_tiles, KEY_TILE, N_KV, HEAD_DIM).transpose(0, 3, 1, 2, 4)

    def with_block_columns(t):
        cols = (jnp.arange(KEY_TILE)[:, None] // SEL_BLOCK == jnp.arange(MASK_COLS)[None, :]).astype(t.dtype)
        return jnp.concatenate([t, jnp.broadcast_to(cols, t.shape[:-1] + (MASK_COLS,))], axis=-1)

    def val_tiles(t):
        t = t.reshape(B, N_KV, HEAD_DIM, n_tiles, KEY_TILE).transpose(0, 1, 3, 2, 4)
        ones = jnp.ones(t.shape[:3] + (ONES_ROWS, KEY_TILE), t.dtype)
        return jnp.concatenate([t, ones], axis=3)

    q_spec = pl.BlockSpec((1, grp, Q_BLOCK), lambda b, g, i: (b, g, i))
    row_spec = pl.BlockSpec((1, Q_BLOCK, grp), lambda b, g, i: (b, i, g))
    per_group = lambda shape: pl.BlockSpec((1, 1) + shape, lambda b, g, i: (b, g) + (0,) * len(shape))
    return pl.pallas_call(
        functools.partial(_nsa_prompt_kernel, n_cmp=n_cmp, n_sel=n_sel),
        grid=(B, N_KV, T // Q_BLOCK),
        in_specs=[q_spec, q_spec,
                  pl.BlockSpec((1, 3 * HPG, Q_BLOCK), lambda b, g, i: (b, g, i)),
                  row_spec,
                  per_group((n_cmp, HEAD_DIM)), per_group((HEAD_DIM, n_cmp)),
                  pl.BlockSpec((n_sel, n_cmp), lambda b, g, i: (0, 0)),
                  per_group((n_tiles, KEY_TILE, HEAD_DIM + MASK_COLS)),
                  per_group((n_tiles, HEAD_DIM + ONES_ROWS, KEY_TILE)),
                  per_group((n_tiles, KEY_TILE, HEAD_DIM)), per_group((n_tiles, HEAD_DIM + ONES_ROWS, KEY_TILE))],
        out_specs=row_spec,
        out_shape=jax.ShapeDtypeStruct((B, T, inner), BF16),
        scratch_shapes=[pltpu.VMEM((n_sel, Q_BLOCK), F32),
                        pltpu.VMEM((1, HPG * Q_BLOCK), F32),
                        pltpu.VMEM((1, HPG * Q_BLOCK), F32),
                        pltpu.VMEM((HEAD_DIM, HPG * Q_BLOCK), F32),
                        pltpu.VMEM((HEAD_DIM, HPG * Q_BLOCK), F32),
                        pltpu.VMEM((2, KEY_TILE, HPG * Q_BLOCK), F32),
                        pltpu.VMEM((2, 1, HPG * Q_BLOCK), F32)],
        compiler_params=pltpu.CompilerParams(
            dimension_semantics=("arbitrary", "arbitrary", "arbitrary"), vmem_limit_bytes=VMEM_LIMIT_BYTES),
        name="nsa_prompt_attend",
    )(qT, qrT, gT, z, ck_p, cvT, ovT, with_block_columns(key_tiles(ks)), val_tiles(vsT), key_tiles(kw), val_tiles(vwT))


def rms_norm(x, g):
    y = x * lax.rsqrt(jnp.mean(x * x, -1, keepdims=True) + NORM_EPS)
    return y * g


TOKEN_TILE = 256
SUBLANES = 8


def _bdot(a, b):
    return jnp.dot(a.astype(BF16), b.astype(BF16), preferred_element_type=F32)


def _split_dot(x, m, dims):
    hi = x.astype(BF16)
    lo = (x - hi.astype(F32)).astype(BF16)
    return (lax.dot_general(hi, m, (dims, ((), ())), preferred_element_type=F32)
            + lax.dot_general(lo, m, (dims, ((), ())), preferred_element_type=F32))


def _head_sum(x, seg):
    sums = _split_dot(x, seg, ((1,), (0,)))
    return _split_dot(sums, seg, ((1,), (1,)))


def _rms(x, g):
    return x * lax.rsqrt(jnp.mean(x * x, axis=-1, keepdims=True) + NORM_EPS) * g


def _normed_and_prev(h_ref, hprev_ref, shift_ref, g_ref):
    g = g_ref[...]
    xn = _rms(h_ref[0], g)
    prev_last = _rms(hprev_ref[0], g)[SUBLANES - 1:SUBLANES, :]
    first = jnp.where(pl.program_id(1) == 0, shift_ref[0], prev_last)
    row = lax.broadcasted_iota(jnp.int32, (xn.shape[0], 1), 0)
    return xn, jnp.where(row == 0, first, pltpu.roll(xn, 1, 0))


def _softplus(u):
    return jnp.maximum(u, 0.0) + jnp.log(1.0 + jnp.exp(-jnp.abs(u)))


def _sigmoid(u):
    return 1.0 / (1.0 + jnp.exp(-u))


def _rwkv_r_kernel(h_ref, hprev_ref, shift_ref, g_ref, mu_ref, w_ref, w0_ref, w1_ref, w2_ref, r_ref, ld_ref):
    xn, xp = _normed_and_prev(h_ref, hprev_ref, shift_ref, g_ref)
    dx = xp - xn
    r_ref[0] = _bdot(xn + dx * mu_ref[0:1, :], w_ref[...])
    lora = _bdot(jnp.tanh(_bdot(xn + dx * mu_ref[1:2, :], w1_ref[...])), w2_ref[...])
    w_log = -_softplus(-(w0_ref[...] + lora)) - 0.5
    ld_ref[0] = -jnp.exp(w_log)


def _rwkv_k_kernel(h_ref, hprev_ref, shift_ref, g_ref, mu_ref, w_ref, a0_ref, a1_ref, a2_ref, kk_w_ref, ka_ref,
                   seg_ref, k_ref, kk_ref, a_ref):
    xn, xp = _normed_and_prev(h_ref, hprev_ref, shift_ref, g_ref)
    dx = xp - xn
    k = _bdot(xn + dx * mu_ref[0:1, :], w_ref[...])
    a = _sigmoid(a0_ref[...] + _bdot(_bdot(xn + dx * mu_ref[1:2, :], a1_ref[...]), a2_ref[...]))
    kk = k * kk_w_ref[...]
    kk_ref[0] = kk * lax.rsqrt(jnp.maximum(_head_sum(kk * kk, seg_ref[...]), 1e-24))
    k_ref[0] = k * (1.0 + (a - 1.0) * ka_ref[...])
    a_ref[0] = a


def _rwkv_v_kernel(h_ref, hprev_ref, shift_ref, g_ref, mu_ref, w_ref, *rest, residual):
    xn, xp = _normed_and_prev(h_ref, hprev_ref, shift_ref, g_ref)
    mix = xn + (xp - xn) * mu_ref[0:1, :]
    v = _bdot(mix, w_ref[...])
    if residual:
        vfirst_ref, v0_ref, v1_ref, v2_ref, v_ref = rest
        v = v + (vfirst_ref[0] - v) * _sigmoid(v0_ref[...] + _bdot(_bdot(mix, v1_ref[...]), v2_ref[...]))
    else:
        (v_ref,) = rest
    v_ref[0] = v


def _rwkv_z_kernel(h_ref, hprev_ref, shift_ref, g_ref, mu_ref, w_ref, z_ref):
    xn, xp = _normed_and_prev(h_ref, hprev_ref, shift_ref, g_ref)
    z_ref[0] = _bdot(xn + (xp - xn) * mu_ref[0:1, :], w_ref[...])


def _rwkv_post_kernel(y_ref, r_ref, k_ref, v_ref, z_ref, h_ref, lnw_ref, lnb_ref, rk_ref, seg_ref, wout_ref, o_ref):
    seg = seg_ref[...]
    y = y_ref[0]
    d = y - _head_sum(y, seg) * (1.0 / RW_HEAD)
    var = _head_sum(d * d, seg) * (1.0 / RW_HEAD)
    yn = d * lax.rsqrt(var + LNX_EPS) * lnw_ref[...] + lnb_ref[...]
    yn = yn + _head_sum(r_ref[0] * k_ref[0] * rk_ref[...], seg) * v_ref[0]
    z = z_ref[0]
    o_ref[0] = h_ref[0] + _bdot(yn * (z * _sigmoid(z)), wout_ref[...])


def _head_indicator():
    return (jnp.arange(INNER)[:, None] // RW_HEAD == jnp.arange(LANES)[None, :]).astype(BF16)


def _token_call(kernel, operands, out_widths, name, out_dtype=F32):
    B, T = next(a.shape[:2] for kind, a in operands if kind == 'tile')
    tm = TOKEN_TILE if T % TOKEN_TILE == 0 else T
    per_tile = tm // SUBLANES
    specs = []
    for kind, a in operands:
        if kind == 'tile':
            specs.append(pl.BlockSpec((1, tm, a.shape[2]), lambda b, t: (b, t, 0)))
        elif kind == 'prev':
            specs.append(pl.BlockSpec((1, SUBLANES, a.shape[2]), lambda b, t: (b, jnp.maximum(t * per_tile - 1, 0), 0)))
        elif kind == 'batch':
            specs.append(pl.BlockSpec((1, 1, a.shape[2]), lambda b, t: (b, 0, 0)))
        else:
            specs.append(pl.BlockSpec(a.shape, lambda b, t, n=a.ndim: (0,) * n))
    return pl.pallas_call(
        kernel,
        grid=(B, T // tm),
        in_specs=specs,
        out_specs=[pl.BlockSpec((1, tm, w), lambda b, t: (b, t, 0)) for w in out_widths],
        out_shape=[jax.ShapeDtypeStruct((B, T, w), out_dtype) for w in out_widths],
        compiler_params=pltpu.CompilerParams(
            dimension_semantics=("arbitrary", "arbitrary"), vmem_limit_bytes=VMEM_LIMIT_BYTES),
        name=name,
    )(*[a for _, a in operands])


def rwkv_layer(h, shift_prev, s0, v_first, norm_g, mu, w_in, w_out, w0, w1, w2, a0, a1, a2, k_k, k_a, r_k,
               lnx_w, lnx_b, vres):
    B, T, D = h.shape
    row = lambda x: x.reshape(1, -1)
    bf = lambda x: x.astype(BF16)
    seg = _head_indicator()
    common = [('tile', h), ('prev', h), ('batch', shift_prev.reshape(B, 1, D)), ('const', row(norm_g))]
    r, logd = _token_call(
        _rwkv_r_kernel, common + [('const', mu[jnp.array([0, 4])]), ('const', bf(w_in[0])), ('const', row(w0)),
                                  ('const', bf(w1)), ('const', bf(w2))], [INNER, INNER], "rwkv_r")
    k, kk, a = _token_call(
        _rwkv_k_kernel, common + [('const', mu[jnp.array([1, 5])]), ('const', bf(w_in[1])), ('const', row(a0)),
                                  ('const', bf(a1)), ('const', bf(a2)), ('const', row(k_k)), ('const', row(k_a)),
                                  ('const', seg)], [INNER] * 3, "rwkv_k")
    if vres is None:
        (v,) = _token_call(functools.partial(_rwkv_v_kernel, residual=False),
                           common + [('const', mu[2:3]), ('const', bf(w_in[2]))], [INNER], "rwkv_v")
        v_first = v
    else:
        v0, v1, v2 = vres
        (v,) = _token_call(functools.partial(_rwkv_v_kernel, residual=True),
                           common + [('const', mu[2:3]), ('const', bf(w_in[2])), ('tile', v_first), ('const', row(v0)),
                                     ('const', bf(v1)), ('const', bf(v2))], [INNER], "rwkv_v")
    (z,) = _token_call(_rwkv_z_kernel, common + [('const', mu[3:4]), ('const', bf(w_in[3]))], [INNER], "rwkv_z")
    y, s_T = wkv_scan(r, logd, k, v, kk, a, s0, chunk=WKV_CHUNK if T % WKV_CHUNK == 0 else T,
                      t_block=WKV_T_BLOCK if T % WKV_T_BLOCK == 0 else T)
    (h_new,) = _token_call(
        _rwkv_post_kernel,
        [('tile', y), ('tile', r), ('tile', k), ('tile', v), ('tile', z), ('tile', h), ('const', row(lnx_w)),
         ('const', row(lnx_b)), ('const', r_k.reshape(1, INNER)), ('const', seg), ('const', bf(w_out))],
        [D], "rwkv_post")
    return h_new, v_first, s_T


ROT_HALF = ROT_DIM // 2


def _rope_rows(x, cos, sin_lo, sin_hi):
    out = []
    for c in range(x.shape[1] // LANES):
        xc = x[:, c * LANES:(c + 1) * LANES]
        out.append(xc * cos + pltpu.roll(xc, LANES - ROT_HALF, 1) * sin_lo + pltpu.roll(xc, ROT_HALF, 1) * sin_hi)
    return jnp.concatenate(out, axis=1)


def _rope_cols(x, cos, sin):
    n = x.shape[0] // HEAD_DIM
    x = x.reshape(n, HEAD_DIM, x.shape[1])
    x1, x2 = x[:, :ROT_HALF], x[:, ROT_HALF:ROT_DIM]
    y = jnp.concatenate([x1 * cos - x2 * sin, x1 * sin + x2 * cos, x[:, ROT_DIM:]], axis=1)
    return y.reshape(n * HEAD_DIM, y.shape[2])


def _nsa_rows_kernel(h_ref, g_ref, w_ref, cos_ref, slo_ref, shi_ref, cmp_ref, sel_ref, win_ref, z_ref, ks_ref, kw_ref):
    xn = _rms(h_ref[0], g_ref[...])
    p = _bdot(xn, w_ref[...])
    cos, slo, shi = cos_ref[0], slo_ref[0], shi_ref[0]
    cmp_ref[0] = p[:, :2 * KV_W]
    ks = _rope_rows(p[:, 2 * KV_W:3 * KV_W], cos, slo, shi)
    kw = _rope_rows(p[:, 4 * KV_W:5 * KV_W], cos, slo, shi)
    sel_ref[0] = jnp.concatenate([ks, p[:, 3 * KV_W:4 * KV_W]], axis=1)
    win_ref[0] = jnp.concatenate([kw, p[:, 5 * KV_W:6 * KV_W]], axis=1)
    z_ref[0] = p[:, 6 * KV_W:]
    ks_ref[0] = ks.astype(BF16)
    kw_ref[0] = kw.astype(BF16)


def _nsa_cols_kernel(h_ref, g_ref, wT_ref, cos_ref, sin_ref, qT_ref, qrT_ref, vsT_ref, vwT_ref, gT_ref):
    xn = _rms(h_ref[0], g_ref[...]).astype(BF16)
    pT = lax.dot_general(wT_ref[...], xn, (((1,), (1,)), ((), ())), preferred_element_type=F32)
    q = pT[:INNER]
    qT_ref[0] = q.astype(BF16)
    qrT_ref[0] = (_rope_cols(q, cos_ref[...], sin_ref[...]) * LOG2_E).astype(BF16)
    vsT_ref[0] = pT[INNER:INNER + KV_W].astype(BF16)
    vwT_ref[0] = pT[INNER + KV_W:INNER + 2 * KV_W].astype(BF16)
    gT_ref[0] = _sigmoid(pT[INNER + 2 * KV_W:])


def _rope_tables(pos):
    inv = ROPE_THETA ** (-jnp.arange(ROT_HALF, dtype=F32) / ROT_HALF)
    ang = pos.astype(F32)[:, None] * inv[None, :]
    cos, sin = jnp.cos(ang), jnp.sin(ang)
    lane = jnp.arange(LANES) % HEAD_DIM
    f = lane % ROT_HALF
    cos_l = jnp.where(lane[None, :] < ROT_DIM, cos[:, f], 1.0)
    slo_l = jnp.where(lane[None, :] < ROT_HALF, -sin[:, f], 0.0)
    shi_l = jnp.where((lane[None, :] >= ROT_HALF) & (lane[None, :] < ROT_DIM), sin[:, f], 0.0)
    return cos_l, slo_l, shi_l, cos.T, sin.T


def nsa_pre(h, norm_g, w_in, pos):
    B, T, D = h.shape
    tm = TOKEN_TILE if T % TOKEN_TILE == 0 else T
    cos_l, slo_l, shi_l, cosT, sinT = _rope_tables(pos)
    scale = HEAD_DIM ** -0.5
    g_row = norm_g.reshape(1, D)
    w_rows = jnp.concatenate([w_in[:, Q_END:KV_END], w_in[:, G_END:]], axis=1).astype(BF16)
    w_g = w_in[:, KV_END:G_END].reshape(D, N_KV, HPG, 3).transpose(0, 1, 3, 2).reshape(D, 3 * N_HEADS)
    w_cols = jnp.concatenate([w_in[:, :Q_END] * scale, w_in[:, Q_END + 3 * KV_W:Q_END + 4 * KV_W],
                              w_in[:, Q_END + 5 * KV_W:Q_END + 6 * KV_W], w_g], axis=1).T.astype(BF16)
    tile = lambda w: pl.BlockSpec((1, tm, w), lambda b, t: (b, t, 0))
    whole = lambda a: pl.BlockSpec(a.shape, lambda b, t, n=a.ndim: (0,) * n)
    tab = pl.BlockSpec((1, tm, LANES), lambda b, t: (0, t, 0))
    params = pltpu.CompilerParams(dimension_semantics=("arbitrary", "arbitrary"), vmem_limit_bytes=VMEM_LIMIT_BYTES)
    cmp_rows, sel_rows, win_rows, z, ks, kw = pl.pallas_call(
        _nsa_rows_kernel,
        grid=(B, T // tm),
        in_specs=[tile(D), whole(g_row), whole(w_rows), tab, tab, tab],
        out_specs=[tile(2 * KV_W), tile(2 * KV_W), tile(2 * KV_W), tile(INNER), tile(KV_W), tile(KV_W)],
        out_shape=[jax.ShapeDtypeStruct((B, T, 2 * KV_W), F32)] * 3 + [jax.ShapeDtypeStruct((B, T, INNER), F32)]
        + [jax.ShapeDtypeStruct((B, T, KV_W), BF16)] * 2,
        compiler_params=params, name="nsa_rows",
    )(h, g_row, w_rows, cos_l[None], slo_l[None], shi_l[None])
    colt = lambda r: pl.BlockSpec((1, r, tm), lambda b, t: (b, 0, t))
    tabT = pl.BlockSpec((ROT_HALF, tm), lambda b, t: (0, t))
    qT, qrT, vsT, vwT, gT = pl.pallas_call(
        _nsa_cols_kernel,
        grid=(B, T // tm),
        in_specs=[tile(D), whole(g_row), whole(w_cols), tabT, tabT],
        out_specs=[colt(INNER), colt(INNER), colt(KV_W), colt(KV_W), colt(3 * N_HEADS)],
        out_shape=[jax.ShapeDtypeStruct((B, INNER, T), BF16)] * 2 + [jax.ShapeDtypeStruct((B, KV_W, T), BF16)] * 2
        + [jax.ShapeDtypeStruct((B, 3 * N_HEADS, T), F32)],
        compiler_params=params, name="nsa_cols",
    )(h, g_row, w_cols, cosT, sinT)
    return cmp_rows, sel_rows, win_rows, z, ks, kw, qT, qrT, vsT, vwT, gT


PAGE = 128
SAMPLE_VMEM_LIMIT_BYTES = 56 * 1024 * 1024


def _gather_pages(pt_ref, b, pool_hbm, buf, sem, n_pages):
    rows = pool_hbm.shape[1]
    copies = [pltpu.make_async_copy(pool_hbm.at[pt_ref[b, p]], buf.at[pl.ds(p * rows, rows)], sem)
              for p in range(n_pages)]
    for cp in copies:
        cp.start()
    for cp in copies:
        cp.wait()


def _sample_compress_kernel(pt_ref, pool_hbm, w1_hbm, pe_ref, w2_ref, out_ref, buf, w1_vmem, sem, wsem, *, n_pages):
    b = pl.program_id(0)

    @pl.when(b == 0)
    def _():
        cp = pltpu.make_async_copy(w1_hbm, w1_vmem, wsem)
        cp.start()
        cp.wait()

    _gather_pages(pt_ref, b, pool_hbm, buf, sem, n_pages)
    n_chunks = n_pages * PAGE // CMP_STRIDE
    row = lax.broadcasted_iota(jnp.int32, (n_chunks, 1), 0)
    for kv in range(2):
        lanes = slice(kv * KV_W, (kv + 1) * KV_W)
        parts = []
        for m in range(CMP_RATIO):
            acc = jnp.zeros((n_chunks, w1_vmem.shape[3]), F32)
            for s in range(CMP_STRIDE):
                at = s * 2 * KV_W + kv * KV_W
                x = buf[:, at:at + KV_W] + pe_ref[m * CMP_STRIDE + s:m * CMP_STRIDE + s + 1, :]
                acc = acc + jnp.dot(x.astype(BF16), w1_vmem[kv, m * CMP_STRIDE + s], preferred_element_type=F32)
            parts.append(acc)
        pre = parts[0] + pltpu.roll(parts[1], n_chunks - 1, 0)
        hid = pre * _sigmoid(pre)
        ck = jnp.dot(hid.astype(BF16), w2_ref[kv], preferred_element_type=F32)
        out_ref[0, :, lanes] = jnp.where(row < n_chunks - 1, ck, 0.0)


def _block_diag(w):
    eye = jnp.eye(N_KV, dtype=w.dtype)
    out = eye[:, None, :, None] * w[..., None, :, None, :]
    return out.reshape(w.shape[:-2] + (N_KV * w.shape[-2], N_KV * w.shape[-1]))


def sample_compress(pool, page_table, pe, w1, w2):
    B, n_pages = page_table.shape
    n_chunks = n_pages * PAGE // CMP_STRIDE
    pool2 = pool.reshape(pool.shape[0], PAGE // CMP_STRIDE, CMP_STRIDE * 2 * KV_W)
    w1_bd = _block_diag(w1).astype(BF16)
    w2_bd = _block_diag(w2).astype(BF16)
    pe_t = jnp.tile(pe, (1, N_KV))
    return pl.pallas_call(
        functools.partial(_sample_compress_kernel, n_pages=n_pages),
        grid_spec=pltpu.PrefetchScalarGridSpec(
            num_scalar_prefetch=1,
            grid=(B,),
            in_specs=[pl.BlockSpec(memory_space=pl.ANY), pl.BlockSpec(memory_space=pl.ANY),
                      pl.BlockSpec(pe_t.shape, lambda b, pt: (0, 0)),
                      pl.BlockSpec(w2_bd.shape, lambda b, pt: (0, 0, 0))],
            out_specs=pl.BlockSpec((1, n_chunks, 2 * KV_W), lambda b, pt: (b, 0, 0)),
            scratch_shapes=[pltpu.VMEM((n_chunks, CMP_STRIDE * 2 * KV_W), F32),
                            pltpu.VMEM(w1_bd.shape, BF16),
                            pltpu.SemaphoreType.DMA(()), pltpu.SemaphoreType.DMA(())]),
        out_shape=jax.ShapeDtypeStruct((B, n_chunks, 2 * KV_W), F32),
        compiler_params=pltpu.CompilerParams(
            dimension_semantics=("arbitrary",), vmem_limit_bytes=SAMPLE_VMEM_LIMIT_BYTES),
        name="sample_compress",
    )(page_table, pool2, w1_bd, pe_t, w2_bd)


SAMPLE_TILE = 512


def _online_step(s, v, m_ref, l_ref, acc_ref):
    m_old = m_ref[...]
    m_new = jnp.maximum(m_old, jnp.max(s, axis=0, keepdims=True))
    alpha = jnp.exp2(m_old - m_new)
    p = jnp.exp2(s - m_new)
    l_ref[...] = alpha * l_ref[...] + jnp.sum(p, axis=0, keepdims=True)
    acc_ref[...] = alpha * acc_ref[...] + lax.dot_general(
        v.astype(BF16), p.astype(BF16), (((0,), (0,)), ((), ())), preferred_element_type=F32)
    m_ref[...] = m_new


def _split_dot_left(m, x):
    hi = x.astype(BF16)
    lo = (x - hi.astype(F32)).astype(BF16)
    return jnp.dot(m, hi, preferred_element_type=F32) + jnp.dot(m, lo, preferred_element_type=F32)


def _sample_attend_kernel(pt_ref, pool_hbm, ckv_ref, qraw_ref, qrot_ref, g_ref, selnew_ref, win_ref, winnew_ref,
                          ovT_ref, fold_ref, o_ref, buf, sel_scr, m_scr, l_scr, acc_scr, sem,
                          *, n_pages, n_cmp, n_sel, past, t_new):
    b = pl.program_id(0)
    _gather_pages(pt_ref, b, pool_hbm, buf, sem, n_pages)
    width = qraw_ref.shape[2]
    q_idx = lax.broadcasted_iota(jnp.int32, (1, width), 1) % t_new
    qpos = past + q_idx

    n_rows = ckv_ref.shape[1]
    n_iota = lax.broadcasted_iota(jnp.int32, (n_rows, width), 0)
    c_valid = (n_iota < n_cmp) & ((n_iota * CMP_STRIDE + (CMP_BLOCK - 1)) <= qpos)
    s = jnp.dot(ckv_ref[0, :, :KV_W].astype(BF16), qraw_ref[0], preferred_element_type=F32)
    s = jnp.where(c_valid, s, NEG)
    mx = jnp.max(s, axis=0, keepdims=True)
    e = jnp.where(c_valid, jnp.exp(s - mx), 0.0)
    l = jnp.sum(e, axis=0, keepdims=True)
    p = e * (1.0 / jnp.where(l > 0.0, l, 1.0))
    o_ref[0] = g_ref[0, 0:1, :] * lax.dot_general(
        ckv_ref[0, :, KV_W:].astype(BF16), p.astype(BF16), (((0,), (0,)), ((), ())), preferred_element_type=F32)

    fold = fold_ref[...]
    p_sum = _split_dot(p, fold, ((1,), (0,)))
    imp = _split_dot_left(ovT_ref[...], p_sum)
    j_iota = lax.broadcasted_iota(jnp.int32, imp.shape, 0)
    cur = (past + lax.broadcasted_iota(jnp.int32, (1, imp.shape[1]), 1) % t_new) // SEL_BLOCK
    forced = (j_iota == 0) | (j_iota == cur) | (j_iota == cur - 1)
    score = jnp.where(j_iota <= cur, imp + jnp.where(forced, FORCE_BONUS, 0.0), NEG)
    sel = jnp.zeros(imp.shape, F32)
    for _ in range(TOP_N):
        mx = jnp.max(score, axis=0, keepdims=True)
        first = jnp.min(jnp.where(score == mx, j_iota, n_sel), axis=0, keepdims=True)
        pick = j_iota == first
        sel = jnp.where(pick & (mx > NEG / 2), 1.0, sel)
        score = jnp.where(pick, PICKED, score)
    sel_scr[...] = lax.dot_general(sel.astype(BF16), fold, (((1,), (1,)), ((), ())), preferred_element_type=F32)

    def reset():
        m_scr[...] = jnp.full(m_scr.shape, NEG, F32)
        l_scr[...] = jnp.zeros(l_scr.shape, F32)
        acc_scr[...] = jnp.zeros(acc_scr.shape, F32)

    qrot = qrot_ref[0]
    new_rows = lax.broadcasted_iota(jnp.int32, (t_new, width), 0)
    causal_new = new_rows <= q_idx

    reset()
    blocks_per_tile = SAMPLE_TILE // SEL_BLOCK
    for kt in range(n_pages * PAGE // SAMPLE_TILE):
        chosen = sel_scr[kt * blocks_per_tile:(kt + 1) * blocks_per_tile, :]
        chosen = jnp.concatenate(
            [jnp.broadcast_to(chosen[i:i + 1, :], (SEL_BLOCK, width)) for i in range(blocks_per_tile)], axis=0)
        rows = buf[kt * SAMPLE_TILE:(kt + 1) * SAMPLE_TILE, :]
        s = jnp.dot(rows[:, :KV_W].astype(BF16), qrot, preferred_element_type=F32)
        _online_step(s + jnp.where(chosen > 0.5, 0.0, NEG), rows[:, KV_W:], m_scr, l_scr, acc_scr)
    last = past // SEL_BLOCK
    allowed = causal_new & (sel_scr[last:last + 1, :] > 0.5)
    s = jnp.dot(selnew_ref[0, :, :KV_W].astype(BF16), qrot, preferred_element_type=F32)
    _online_step(s + jnp.where(allowed, 0.0, NEG), selnew_ref[0, :, KV_W:], m_scr, l_scr, acc_scr)
    o_ref[0] = o_ref[0] + g_ref[0, 1:2, :] * acc_scr[...] * (1.0 / l_scr[...])

    reset()
    n_buf = win_ref.shape[1]
    w_pos = past - n_buf + lax.broadcasted_iota(jnp.int32, (n_buf, width), 0)
    allowed = (w_pos >= qpos - WINDOW) & (w_pos >= 0)
    s = jnp.dot(win_ref[0, :, :KV_W].astype(BF16), qrot, preferred_element_type=F32)
    _online_step(s + jnp.where(allowed, 0.0, NEG), win_ref[0, :, KV_W:], m_scr, l_scr, acc_scr)
    s = jnp.dot(winnew_ref[0, :, :KV_W].astype(BF16), qrot, preferred_element_type=F32)
    _online_step(s + jnp.where(causal_new, 0.0, NEG), winnew_ref[0, :, KV_W:], m_scr, l_scr, acc_scr)
    o_ref[0] = o_ref[0] + g_ref[0, 2:3, :] * acc_scr[...] * (1.0 / l_scr[...])


def sample_attend(pool, page_table, ckv, qT, qrT, gT, sel_new, win_buf, win_new):
    B, n_pages = page_table.shape
    t_new = sel_new.shape[1]
    past = n_pages * PAGE
    width = N_HEADS * t_new
    assert t_new < CMP_STRIDE and t_new <= SEL_BLOCK and past % SAMPLE_TILE == 0 and win_buf.shape[1] <= past
    assert t_new <= win_buf.shape[1] == min(WINDOW, past)
    n_cmp = past // CMP_STRIDE - CMP_RATIO + 1
    n_sel = -(-(past // SEL_BLOCK + 1) // SUBLANES) * SUBLANES

    def block_q(x):
        x = x.reshape(N_KV, HPG, HEAD_DIM, B, t_new).transpose(3, 0, 2, 1, 4)
        eye = jnp.eye(N_KV, dtype=x.dtype)
        x = x[:, :, :, None, :, :] * eye[None, :, None, :, None, None]
        return x.reshape(B, KV_W, width)

    gates = gT.reshape(N_KV, 3, HPG, B, t_new).transpose(3, 1, 0, 2, 4).reshape(B, 3, width)
    cs = jnp.arange(ckv.shape[1]) * CMP_STRIDE
    ss = jnp.arange(n_sel) * SEL_BLOCK
    ovT = (jnp.clip(jnp.minimum(cs[None, :] + CMP_BLOCK, ss[:, None] + SEL_BLOCK)
                    - jnp.maximum(cs[None, :], ss[:, None]), 0, None).astype(F32) / CMP_BLOCK)
    ovT = jnp.where(jnp.arange(ckv.shape[1])[None, :] < n_cmp, ovT, 0.0).astype(BF16)
    lane = jnp.arange(width)
    col = (lane // (HPG * t_new)) * t_new + lane % t_new
    fold = (col[:, None] == jnp.arange(LANES)[None, :]).astype(BF16)
    per_b = lambda a: pl.BlockSpec((1,) + a.shape[1:], lambda b, pt, n=a.ndim: (b,) + (0,) * (n - 1))
    whole = lambda a: pl.BlockSpec(a.shape, lambda b, pt, n=a.ndim: (0,) * n)
    pool2 = pool.reshape(pool.shape[0], PAGE, 2 * KV_W)
    qraw, qrot = block_q(qT), block_q(qrT)
    o_bd = pl.pallas_call(
        functools.partial(_sample_attend_kernel, n_pages=n_pages, n_cmp=n_cmp, n_sel=n_sel, past=past, t_new=t_new),
        grid_spec=pltpu.PrefetchScalarGridSpec(
            num_scalar_prefetch=1,
            grid=(B,),
            in_specs=[pl.BlockSpec(memory_space=pl.ANY), per_b(ckv), per_b(qraw), per_b(qrot), per_b(gates),
                      per_b(sel_new), per_b(win_buf), per_b(win_new), whole(ovT), whole(fold)],
            out_specs=pl.BlockSpec((1, KV_W, width), lambda b, pt: (b, 0, 0)),
            scratch_shapes=[pltpu.VMEM((past, 2 * KV_W), F32),
                            pltpu.VMEM((n_sel, width), F32),
                            pltpu.VMEM((1, width), F32), pltpu.VMEM((1, width), F32),
                            pltpu.VMEM((KV_W, width), F32),
                            pltpu.SemaphoreType.DMA(())]),
        out_shape=jax.ShapeDtypeStruct((B, KV_W, width), F32),
        compiler_params=pltpu.CompilerParams(dimension_semantics=("arbitrary",), vmem_limit_bytes=VMEM_LIMIT_BYTES),
        name="sample_attend",
    )(page_table, pool2, ckv, qraw, qrot, gates, sel_new, win_buf, win_new, ovT, fold)
    o = o_bd.reshape(B, N_KV, HEAD_DIM, N_KV, HPG, t_new)
    o = jnp.stack([o[:, g, :, g] for g in range(N_KV)], axis=1)
    return o.transpose(0, 4, 1, 3, 2).reshape(B, t_new, INNER)


CHUNK_TILE = 128


def _prompt_compress_kernel(x_ref, nxt_ref, w1_hbm, pe_ref, w2_ref, out_ref, w1_vmem, wsem, *, n_valid):
    t = pl.program_id(1)

    @pl.when((pl.program_id(0) == 0) & (t == 0))
    def _():
        cp = pltpu.make_async_copy(w1_hbm, w1_vmem, wsem)
        cp.start()
        cp.wait()

    rows = x_ref.shape[1]
    row = lax.broadcasted_iota(jnp.int32, (rows, 1), 0)
    for kv in range(2):
        lanes = slice(kv * KV_W, (kv + 1) * KV_W)
        pre = jnp.zeros((rows, w1_vmem.shape[3]), F32)
        for m in range(CMP_RATIO):
            for s in range(CMP_STRIDE):
                at = s * 2 * KV_W + kv * KV_W
                x = x_ref[0, :, at:at + KV_W]
                if m == 1:
                    x = jnp.where(row == rows - 1, nxt_ref[0, 0:1, at:at + KV_W], pltpu.roll(x, rows - 1, 0))
                x = x + pe_ref[m * CMP_STRIDE + s:m * CMP_STRIDE + s + 1, :]
                pre = pre + jnp.dot(x.astype(BF16), w1_vmem[kv, m * CMP_STRIDE + s], preferred_element_type=F32)
        hid = pre * _sigmoid(pre)
        ck = jnp.dot(hid.astype(BF16), w2_ref[kv], preferred_element_type=F32)
        out_ref[0, :, lanes] = jnp.where(t * rows + row < n_valid, ck, 0.0)


def prompt_compress(cmp_rows, pe, w1, w2):
    B, T, _ = cmp_rows.shape
    n_chunks = T // CMP_STRIDE
    rows = CHUNK_TILE if n_chunks % CHUNK_TILE == 0 else n_chunks
    x = cmp_rows.reshape(B, n_chunks, CMP_STRIDE * 2 * KV_W)
    w1_bd = _block_diag(w1).astype(BF16)
    w2_bd = _block_diag(w2).astype(BF16)
    pe_t = jnp.tile(pe, (1, N_KV))
    per_tile = rows // SUBLANES
    last = n_chunks // SUBLANES - 1
    return pl.pallas_call(
        functools.partial(_prompt_compress_kernel, n_valid=n_chunks - CMP_RATIO + 1),
        grid=(B, n_chunks // rows),
        in_specs=[pl.BlockSpec((1, rows, x.shape[2]), lambda b, t: (b, t, 0)),
                  pl.BlockSpec((1, SUBLANES, x.shape[2]), lambda b, t: (b, jnp.minimum((t + 1) * per_tile, last), 0)),
                  pl.BlockSpec(memory_space=pl.ANY),
                  pl.BlockSpec(pe_t.shape, lambda b, t: (0, 0)),
                  pl.BlockSpec(w2_bd.shape, lambda b, t: (0, 0, 0))],
        out_specs=pl.BlockSpec((1, rows, 2 * KV_W), lambda b, t: (b, t, 0)),
        out_shape=jax.ShapeDtypeStruct((B, n_chunks, 2 * KV_W), F32),
        scratch_shapes=[pltpu.VMEM(w1_bd.shape, BF16), pltpu.SemaphoreType.DMA(())],
        compiler_params=pltpu.CompilerParams(
            dimension_semantics=("arbitrary", "arbitrary"), vmem_limit_bytes=VMEM_LIMIT_BYTES),
        name="prompt_compress",
    )(x, x, w1_bd, pe_t, w2_bd)


def _residual_matmul_kernel(x_ref, h_ref, w_ref, o_ref):
    o_ref[0] = h_ref[0] + jnp.dot(x_ref[0], w_ref[...], preferred_element_type=F32)


def nsa_prompt(h, norm_g, w_in, w_out, pe, cw1, cw2):
    B, T, _ = h.shape
    cmp_rows, sel_rows, win_rows, z, ks, kw, qT, qrT, vsT, vwT, gT = nsa_pre(h, norm_g, w_in, jnp.arange(T))
    ckv = prompt_compress(cmp_rows, pe, cw1, cw2)
    nc = T // CMP_STRIDE - CMP_RATIO + 1
    blocks = lambda t: t[:, :nc].reshape(B, nc, N_KV, HEAD_DIM)
    gated = nsa_prompt_attend(qT, qrT, gT, z, blocks(ckv[..., :KV_W]), blocks(ckv[..., KV_W:]), ks, vsT, kw, vwT)
    (h_new,) = _token_call(_residual_matmul_kernel, [('tile', gated), ('tile', h), ('const', w_out.astype(BF16))],
                           [D_MODEL], "nsa_out")
    n_keep = min(WINDOW, T)
    return h_new, cmp_rows, sel_rows, win_rows[:, -n_keep:]


def nsa_sample(h, norm_g, pool_cmp, pool_sel, win_buf, page_table, w_in, w_out, pe, cw1, cw2):
    B, T, D = h.shape
    past = page_table.shape[1] * pool_cmp.shape[1]
    pos = past + jnp.arange(B * T) % T
    cmp_rows, sel_rows, win_rows, z, _, _, qT, qrT, _, _, gT = nsa_pre(h.reshape(1, B * T, D), norm_g, w_in, pos)
    rows = lambda t: t.reshape(B, T, 2 * KV_W)
    ckv = sample_compress(pool_cmp, page_table, pe, cw1, cw2)
    n_buf = win_buf.shape[1]
    win_flat = win_buf.reshape(B, n_buf, 2 * KV_W)
    o = sample_attend(pool_sel, page_table, ckv, qT[0], qrT[0], gT[0], rows(sel_rows), win_flat, rows(win_rows))
    gated = (o * jax.nn.silu(z.reshape(B, T, INNER))).astype(BF16)
    (h_new,) = _token_call(_residual_matmul_kernel, [('tile', gated), ('tile', h), ('const', w_out.astype(BF16))],
                           [D_MODEL], "nsa_out")
    n_keep = min(WINDOW, n_buf + T)
    win_out = jnp.concatenate([win_flat, rows(win_rows)], axis=1)[:, -n_keep:]
    return h_new, rows(cmp_rows), rows(sel_rows), win_out


def kernel(x_prompt, x_sample, state_rwkv_wkv, state_rwkv_shift, cache_nsa_cmp, cache_nsa_sel, cache_nsa_win, page_table, norm_g, final_norm_g, rwkv_mu, rwkv_w_in, rwkv_w_out, rwkv_w0, rwkv_w1, rwkv_w2, rwkv_a0, rwkv_a1, rwkv_a2, rwkv_v0, rwkv_v1, rwkv_v2, rwkv_k_k, rwkv_k_a, rwkv_r_k, rwkv_lnx_w, rwkv_lnx_b, nsa_w_in, nsa_w_out, nsa_cmp_pe, nsa_cmp_w1, nsa_cmp_w2):

    def trunk(x, wkv0, shift0, sample):
        h = x
        v_first = None
        wkv, shift, cmp_rows, sel_rows, win_rows = [], [], [], [], []
        for layer in range(DEPTH):
            j = layer // N_MIXERS
            if layer % N_MIXERS == 0:
                vres = None if j == 0 else (rwkv_v0[j - 1], rwkv_v1[j - 1], rwkv_v2[j - 1])
                shift.append(rms_norm(h[:, -1], norm_g[layer]))
                h, v_first, s_T = rwkv_layer(
                    h, shift0[j], wkv0[j], v_first, norm_g[layer], rwkv_mu[j], rwkv_w_in[j], rwkv_w_out[j],
                    rwkv_w0[j], rwkv_w1[j], rwkv_w2[j], rwkv_a0[j], rwkv_a1[j], rwkv_a2[j],
                    rwkv_k_k[j], rwkv_k_a[j], rwkv_r_k[j], rwkv_lnx_w[j], rwkv_lnx_b[j], vres)
                wkv.append(s_T)
            else:
                if sample:
                    h, c, s, w = nsa_sample(h, norm_g[layer], cache_nsa_cmp[j], cache_nsa_sel[j], cache_nsa_win[j],
                                            page_table, nsa_w_in[j], nsa_w_out[j], nsa_cmp_pe[j],
                                            nsa_cmp_w1[j], nsa_cmp_w2[j])
                else:
                    h, c, s, w = nsa_prompt(h, norm_g[layer], nsa_w_in[j], nsa_w_out[j], nsa_cmp_pe[j],
                                            nsa_cmp_w1[j], nsa_cmp_w2[j])
                cmp_rows.append(c)
                sel_rows.append(s)
                win_rows.append(w)
        def stacked(parts):
            t = jnp.stack(parts)
            return t.reshape(t.shape[:3] + (2, N_KV, HEAD_DIM))

        return (rms_norm(h, final_norm_g), jnp.stack(wkv), jnp.stack(shift),
                stacked(cmp_rows), stacked(sel_rows), stacked(win_rows))

    zeros_wkv = jnp.zeros((rwkv_mu.shape[0], x_prompt.shape[0], RW_HEADS, RW_HEAD, RW_HEAD), x_prompt.dtype)
    zeros_shift = jnp.zeros((rwkv_mu.shape[0], x_prompt.shape[0], D_MODEL), x_prompt.dtype)
    y_prompt, wkv_p, shift_p, cmp_p, sel_p, win_p = trunk(x_prompt, zeros_wkv, zeros_shift, False)
    y_sample, wkv_s, shift_s, cmp_s, sel_s, win_s = trunk(x_sample, state_rwkv_wkv, state_rwkv_shift, True)
    return (y_prompt, y_sample, wkv_p, wkv_s, shift_p, shift_s, cmp_p, cmp_s, sel_p, sel_s, win_p, win_s)
```

```python
import functools

import jax
import jax.numpy as jnp
from jax import lax
from jax.experimental import pallas as pl
from jax.experimental.pallas import tpu as pltpu

D_MODEL = 1024
DEPTH = 4
N_MIXERS = 2
INNER = 2 * D_MODEL
NORM_EPS = 1e-6
RW_HEAD = 64
RW_HEADS = INNER // RW_HEAD
LNX_EPS = 64e-5
HEAD_DIM = 64
N_HEADS = INNER // HEAD_DIM
N_KV = 4
HPG = N_HEADS // N_KV
KV_W = N_KV * HEAD_DIM
ROT_DIM = HEAD_DIM // 4
ROPE_THETA = 500000.0
CMP_BLOCK = 32
CMP_STRIDE = 16
CMP_RATIO = CMP_BLOCK // CMP_STRIDE
SEL_BLOCK = 64
TOP_N = 16
WINDOW = 512
Q_BLOCK = 128
NEG = -1e30
FORCE_BONUS = 1e4
Q_END = INNER
KV_END = Q_END + 6 * KV_W
G_END = KV_END + 3 * N_HEADS
NSA_IN = G_END + INNER

F32 = jnp.float32
BF16 = jnp.bfloat16
VMEM_LIMIT_BYTES = 48 * 1024 * 1024


LANES = 128
WKV_CHUNK = 64
WKV_T_BLOCK = 256
WKV_PAIRS = 16


def _dot(a, b):
    return jnp.dot(a.astype(BF16), b.astype(BF16), preferred_element_type=F32)


def _dot_nt(a, b):
    return lax.dot_general(a.astype(BF16), b.astype(BF16), (((1,), (1,)), ((), ())), preferred_element_type=F32)


def _dot_tn(a, b):
    return lax.dot_general(a.astype(BF16), b.astype(BF16), (((0,), (0,)), ((), ())), preferred_element_type=F32)


def _wkv_kernel(r_ref, ld_ref, k_ref, v_ref, kk_ref, a_ref, s0_ref, y_ref, st_ref, s_scr, *, chunk, n_chunks, n_pairs):
    C = chunk
    R = 2 * C
    tb = pl.program_id(2)

    @pl.when(tb == 0)
    def _():
        z = jnp.zeros((RW_HEAD, RW_HEAD), F32)
        for g in range(n_pairs):
            s_scr[g] = jnp.concatenate(
                [jnp.concatenate([s0_ref[0, 2 * g], z], axis=1),
                 jnp.concatenate([z, s0_ref[0, 2 * g + 1]], axis=1)], axis=0)

    lane = lax.broadcasted_iota(jnp.int32, (1, LANES), 1)
    head_a = lane < RW_HEAD
    row = lax.broadcasted_iota(jnp.int32, (R, R), 0)
    col = lax.broadcasted_iota(jnp.int32, (R, R), 1)
    same = (row // C) == (col // C)
    strict = same & ((col % C) < (row % C))
    incl = same & ((col % C) <= (row % C))
    eye = (row == col).astype(F32)
    tr = lax.broadcasted_iota(jnp.int32, (C, C), 0)
    tc = lax.broadcasted_iota(jnp.int32, (C, C), 1)
    tri = (tc <= tr).astype(BF16)

    def stack(z):
        return jnp.concatenate([jnp.where(head_a, z, 0.0), jnp.where(head_a, 0.0, z)], axis=0)

    def body(ci, carry):
        sl = pl.ds(pl.multiple_of(ci * C, C), C)
        G = range(n_pairs)
        ld = [ld_ref[0, sl, g * LANES:(g + 1) * LANES] for g in G]
        ld_hi = [x.astype(BF16) for x in ld]
        ld_lo = [(x - h.astype(F32)).astype(BF16) for x, h in zip(ld, ld_hi)]
        cum = [jnp.dot(tri, h, preferred_element_type=F32) + jnp.dot(tri, l, preferred_element_type=F32)
               for h, l in zip(ld_hi, ld_lo)]
        e_neg = [jnp.exp(-c) for c in cum]
        kk = [kk_ref[0, sl, g * LANES:(g + 1) * LANES] for g in G]
        a_s = [stack(-kk[g] * jnp.exp(cum[g] - ld[g])) for g in G]
        r_s = [stack(r_ref[0, sl, g * LANES:(g + 1) * LANES] * jnp.exp(cum[g])) for g in G]
        b_s = [stack(kk[g] * a_ref[0, sl, g * LANES:(g + 1) * LANES] * e_neg[g]) for g in G]
        k_s = [stack(k_ref[0, sl, g * LANES:(g + 1) * LANES] * e_neg[g]) for g in G]
        v_s = [stack(v_ref[0, sl, g * LANES:(g + 1) * LANES]) for g in G]
        s2 = [s_scr[g] for g in G]
        ab = [jnp.where(strict, _dot_nt(a_s[g], b_s[g]), 0.0) for g in G]
        ak = [jnp.where(strict, _dot_nt(a_s[g], k_s[g]), 0.0) for g in G]
        rb = [jnp.where(incl, _dot_nt(r_s[g], b_s[g]), 0.0) for g in G]
        rk = [jnp.where(incl, _dot_nt(r_s[g], k_s[g]), 0.0) for g in G]
        rhs = [_dot_nt(a_s[g], s2[g]) + _dot(ak[g], v_s[g]) for g in G]
        y0 = [_dot_nt(r_s[g], s2[g]) + _dot(rk[g], v_s[g]) for g in G]
        tm = [eye + ab[g] for g in G]
        p = ab
        n = 2
        while n < C:
            p = [_dot(p[g], p[g]) for g in G]
            tm = [tm[g] + _dot(tm[g], p[g]) for g in G]
            n *= 2
        u_s = [_dot(tm[g], rhs[g]) for g in G]
        y_s = [y0[g] + _dot(rb[g], u_s[g]) for g in G]
        for g in G:
            y_ref[0, sl, g * LANES:(g + 1) * LANES] = y_s[g][:C] + y_s[g][C:]
            gamma = jnp.exp(cum[g][C - 1:C, :])
            s_scr[g] = (s2[g] + _dot_tn(u_s[g], b_s[g]) + _dot_tn(v_s[g], k_s[g])) * gamma
        return carry

    lax.fori_loop(0, n_chunks, body, 0)

    @pl.when(tb == pl.num_programs(2) - 1)
    def _():
        for g in range(n_pairs):
            st_ref[0, 2 * g] = s_scr[g, :RW_HEAD, :RW_HEAD]
            st_ref[0, 2 * g + 1] = s_scr[g, RW_HEAD:, RW_HEAD:]


def wkv_scan(r, logd, k, v, kk, a, s0, *, chunk, t_block):
    B, T, inner = r.shape
    width = WKV_PAIRS * LANES
    assert T % t_block == 0 and t_block % chunk == 0 and inner % width == 0
    seq_spec = pl.BlockSpec((1, t_block, width), lambda b, p, t: (b, t, p))
    st_spec = pl.BlockSpec((1, 2 * WKV_PAIRS, RW_HEAD, RW_HEAD), lambda b, p, t: (b, p, 0, 0))
    return pl.pallas_call(
        functools.partial(_wkv_kernel, chunk=chunk, n_chunks=t_block // chunk, n_pairs=WKV_PAIRS),
        grid=(B, inner // width, T // t_block),
        in_specs=[seq_spec] * 6 + [st_spec],
        out_specs=[seq_spec, st_spec],
        out_shape=[jax.ShapeDtypeStruct((B, T, inner), F32), jax.ShapeDtypeStruct(s0.shape, F32)],
        scratch_shapes=[pltpu.VMEM((WKV_PAIRS, LANES, LANES), F32)],
        compiler_params=pltpu.CompilerParams(
            dimension_semantics=("arbitrary", "arbitrary", "arbitrary"), vmem_limit_bytes=VMEM_LIMIT_BYTES),
        name="wkv_scan",
    )(r, logd, k, v, kk, a, s0)


KEY_TILE = 512
assert WINDOW <= KEY_TILE and KEY_TILE % Q_BLOCK == 0
BLOCKS_PER_TILE = KEY_TILE // SEL_BLOCK
MASK_COLS = 16
ONES_ROWS = 16
LOG2_E = 1.4426950408889634
PICKED = -3e38


def _nsa_prompt_kernel(qT_ref, qrT_ref, gT_ref, z_ref, ck_ref, cvT_ref, ovT_ref, ks_ref, vsT_ref, kw_ref, vwT_ref,
                       o_ref, sel_scr, m_scr, l_scr, acc_scr, o_scr, s_scr, cm_scr, *, n_cmp, n_sel):
    i = pl.program_id(2)
    q0 = i * Q_BLOCK
    qpos = q0 + lax.broadcasted_iota(jnp.int32, (1, Q_BLOCK), 1)

    def heads_on_lanes(ref):
        return jnp.concatenate([ref[0, h * HEAD_DIM:(h + 1) * HEAD_DIM, :] for h in range(HPG)], axis=1)

    def gate_row(br):
        return jnp.concatenate([gT_ref[0, br * HPG + h:br * HPG + h + 1, :] for h in range(HPG)], axis=1)

    def per_head(x):
        return jnp.concatenate([x] * HPG, axis=1)

    n_iota = lax.broadcasted_iota(jnp.int32, (n_cmp, Q_BLOCK), 0)
    c_valid = per_head((n_iota * CMP_STRIDE + (CMP_BLOCK - 1)) <= qpos)
    s = jnp.dot(ck_ref[0, 0], heads_on_lanes(qT_ref), preferred_element_type=F32)
    s = jnp.where(c_valid, s, NEG)
    mx = jnp.max(s, axis=0, keepdims=True)
    e = jnp.where(c_valid, jnp.exp(s - mx), 0.0)
    l = jnp.sum(e, axis=0, keepdims=True)
    p = e * (1.0 / jnp.where(l > 0.0, l, 1.0))
    o_scr[...] = gate_row(0) * jnp.dot(cvT_ref[0, 0], p.astype(BF16), preferred_element_type=F32)
    p_sum = p[:, :Q_BLOCK]
    for h in range(1, HPG):
        p_sum = p_sum + p[:, h * Q_BLOCK:(h + 1) * Q_BLOCK]

    ps_hi = p_sum.astype(BF16)
    ps_lo = (p_sum - ps_hi.astype(F32)).astype(BF16)
    ovT = ovT_ref[...]
    imp = jnp.dot(ovT, ps_hi, preferred_element_type=F32) + jnp.dot(ovT, ps_lo, preferred_element_type=F32)
    j_iota = lax.broadcasted_iota(jnp.int32, (n_sel, Q_BLOCK), 0)
    cur = qpos // SEL_BLOCK
    forced = (j_iota == 0) | (j_iota == cur) | (j_iota == cur - 1)
    score = jnp.where(j_iota <= cur, imp + jnp.where(forced, FORCE_BONUS, 0.0), NEG)
    sel = jnp.zeros((n_sel, Q_BLOCK), F32)
    for _ in range(TOP_N):
        mx = jnp.max(score, axis=0, keepdims=True)
        first = jnp.min(jnp.where(score == mx, j_iota, n_sel), axis=0, keepdims=True)
        pick = j_iota == first
        sel = jnp.where(pick & (mx > NEG / 2), 1.0, sel)
        score = jnp.where(pick, PICKED, score)
    sel_scr[...] = sel

    def reset():
        m_scr[...] = jnp.full(m_scr.shape, NEG, F32)
        l_scr[...] = jnp.zeros(l_scr.shape, F32)
        acc_scr[...] = jnp.zeros(acc_scr.shape, F32)

    qrT = heads_on_lanes(qrT_ref)
    rows = lax.broadcasted_iota(jnp.int32, (KEY_TILE, Q_BLOCK), 0)
    last = (q0 + Q_BLOCK - 1) // KEY_TILE

    def stage_a(slot, s):
        s_scr[slot] = s
        cm_scr[slot] = jnp.max(s, axis=0, keepdims=True)

    def stage_b(slot, vT_tile):
        m_old = m_scr[...]
        m_new = jnp.maximum(m_old, cm_scr[slot])
        alpha = jnp.exp2(m_old - m_new)
        pv = jnp.dot(vT_tile, jnp.exp2(s_scr[slot] - m_new).astype(BF16), preferred_element_type=F32)
        l_scr[...] = alpha * l_scr[...] + pv[HEAD_DIM:HEAD_DIM + 1, :]
        acc_scr[...] = alpha * acc_scr[...] + pv[:HEAD_DIM, :]
        m_scr[...] = m_new

    def causal(kt):
        return per_head(jnp.where((kt * KEY_TILE + rows) <= qpos, 0.0, NEG))

    reset()
    pad_rows = jnp.zeros((MASK_COLS - BLOCKS_PER_TILE, HPG * Q_BLOCK), BF16)

    def sel_scores(kt):
        chosen = sel_scr[pl.ds(pl.multiple_of(kt * BLOCKS_PER_TILE, BLOCKS_PER_TILE), BLOCKS_PER_TILE), :]
        mask_rows = per_head(jnp.where(chosen > 0.5, 0.0, NEG)).astype(BF16)
        return jnp.dot(ks_ref[0, 0, kt], jnp.concatenate([qrT, mask_rows, pad_rows], axis=0),
                       preferred_element_type=F32)

    stage_a(0, sel_scores(0) + causal(0))

    def sel_body(j, carry):
        for slot, kt in ((1, 2 * j + 1), (0, 2 * j + 2)):
            @pl.when(kt < last)
            def _():
                stage_a(slot, sel_scores(kt))
                stage_b(1 - slot, vsT_ref[0, 0, kt - 1])
        return carry

    lax.fori_loop(0, last // 2, sel_body, 0)

    for slot in (0, 1):
        @pl.when((last > 0) & (last % 2 == slot))
        def _():
            stage_a(slot, sel_scores(last) + causal(last))
            stage_b(1 - slot, vsT_ref[0, 0, last - 1])
            stage_b(slot, vsT_ref[0, 0, last])

    @pl.when(last == 0)
    def _():
        stage_b(0, vsT_ref[0, 0, 0])

    o_scr[...] = o_scr[...] + gate_row(1) * acc_scr[...] * (1.0 / l_scr[...])

    reset()

    def win_scores(kt):
        kp = kt * KEY_TILE + rows
        bias = per_head(jnp.where((kp <= qpos) & (kp >= qpos - WINDOW), 0.0, NEG))
        return jnp.dot(kw_ref[0, 0, kt], qrT, preferred_element_type=F32) + bias

    @pl.when(last > 0)
    def _():
        stage_a(0, win_scores(last - 1))
        stage_a(1, win_scores(last))
        stage_b(0, vwT_ref[0, 0, last - 1])
        stage_b(1, vwT_ref[0, 0, last])

    @pl.when(last == 0)
    def _():
        stage_a(0, win_scores(0))
        stage_b(0, vwT_ref[0, 0, 0])

    o = o_scr[...] + gate_row(2) * acc_scr[...] * (1.0 / l_scr[...])
    cols = []
    for hp in range(HPG // 2):
        pair = jnp.concatenate([o[:, (2 * hp) * Q_BLOCK:(2 * hp + 1) * Q_BLOCK],
                                o[:, (2 * hp + 1) * Q_BLOCK:(2 * hp + 2) * Q_BLOCK]], axis=0)
        cols.append(pair.T)
    z = z_ref[0]
    o_ref[0] = (jnp.concatenate(cols, axis=1) * (z * _sigmoid(z))).astype(o_ref.dtype)


def nsa_prompt_attend(qT, qrT, gT, z, ck, cv, ks, vsT, kw, vwT):
    B, inner, T = qT.shape
    n_sel = T // SEL_BLOCK
    nc = ck.shape[1]
    n_cmp = -(-nc // LANES) * LANES
    grp = HPG * HEAD_DIM
    pad_c = ((0, 0), (0, n_cmp - nc), (0, 0), (0, 0))
    ck_p = jnp.pad(ck, pad_c).transpose(0, 2, 1, 3).astype(BF16)
    cvT = jnp.pad(cv, pad_c).transpose(0, 2, 3, 1).astype(BF16)
    cs = jnp.arange(n_cmp) * CMP_STRIDE
    ss = jnp.arange(n_sel) * SEL_BLOCK
    ovT = (jnp.clip(jnp.minimum(cs[None, :] + CMP_BLOCK, ss[:, None] + SEL_BLOCK)
                    - jnp.maximum(cs[None, :], ss[:, None]), 0, None).astype(F32) / CMP_BLOCK)
    ovT = jnp.where(jnp.arange(n_cmp)[None, :] < nc, ovT, 0.0).astype(BF16)

    n_tiles = T // KEY_TILE

    def key_tiles(t):
        return t.reshape(B, n_tiles, KEY_TILE, N_KV, HEAD_DIM).transpose(0, 3, 1, 2, 4)

    def with_block_columns(t):
        cols = (jnp.arange(KEY_TILE)[:, None] // SEL_BLOCK == jnp.arange(MASK_COLS)[None, :]).astype(t.dtype)
        return jnp.concatenate([t, jnp.broadcast_to(cols, t.shape[:-1] + (MASK_COLS,))], axis=-1)

    def val_tiles(t):
        t = t.reshape(B, N_KV, HEAD_DIM, n_tiles, KEY_TILE).transpose(0, 1, 3, 2, 4)
        ones = jnp.ones(t.shape[:3] + (ONES_ROWS, KEY_TILE), t.dtype)
        return jnp.concatenate([t, ones], axis=3)

    q_spec = pl.BlockSpec((1, grp, Q_BLOCK), lambda b, g, i: (b, g, i))
    row_spec = pl.BlockSpec((1, Q_BLOCK, grp), lambda b, g, i: (b, i, g))
    per_group = lambda shape: pl.BlockSpec((1, 1) + shape, lambda b, g, i: (b, g) + (0,) * len(shape))
    return pl.pallas_call(
        functools.partial(_nsa_prompt_kernel, n_cmp=n_cmp, n_sel=n_sel),
        grid=(B, N_KV, T // Q_BLOCK),
        in_specs=[q_spec, q_spec,
                  pl.BlockSpec((1, 3 * HPG, Q_BLOCK), lambda b, g, i: (b, g, i)),
                  row_spec,
                  per_group((n_cmp, HEAD_DIM)), per_group((HEAD_DIM, n_cmp)),
                  pl.BlockSpec((n_sel, n_cmp), lambda b, g, i: (0, 0)),
                  per_group((n_tiles, KEY_TILE, HEAD_DIM + MASK_COLS)),
                  per_group((n_tiles, HEAD_DIM + ONES_ROWS, KEY_TILE)),
                  per_group((n_tiles, KEY_TILE, HEAD_DIM)), per_group((n_tiles, HEAD_DIM + ONES_ROWS, KEY_TILE))],
        out_specs=row_spec,
        out_shape=jax.ShapeDtypeStruct((B, T, inner), BF16),
        scratch_shapes=[pltpu.VMEM((n_sel, Q_BLOCK), F32),
                        pltpu.VMEM((1, HPG * Q_BLOCK), F32),
                        pltpu.VMEM((1, HPG * Q_BLOCK), F32),
                        pltpu.VMEM((HEAD_DIM, HPG * Q_BLOCK), F32),
                        pltpu.VMEM((HEAD_DIM, HPG * Q_BLOCK), F32),
                        pltpu.VMEM((2, KEY_TILE, HPG * Q_BLOCK), F32),
                        pltpu.VMEM((2, 1, HPG * Q_BLOCK), F32)],
        compiler_params=pltpu.CompilerParams(
            dimension_semantics=("arbitrary", "arbitrary", "arbitrary"), vmem_limit_bytes=VMEM_LIMIT_BYTES),
        name="nsa_prompt_attend",
    )(qT, qrT, gT, z, ck_p, cvT, ovT, with_block_columns(key_tiles(ks)), val_tiles(vsT), key_tiles(kw), val_tiles(vwT))


def rms_norm(x, g):
    y = x * lax.rsqrt(jnp.mean(x * x, -1, keepdims=True) + NORM_EPS)
    return y * g


TOKEN_TILE = 256
SUBLANES = 8


def _bdot(a, b):
    return jnp.dot(a.astype(BF16), b.astype(BF16), preferred_element_type=F32)


def _split_dot(x, m, dims):
    hi = x.astype(BF16)
    lo = (x - hi.astype(F32)).astype(BF16)
    return (lax.dot_general(hi, m, (dims, ((), ())), preferred_element_type=F32)
            + lax.dot_general(lo, m, (dims, ((), ())), preferred_element_type=F32))


def _head_sum(x, seg):
    sums = _split_dot(x, seg, ((1,), (0,)))
    return _split_dot(sums, seg, ((1,), (1,)))


def _rms(x, g):
    return x * lax.rsqrt(jnp.mean(x * x, axis=-1, keepdims=True) + NORM_EPS) * g


def _normed_and_prev(h_ref, hprev_ref, shift_ref, g_ref):
    g = g_ref[...]
    xn = _rms(h_ref[0], g)
    prev_last = _rms(hprev_ref[0], g)[SUBLANES - 1:SUBLANES, :]
    first = jnp.where(pl.program_id(1) == 0, shift_ref[0], prev_last)
    row = lax.broadcasted_iota(jnp.int32, (xn.shape[0], 1), 0)
    return xn, jnp.where(row == 0, first, pltpu.roll(xn, 1, 0))


def _softplus(u):
    return jnp.maximum(u, 0.0) + jnp.log(1.0 + jnp.exp(-jnp.abs(u)))


def _sigmoid(u):
    return 1.0 / (1.0 + jnp.exp(-u))


def _rwkv_r_kernel(h_ref, hprev_ref, shift_ref, g_ref, mu_ref, w_ref, w0_ref, w1_ref, w2_ref, r_ref, ld_ref):
    xn, xp = _normed_and_prev(h_ref, hprev_ref, shift_ref, g_ref)
    dx = xp - xn
    r_ref[0] = _bdot(xn + dx * mu_ref[0:1, :], w_ref[...])
    lora = _bdot(jnp.tanh(_bdot(xn + dx * mu_ref[1:2, :], w1_ref[...])), w2_ref[...])
    w_log = -_softplus(-(w0_ref[...] + lora)) - 0.5
    ld_ref[0] = -jnp.exp(w_log)


def _rwkv_k_kernel(h_ref, hprev_ref, shift_ref, g_ref, mu_ref, w_ref, a0_ref, a1_ref, a2_ref, kk_w_ref, ka_ref,
                   seg_ref, k_ref, kk_ref, a_ref):
    xn, xp = _normed_and_prev(h_ref, hprev_ref, shift_ref, g_ref)
    dx = xp - xn
    k = _bdot(xn + dx * mu_ref[0:1, :], w_ref[...])
    a = _sigmoid(a0_ref[...] + _bdot(_bdot(xn + dx * mu_ref[1:2, :], a1_ref[...]), a2_ref[...]))
    kk = k * kk_w_ref[...]
    kk_ref[0] = kk * lax.rsqrt(jnp.maximum(_head_sum(kk * kk, seg_ref[...]), 1e-24))
    k_ref[0] = k * (1.0 + (a - 1.0) * ka_ref[...])
    a_ref[0] = a


def _rwkv_v_kernel(h_ref, hprev_ref, shift_ref, g_ref, mu_ref, w_ref, *rest, residual):
    xn, xp = _normed_and_prev(h_ref, hprev_ref, shift_ref, g_ref)
    mix = xn + (xp - xn) * mu_ref[0:1, :]
    v = _bdot(mix, w_ref[...])
    if residual:
        vfirst_ref, v0_ref, v1_ref, v2_ref, v_ref = rest
        v = v + (vfirst_ref[0] - v) * _sigmoid(v0_ref[...] + _bdot(_bdot(mix, v1_ref[...]), v2_ref[...]))
    else:
        (v_ref,) = rest
    v_ref[0] = v


def _rwkv_z_kernel(h_ref, hprev_ref, shift_ref, g_ref, mu_ref, w_ref, z_ref):
    xn, xp = _normed_and_prev(h_ref, hprev_ref, shift_ref, g_ref)
    z_ref[0] = _bdot(xn + (xp - xn) * mu_ref[0:1, :], w_ref[...])


def _rwkv_post_kernel(y_ref, r_ref, k_ref, v_ref, z_ref, h_ref, lnw_ref, lnb_ref, rk_ref, seg_ref, wout_ref, o_ref):
    seg = seg_ref[...]
    y = y_ref[0]
    d = y - _head_sum(y, seg) * (1.0 / RW_HEAD)
    var = _head_sum(d * d, seg) * (1.0 / RW_HEAD)
    yn = d * lax.rsqrt(var + LNX_EPS) * lnw_ref[...] + lnb_ref[...]
    yn = yn + _head_sum(r_ref[0] * k_ref[0] * rk_ref[...], seg) * v_ref[0]
    z = z_ref[0]
    o_ref[0] = h_ref[0] + _bdot(yn * (z * _sigmoid(z)), wout_ref[...])


def _head_indicator():
    return (jnp.arange(INNER)[:, None] // RW_HEAD == jnp.arange(LANES)[None, :]).astype(BF16)


def _token_call(kernel, operands, out_widths, name, out_dtype=F32):
    B, T = next(a.shape[:2] for kind, a in operands if kind == 'tile')
    tm = TOKEN_TILE if T % TOKEN_TILE == 0 else T
    per_tile = tm // SUBLANES
    specs = []
    for kind, a in operands:
        if kind == 'tile':
            specs.append(pl.BlockSpec((1, tm, a.shape[2]), lambda b, t: (b, t, 0)))
        elif kind == 'prev':
            specs.append(pl.BlockSpec((1, SUBLANES, a.shape[2]), lambda b, t: (b, jnp.maximum(t * per_tile - 1, 0), 0)))
        elif kind == 'batch':
            specs.append(pl.BlockSpec((1, 1, a.shape[2]), lambda b, t: (b, 0, 0)))
        else:
            specs.append(pl.BlockSpec(a.shape, lambda b, t, n=a.ndim: (0,) * n))
    return pl.pallas_call(
        kernel,
        grid=(B, T // tm),
        in_specs=specs,
        out_specs=[pl.BlockSpec((1, tm, w), lambda b, t: (b, t, 0)) for w in out_widths],
        out_shape=[jax.ShapeDtypeStruct((B, T, w), out_dtype) for w in out_widths],
        compiler_params=pltpu.CompilerParams(
            dimension_semantics=("arbitrary", "arbitrary"), vmem_limit_bytes=VMEM_LIMIT_BYTES),
        name=name,
    )(*[a for _, a in operands])


def rwkv_layer(h, shift_prev, s0, v_first, norm_g, mu, w_in, w_out, w0, w1, w2, a0, a1, a2, k_k, k_a, r_k,
               lnx_w, lnx_b, vres):
    B, T, D = h.shape
    row = lambda x: x.reshape(1, -1)
    bf = lambda x: x.astype(BF16)
    seg = _head_indicator()
    common = [('tile', h), ('prev', h), ('batch', shift_prev.reshape(B, 1, D)), ('const', row(norm_g))]
    r, logd = _token_call(
        _rwkv_r_kernel, common + [('const', mu[jnp.array([0, 4])]), ('const', bf(w_in[0])), ('const', row(w0)),
                                  ('const', bf(w1)), ('const', bf(w2))], [INNER, INNER], "rwkv_r")
    k, kk, a = _token_call(
        _rwkv_k_kernel, common + [('const', mu[jnp.array([1, 5])]), ('const', bf(w_in[1])), ('const', row(a0)),
                                  ('const', bf(a1)), ('const', bf(a2)), ('const', row(k_k)), ('const', row(k_a)),
                                  ('const', seg)], [INNER] * 3, "rwkv_k")
    if vres is None:
        (v,) = _token_call(functools.partial(_rwkv_v_kernel, residual=False),
                           common + [('const', mu[2:3]), ('const', bf(w_in[2]))], [INNER], "rwkv_v")
        v_first = v
    else:
        v0, v1, v2 = vres
        (v,) = _token_call(functools.partial(_rwkv_v_kernel, residual=True),
                           common + [('const', mu[2:3]), ('const', bf(w_in[2])), ('tile', v_first), ('const', row(v0)),
                                     ('const', bf(v1)), ('const', bf(v2))], [INNER], "rwkv_v")
    (z,) = _token_call(_rwkv_z_kernel, common + [('const', mu[3:4]), ('const', bf(w_in[3]))], [INNER], "rwkv_z")
    y, s_T = wkv_scan(r, logd, k, v, kk, a, s0, chunk=WKV_CHUNK if T % WKV_CHUNK == 0 else T,
                      t_block=WKV_T_BLOCK if T % WKV_T_BLOCK == 0 else T)
    (h_new,) = _token_call(
        _rwkv_post_kernel,
        [('tile', y), ('tile', r), ('tile', k), ('tile', v), ('tile', z), ('tile', h), ('const', row(lnx_w)),
         ('const', row(lnx_b)), ('const', r_k.reshape(1, INNER)), ('const', seg), ('const', bf(w_out))],
        [D], "rwkv_post")
    return h_new, v_first, s_T


ROT_HALF = ROT_DIM // 2


def _rope_rows(x, cos, sin_lo, sin_hi):
    out = []
    for c in range(x.shape[1] // LANES):
        xc = x[:, c * LANES:(c + 1) * LANES]
        out.append(xc * cos + pltpu.roll(xc, LANES - ROT_HALF, 1) * sin_lo + pltpu.roll(xc, ROT_HALF, 1) * sin_hi)
    return jnp.concatenate(out, axis=1)


def _rope_cols(x, cos, sin):
    n = x.shape[0] // HEAD_DIM
    x = x.reshape(n, HEAD_DIM, x.shape[1])
    x1, x2 = x[:, :ROT_HALF], x[:, ROT_HALF:ROT_DIM]
    y = jnp.concatenate([x1 * cos - x2 * sin, x1 * sin + x2 * cos, x[:, ROT_DIM:]], axis=1)
    return y.reshape(n * HEAD_DIM, y.shape[2])


def _nsa_rows_kernel(h_ref, g_ref, w_ref, cos_ref, slo_ref, shi_ref, cmp_ref, sel_ref, win_ref, z_ref, ks_ref, kw_ref):
    xn = _rms(h_ref[0], g_ref[...])
    p = _bdot(xn, w_ref[...])
    cos, slo, shi = cos_ref[0], slo_ref[0], shi_ref[0]
    cmp_ref[0] = p[:, :2 * KV_W]
    ks = _rope_rows(p[:, 2 * KV_W:3 * KV_W], cos, slo, shi)
    kw = _rope_rows(p[:, 4 * KV_W:5 * KV_W], cos, slo, shi)
    sel_ref[0] = jnp.concatenate([ks, p[:, 3 * KV_W:4 * KV_W]], axis=1)
    win_ref[0] = jnp.concatenate([kw, p[:, 5 * KV_W:6 * KV_W]], axis=1)
    z_ref[0] = p[:, 6 * KV_W:]
    ks_ref[0] = ks.astype(BF16)
    kw_ref[0] = kw.astype(BF16)


def _nsa_cols_kernel(h_ref, g_ref, wT_ref, cos_ref, sin_ref, qT_ref, qrT_ref, vsT_ref, vwT_ref, gT_ref):
    xn = _rms(h_ref[0], g_ref[...]).astype(BF16)
    pT = lax.dot_general(wT_ref[...], xn, (((1,), (1,)), ((), ())), preferred_element_type=F32)
    q = pT[:INNER]
    qT_ref[0] = q.astype(BF16)
    qrT_ref[0] = (_rope_cols(q, cos_ref[...], sin_ref[...]) * LOG2_E).astype(BF16)
    vsT_ref[0] = pT[INNER:INNER + KV_W].astype(BF16)
    vwT_ref[0] = pT[INNER + KV_W:INNER + 2 * KV_W].astype(BF16)
    gT_ref[0] = _sigmoid(pT[INNER + 2 * KV_W:])


def _rope_tables(pos):
    inv = ROPE_THETA ** (-jnp.arange(ROT_HALF, dtype=F32) / ROT_HALF)
    ang = pos.astype(F32)[:, None] * inv[None, :]
    cos, sin = jnp.cos(ang), jnp.sin(ang)
    lane = jnp.arange(LANES) % HEAD_DIM
    f = lane % ROT_HALF
    cos_l = jnp.where(lane[None, :] < ROT_DIM, cos[:, f], 1.0)
    slo_l = jnp.where(lane[None, :] < ROT_HALF, -sin[:, f], 0.0)
    shi_l = jnp.where((lane[None, :] >= ROT_HALF) & (lane[None, :] < ROT_DIM), sin[:, f], 0.0)
    return cos_l, slo_l, shi_l, cos.T, sin.T


def nsa_pre(h, norm_g, w_in, pos):
    B, T, D = h.shape
    tm = TOKEN_TILE if T % TOKEN_TILE == 0 else T
    cos_l, slo_l, shi_l, cosT, sinT = _rope_tables(pos)
    scale = HEAD_DIM ** -0.5
    g_row = norm_g.reshape(1, D)
    w_rows = jnp.concatenate([w_in[:, Q_END:KV_END], w_in[:, G_END:]], axis=1).astype(BF16)
    w_g = w_in[:, KV_END:G_END].reshape(D, N_KV, HPG, 3).transpose(0, 1, 3, 2).reshape(D, 3 * N_HEADS)
    w_cols = jnp.concatenate([w_in[:, :Q_END] * scale, w_in[:, Q_END + 3 * KV_W:Q_END + 4 * KV_W],
                              w_in[:, Q_END + 5 * KV_W:Q_END + 6 * KV_W], w_g], axis=1).T.astype(BF16)
    tile = lambda w: pl.BlockSpec((1, tm, w), lambda b, t: (b, t, 0))
    whole = lambda a: pl.BlockSpec(a.shape, lambda b, t, n=a.ndim: (0,) * n)
    tab = pl.BlockSpec((1, tm, LANES), lambda b, t: (0, t, 0))
    params = pltpu.CompilerParams(dimension_semantics=("arbitrary", "arbitrary"), vmem_limit_bytes=VMEM_LIMIT_BYTES)
    cmp_rows, sel_rows, win_rows, z, ks, kw = pl.pallas_call(
        _nsa_rows_kernel,
        grid=(B, T // tm),
        in_specs=[tile(D), whole(g_row), whole(w_rows), tab, tab, tab],
        out_specs=[tile(2 * KV_W), tile(2 * KV_W), tile(2 * KV_W), tile(INNER), tile(KV_W), tile(KV_W)],
        out_shape=[jax.ShapeDtypeStruct((B, T, 2 * KV_W), F32)] * 3 + [jax.ShapeDtypeStruct((B, T, INNER), F32)]
        + [jax.ShapeDtypeStruct((B, T, KV_W), BF16)] * 2,
        compiler_params=params, name="nsa_rows",
    )(h, g_row, w_rows, cos_l[None], slo_l[None], shi_l[None])
    colt = lambda r: pl.BlockSpec((1, r, tm), lambda b, t: (b, 0, t))
    tabT = pl.BlockSpec((ROT_HALF, tm), lambda b, t: (0, t))
    qT, qrT, vsT, vwT, gT = pl.pallas_call(
        _nsa_cols_kernel,
        grid=(B, T // tm),
        in_specs=[tile(D), whole(g_row), whole(w_cols), tabT, tabT],
        out_specs=[colt(INNER), colt(INNER), colt(KV_W), colt(KV_W), colt(3 * N_HEADS)],
        out_shape=[jax.ShapeDtypeStruct((B, INNER, T), BF16)] * 2 + [jax.ShapeDtypeStruct((B, KV_W, T), BF16)] * 2
        + [jax.ShapeDtypeStruct((B, 3 * N_HEADS, T), F32)],
        compiler_params=params, name="nsa_cols",
    )(h, g_row, w_cols, cosT, sinT)
    return cmp_rows, sel_rows, win_rows, z, ks, kw, qT, qrT, vsT, vwT, gT


PAGE = 128
SAMPLE_VMEM_LIMIT_BYTES = 56 * 1024 * 1024


def _gather_pages(pt_ref, b, pool_hbm, buf, sem, n_pages):
    rows = pool_hbm.shape[1]
    copies = [pltpu.make_async_copy(pool_hbm.at[pt_ref[b, p]], buf.at[pl.ds(p * rows, rows)], sem)
              for p in range(n_pages)]
    for cp in copies:
        cp.start()
    for cp in copies:
        cp.wait()


def _sample_compress_kernel(pt_ref, pool_hbm, w1_hbm, pe_ref, w2_ref, out_ref, buf, w1_vmem, sem, wsem, *, n_pages):
    b = pl.program_id(0)

    @pl.when(b == 0)
    def _():
        cp = pltpu.make_async_copy(w1_hbm, w1_vmem, wsem)
        cp.start()
        cp.wait()

    _gather_pages(pt_ref, b, pool_hbm, buf, sem, n_pages)
    n_chunks = n_pages * PAGE // CMP_STRIDE
    row = lax.broadcasted_iota(jnp.int32, (n_chunks, 1), 0)
    for kv in range(2):
        lanes = slice(kv * KV_W, (kv + 1) * KV_W)
        parts = []
        for m in range(CMP_RATIO):
            acc = jnp.zeros((n_chunks, w1_vmem.shape[3]), F32)
            for s in range(CMP_STRIDE):
                at = s * 2 * KV_W + kv * KV_W
                x = buf[:, at:at + KV_W] + pe_ref[m * CMP_STRIDE + s:m * CMP_STRIDE + s + 1, :]
                acc = acc + jnp.dot(x.astype(BF16), w1_vmem[kv, m * CMP_STRIDE + s], preferred_element_type=F32)
            parts.append(acc)
        pre = parts[0] + pltpu.roll(parts[1], n_chunks - 1, 0)
        hid = pre * _sigmoid(pre)
        ck = jnp.dot(hid.astype(BF16), w2_ref[kv], preferred_element_type=F32)
        out_ref[0, :, lanes] = jnp.where(row < n_chunks - 1, ck, 0.0)


def _block_diag(w):
    eye = jnp.eye(N_KV, dtype=w.dtype)
    out = eye[:, None, :, None] * w[..., None, :, None, :]
    return out.reshape(w.shape[:-2] + (N_KV * w.shape[-2], N_KV * w.shape[-1]))


def sample_compress(pool, page_table, pe, w1, w2):
    B, n_pages = page_table.shape
    n_chunks = n_pages * PAGE // CMP_STRIDE
    pool2 = pool.reshape(pool.shape[0], PAGE // CMP_STRIDE, CMP_STRIDE * 2 * KV_W)
    w1_bd = _block_diag(w1).astype(BF16)
    w2_bd = _block_diag(w2).astype(BF16)
    pe_t = jnp.tile(pe, (1, N_KV))
    return pl.pallas_call(
        functools.partial(_sample_compress_kernel, n_pages=n_pages),
        grid_spec=pltpu.PrefetchScalarGridSpec(
            num_scalar_prefetch=1,
            grid=(B,),
            in_specs=[pl.BlockSpec(memory_space=pl.ANY), pl.BlockSpec(memory_space=pl.ANY),
                      pl.BlockSpec(pe_t.shape, lambda b, pt: (0, 0)),
                      pl.BlockSpec(w2_bd.shape, lambda b, pt: (0, 0, 0))],
            out_specs=pl.BlockSpec((1, n_chunks, 2 * KV_W), lambda b, pt: (b, 0, 0)),
            scratch_shapes=[pltpu.VMEM((n_chunks, CMP_STRIDE * 2 * KV_W), F32),
                            pltpu.VMEM(w1_bd.shape, BF16),
                            pltpu.SemaphoreType.DMA(()), pltpu.SemaphoreType.DMA(())]),
        out_shape=jax.ShapeDtypeStruct((B, n_chunks, 2 * KV_W), F32),
        compiler_params=pltpu.CompilerParams(
            dimension_semantics=("arbitrary",), vmem_limit_bytes=SAMPLE_VMEM_LIMIT_BYTES),
        name="sample_compress",
    )(page_table, pool2, w1_bd, pe_t, w2_bd)


SAMPLE_TILE = 512


def _online_step(s, v, m_ref, l_ref, acc_ref):
    m_old = m_ref[...]
    m_new = jnp.maximum(m_old, jnp.max(s, axis=0, keepdims=True))
    alpha = jnp.exp2(m_old - m_new)
    p = jnp.exp2(s - m_new)
    l_ref[...] = alpha * l_ref[...] + jnp.sum(p, axis=0, keepdims=True)
    acc_ref[...] = alpha * acc_ref[...] + lax.dot_general(
        v.astype(BF16), p.astype(BF16), (((0,), (0,)), ((), ())), preferred_element_type=F32)
    m_ref[...] = m_new


def _split_dot_left(m, x):
    hi = x.astype(BF16)
    lo = (x - hi.astype(F32)).astype(BF16)
    return jnp.dot(m, hi, preferred_element_type=F32) + jnp.dot(m, lo, preferred_element_type=F32)


def _sample_attend_kernel(pt_ref, pool_hbm, ckv_ref, qraw_ref, qrot_ref, g_ref, selnew_ref, win_ref, winnew_ref,
                          ovT_ref, fold_ref, o_ref, buf, sel_scr, m_scr, l_scr, acc_scr, sem,
                          *, n_pages, n_cmp, n_sel, past, t_new):
    b = pl.program_id(0)
    slot = b % 2

    def page_copies(seq, into):
        return [pltpu.make_async_copy(pool_hbm.at[pt_ref[seq, p]], buf.at[into, pl.ds(p * PAGE, PAGE)], sem.at[into])
                for p in range(n_pages)]

    @pl.when(b == 0)
    def _():
        for cp in page_copies(0, 0):
            cp.start()

    @pl.when(b + 1 < pl.num_programs(0))
    def _():
        for cp in page_copies(b + 1, 1 - slot):
            cp.start()

    for cp in page_copies(b, slot):
        cp.wait()
    width = qraw_ref.shape[2]
    q_idx = lax.broadcasted_iota(jnp.int32, (1, width), 1) % t_new
    qpos = past + q_idx

    n_rows = ckv_ref.shape[1]
    n_iota = lax.broadcasted_iota(jnp.int32, (n_rows, width), 0)
    c_valid = (n_iota < n_cmp) & ((n_iota * CMP_STRIDE + (CMP_BLOCK - 1)) <= qpos)
    s = jnp.dot(ckv_ref[0, :, :KV_W].astype(BF16), qraw_ref[0], preferred_element_type=F32)
    s = jnp.where(c_valid, s, NEG)
    mx = jnp.max(s, axis=0, keepdims=True)
    e = jnp.where(c_valid, jnp.exp(s - mx), 0.0)
    l = jnp.sum(e, axis=0, keepdims=True)
    p = e * (1.0 / jnp.where(l > 0.0, l, 1.0))
    o_ref[0] = g_ref[0, 0:1, :] * lax.dot_general(
        ckv_ref[0, :, KV_W:].astype(BF16), p.astype(BF16), (((0,), (0,)), ((), ())), preferred_element_type=F32)

    fold = fold_ref[...]
    p_sum = _split_dot(p, fold, ((1,), (0,)))
    imp = _split_dot_left(ovT_ref[...], p_sum)
    j_iota = lax.broadcasted_iota(jnp.int32, imp.shape, 0)
    cur = (past + lax.broadcasted_iota(jnp.int32, (1, imp.shape[1]), 1) % t_new) // SEL_BLOCK
    forced = (j_iota == 0) | (j_iota == cur) | (j_iota == cur - 1)
    score = jnp.where(j_iota <= cur, imp + jnp.where(forced, FORCE_BONUS, 0.0), NEG)
    sel = jnp.zeros(imp.shape, F32)
    for _ in range(TOP_N):
        mx = jnp.max(score, axis=0, keepdims=True)
        first = jnp.min(jnp.where(score == mx, j_iota, n_sel), axis=0, keepdims=True)
        pick = j_iota == first
        sel = jnp.where(pick & (mx > NEG / 2), 1.0, sel)
        score = jnp.where(pick, PICKED, score)
    sel_scr[...] = lax.dot_general(sel.astype(BF16), fold, (((1,), (1,)), ((), ())), preferred_element_type=F32)

    def reset():
        m_scr[...] = jnp.full(m_scr.shape, NEG, F32)
        l_scr[...] = jnp.zeros(l_scr.shape, F32)
        acc_scr[...] = jnp.zeros(acc_scr.shape, F32)

    qrot = qrot_ref[0]
    new_rows = lax.broadcasted_iota(jnp.int32, (t_new, width), 0)
    causal_new = new_rows <= q_idx

    reset()
    blocks_per_tile = SAMPLE_TILE // SEL_BLOCK
    for kt in range(n_pages * PAGE // SAMPLE_TILE):
        chosen = sel_scr[kt * blocks_per_tile:(kt + 1) * blocks_per_tile, :]
        chosen = jnp.concatenate(
            [jnp.broadcast_to(chosen[i:i + 1, :], (SEL_BLOCK, width)) for i in range(blocks_per_tile)], axis=0)
        rows = buf[slot, kt * SAMPLE_TILE:(kt + 1) * SAMPLE_TILE, :]
        s = jnp.dot(rows[:, :KV_W].astype(BF16), qrot, preferred_element_type=F32)
        _online_step(s + jnp.where(chosen > 0.5, 0.0, NEG), rows[:, KV_W:], m_scr, l_scr, acc_scr)
    last = past // SEL_BLOCK
    allowed = causal_new & (sel_scr[last:last + 1, :] > 0.5)
    s = jnp.dot(selnew_ref[0, :, :KV_W].astype(BF16), qrot, preferred_element_type=F32)
    _online_step(s + jnp.where(allowed, 0.0, NEG), selnew_ref[0, :, KV_W:], m_scr, l_scr, acc_scr)
    o_ref[0] = o_ref[0] + g_ref[0, 1:2, :] * acc_scr[...] * (1.0 / l_scr[...])

    reset()
    n_buf = win_ref.shape[1]
    w_pos = past - n_buf + lax.broadcasted_iota(jnp.int32, (n_buf, width), 0)
    allowed = (w_pos >= qpos - WINDOW) & (w_pos >= 0)
    s = jnp.dot(win_ref[0, :, :KV_W].astype(BF16), qrot, preferred_element_type=F32)
    _online_step(s + jnp.where(allowed, 0.0, NEG), win_ref[0, :, KV_W:], m_scr, l_scr, acc_scr)
    s = jnp.dot(winnew_ref[0, :, :KV_W].astype(BF16), qrot, preferred_element_type=F32)
    _online_step(s + jnp.where(causal_new, 0.0, NEG), winnew_ref[0, :, KV_W:], m_scr, l_scr, acc_scr)
    o_ref[0] = o_ref[0] + g_ref[0, 2:3, :] * acc_scr[...] * (1.0 / l_scr[...])


def sample_attend(pool, page_table, ckv, qT, qrT, gT, sel_new, win_buf, win_new):
    B, n_pages = page_table.shape
    t_new = sel_new.shape[1]
    past = n_pages * PAGE
    width = N_HEADS * t_new
    assert t_new < CMP_STRIDE and t_new <= SEL_BLOCK and past % SAMPLE_TILE == 0 and win_buf.shape[1] <= past
    assert t_new <= win_buf.shape[1] == min(WINDOW, past)
    n_cmp = past // CMP_STRIDE - CMP_RATIO + 1
    n_sel = -(-(past // SEL_BLOCK + 1) // SUBLANES) * SUBLANES

    def block_q(x):
        x = x.reshape(N_KV, HPG, HEAD_DIM, B, t_new).transpose(3, 0, 2, 1, 4)
        eye = jnp.eye(N_KV, dtype=x.dtype)
        x = x[:, :, :, None, :, :] * eye[None, :, None, :, None, None]
        return x.reshape(B, KV_W, width)

    gates = gT.reshape(N_KV, 3, HPG, B, t_new).transpose(3, 1, 0, 2, 4).reshape(B, 3, width)
    cs = jnp.arange(ckv.shape[1]) * CMP_STRIDE
    ss = jnp.arange(n_sel) * SEL_BLOCK
    ovT = (jnp.clip(jnp.minimum(cs[None, :] + CMP_BLOCK, ss[:, None] + SEL_BLOCK)
                    - jnp.maximum(cs[None, :], ss[:, None]), 0, None).astype(F32) / CMP_BLOCK)
    ovT = jnp.where(jnp.arange(ckv.shape[1])[None, :] < n_cmp, ovT, 0.0).astype(BF16)
    lane = jnp.arange(width)
    col = (lane // (HPG * t_new)) * t_new + lane % t_new
    fold = (col[:, None] == jnp.arange(LANES)[None, :]).astype(BF16)
    per_b = lambda a: pl.BlockSpec((1,) + a.shape[1:], lambda b, pt, n=a.ndim: (b,) + (0,) * (n - 1))
    whole = lambda a: pl.BlockSpec(a.shape, lambda b, pt, n=a.ndim: (0,) * n)
    pool2 = pool.reshape(pool.shape[0], PAGE, 2 * KV_W)
    qraw, qrot = block_q(qT), block_q(qrT)
    o_bd = pl.pallas_call(
        functools.partial(_sample_attend_kernel, n_pages=n_pages, n_cmp=n_cmp, n_sel=n_sel, past=past, t_new=t_new),
        grid_spec=pltpu.PrefetchScalarGridSpec(
            num_scalar_prefetch=1,
            grid=(B,),
            in_specs=[pl.BlockSpec(memory_space=pl.ANY), per_b(ckv), per_b(qraw), per_b(qrot), per_b(gates),
                      per_b(sel_new), per_b(win_buf), per_b(win_new), whole(ovT), whole(fold)],
            out_specs=pl.BlockSpec((1, KV_W, width), lambda b, pt: (b, 0, 0)),
            scratch_shapes=[pltpu.VMEM((2, past, 2 * KV_W), F32),
                            pltpu.VMEM((n_sel, width), F32),
                            pltpu.VMEM((1, width), F32), pltpu.VMEM((1, width), F32),
                            pltpu.VMEM((KV_W, width), F32),
                            pltpu.SemaphoreType.DMA((2,))]),
        out_shape=jax.ShapeDtypeStruct((B, KV_W, width), F32),
        compiler_params=pltpu.CompilerParams(dimension_semantics=("arbitrary",), vmem_limit_bytes=VMEM_LIMIT_BYTES),
        name="sample_attend",
    )(page_table, pool2, ckv, qraw, qrot, gates, sel_new, win_buf, win_new, ovT, fold)
    o = o_bd.reshape(B, N_KV, HEAD_DIM, N_KV, HPG, t_new)
    o = jnp.stack([o[:, g, :, g] for g in range(N_KV)], axis=1)
    return o.transpose(0, 4, 1, 3, 2).reshape(B, t_new, INNER)


CHUNK_TILE = 128


def _prompt_compress_kernel(x_ref, nxt_ref, w1_hbm, pe_ref, w2_ref, out_ref, w1_vmem, wsem, *, n_valid):
    t = pl.program_id(1)

    @pl.when((pl.program_id(0) == 0) & (t == 0))
    def _():
        cp = pltpu.make_async_copy(w1_hbm, w1_vmem, wsem)
        cp.start()
        cp.wait()

    rows = x_ref.shape[1]
    row = lax.broadcasted_iota(jnp.int32, (rows, 1), 0)
    for kv in range(2):
        lanes = slice(kv * KV_W, (kv + 1) * KV_W)
        pre = jnp.zeros((rows, w1_vmem.shape[3]), F32)
        for m in range(CMP_RATIO):
            for s in range(CMP_STRIDE):
                at = s * 2 * KV_W + kv * KV_W
                x = x_ref[0, :, at:at + KV_W]
                if m == 1:
                    x = jnp.where(row == rows - 1, nxt_ref[0, 0:1, at:at + KV_W], pltpu.roll(x, rows - 1, 0))
                x = x + pe_ref[m * CMP_STRIDE + s:m * CMP_STRIDE + s + 1, :]
                pre = pre + jnp.dot(x.astype(BF16), w1_vmem[kv, m * CMP_STRIDE + s], preferred_element_type=F32)
        hid = pre * _sigmoid(pre)
        ck = jnp.dot(hid.astype(BF16), w2_ref[kv], preferred_element_type=F32)
        out_ref[0, :, lanes] = jnp.where(t * rows + row < n_valid, ck, 0.0)


def prompt_compress(cmp_rows, pe, w1, w2):
    B, T, _ = cmp_rows.shape
    n_chunks = T // CMP_STRIDE
    rows = CHUNK_TILE if n_chunks % CHUNK_TILE == 0 else n_chunks
    x = cmp_rows.reshape(B, n_chunks, CMP_STRIDE * 2 * KV_W)
    w1_bd = _block_diag(w1).astype(BF16)
    w2_bd = _block_diag(w2).astype(BF16)
    pe_t = jnp.tile(pe, (1, N_KV))
    per_tile = rows // SUBLANES
    last = n_chunks // SUBLANES - 1
    return pl.pallas_call(
        functools.partial(_prompt_compress_kernel, n_valid=n_chunks - CMP_RATIO + 1),
        grid=(B, n_chunks // rows),
        in_specs=[pl.BlockSpec((1, rows, x.shape[2]), lambda b, t: (b, t, 0)),
                  pl.BlockSpec((1, SUBLANES, x.shape[2]), lambda b, t: (b, jnp.minimum((t + 1) * per_tile, last), 0)),
                  pl.BlockSpec(memory_space=pl.ANY),
                  pl.BlockSpec(pe_t.shape, lambda b, t: (0, 0)),
                  pl.BlockSpec(w2_bd.shape, lambda b, t: (0, 0, 0))],
        out_specs=pl.BlockSpec((1, rows, 2 * KV_W), lambda b, t: (b, t, 0)),
        out_shape=jax.ShapeDtypeStruct((B, n_chunks, 2 * KV_W), F32),
        scratch_shapes=[pltpu.VMEM(w1_bd.shape, BF16), pltpu.SemaphoreType.DMA(())],
        compiler_params=pltpu.CompilerParams(
            dimension_semantics=("arbitrary", "arbitrary"), vmem_limit_bytes=VMEM_LIMIT_BYTES),
        name="prompt_compress",
    )(x, x, w1_bd, pe_t, w2_bd)


def _residual_matmul_kernel(x_ref, h_ref, w_ref, o_ref):
    o_ref[0] = h_ref[0] + jnp.dot(x_ref[0], w_ref[...], preferred_element_type=F32)


def nsa_prompt(h, norm_g, w_in, w_out, pe, cw1, cw2):
    B, T, _ = h.shape
    cmp_rows, sel_rows, win_rows, z, ks, kw, qT, qrT, vsT, vwT, gT = nsa_pre(h, norm_g, w_in, jnp.arange(T))
    ckv = prompt_compress(cmp_rows, pe, cw1, cw2)
    nc = T // CMP_STRIDE - CMP_RATIO + 1
    blocks = lambda t: t[:, :nc].reshape(B, nc, N_KV, HEAD_DIM)
    gated = nsa_prompt_attend(qT, qrT, gT, z, blocks(ckv[..., :KV_W]), blocks(ckv[..., KV_W:]), ks, vsT, kw, vwT)
    (h_new,) = _token_call(_residual_matmul_kernel, [('tile', gated), ('tile', h), ('const', w_out.astype(BF16))],
                           [D_MODEL], "nsa_out")
    n_keep = min(WINDOW, T)
    return h_new, cmp_rows, sel_rows, win_rows[:, -n_keep:]


def nsa_sample(h, norm_g, pool_cmp, pool_sel, win_buf, page_table, w_in, w_out, pe, cw1, cw2):
    B, T, D = h.shape
    past = page_table.shape[1] * pool_cmp.shape[1]
    pos = past + jnp.arange(B * T) % T
    cmp_rows, sel_rows, win_rows, z, _, _, qT, qrT, _, _, gT = nsa_pre(h.reshape(1, B * T, D), norm_g, w_in, pos)
    rows = lambda t: t.reshape(B, T, 2 * KV_W)
    ckv = sample_compress(pool_cmp, page_table, pe, cw1, cw2)
    n_buf = win_buf.shape[1]
    win_flat = win_buf.reshape(B, n_buf, 2 * KV_W)
    o = sample_attend(pool_sel, page_table, ckv, qT[0], qrT[0], gT[0], rows(sel_rows), win_flat, rows(win_rows))
    gated = (o * jax.nn.silu(z.reshape(B, T, INNER))).astype(BF16)
    (h_new,) = _token_call(_residual_matmul_kernel, [('tile', gated), ('tile', h), ('const', w_out.astype(BF16))],
                           [D_MODEL], "nsa_out")
    n_keep = min(WINDOW, n_buf + T)
    win_out = jnp.concatenate([win_flat, rows(win_rows)], axis=1)[:, -n_keep:]
    return h_new, rows(cmp_rows), rows(sel_rows), win_out


def kernel(x_prompt, x_sample, state_rwkv_wkv, state_rwkv_shift, cache_nsa_cmp, cache_nsa_sel, cache_nsa_win, page_table, norm_g, final_norm_g, rwkv_mu, rwkv_w_in, rwkv_w_out, rwkv_w0, rwkv_w1, rwkv_w2, rwkv_a0, rwkv_a1, rwkv_a2, rwkv_v0, rwkv_v1, rwkv_v2, rwkv_k_k, rwkv_k_a, rwkv_r_k, rwkv_lnx_w, rwkv_lnx_b, nsa_w_in, nsa_w_out, nsa_cmp_pe, nsa_cmp_w1, nsa_cmp_w2):

    def trunk(x, wkv0, shift0, sample):
        h = x
        v_first = None
        wkv, shift, cmp_rows, sel_rows, win_rows = [], [], [], [], []
        for layer in range(DEPTH):
            j = layer // N_MIXERS
            if layer % N_MIXERS == 0:
                vres = None if j == 0 else (rwkv_v0[j - 1], rwkv_v1[j - 1], rwkv_v2[j - 1])
                shift.append(rms_norm(h[:, -1], norm_g[layer]))
                h, v_first, s_T = rwkv_layer(
                    h, shift0[j], wkv0[j], v_first, norm_g[layer], rwkv_mu[j], rwkv_w_in[j], rwkv_w_out[j],
                    rwkv_w0[j], rwkv_w1[j], rwkv_w2[j], rwkv_a0[j], rwkv_a1[j], rwkv_a2[j],
                    rwkv_k_k[j], rwkv_k_a[j], rwkv_r_k[j], rwkv_lnx_w[j], rwkv_lnx_b[j], vres)
                wkv.append(s_T)
            else:
                if sample:
                    h, c, s, w = nsa_sample(h, norm_g[layer], cache_nsa_cmp[j], cache_nsa_sel[j], cache_nsa_win[j],
                                            page_table, nsa_w_in[j], nsa_w_out[j], nsa_cmp_pe[j],
                                            nsa_cmp_w1[j], nsa_cmp_w2[j])
                else:
                    h, c, s, w = nsa_prompt(h, norm_g[layer], nsa_w_in[j], nsa_w_out[j], nsa_cmp_pe[j],
                                            nsa_cmp_w1[j], nsa_cmp_w2[j])
                cmp_rows.append(c)
                sel_rows.append(s)
                win_rows.append(w)
        def stacked(parts):
            t = jnp.stack(parts)
            return t.reshape(t.shape[:3] + (2, N_KV, HEAD_DIM))

        return (rms_norm(h, final_norm_g), jnp.stack(wkv), jnp.stack(shift),
                stacked(cmp_rows), stacked(sel_rows), stacked(win_rows))

    zeros_wkv = jnp.zeros((rwkv_mu.shape[0], x_prompt.shape[0], RW_HEADS, RW_HEAD, RW_HEAD), x_prompt.dtype)
    zeros_shift = jnp.zeros((rwkv_mu.shape[0], x_prompt.shape[0], D_MODEL), x_prompt.dtype)
    y_prompt, wkv_p, shift_p, cmp_p, sel_p, win_p = trunk(x_prompt, zeros_wkv, zeros_shift, False)
    y_sample, wkv_s, shift_s, cmp_s, sel_s, win_s = trunk(x_sample, state_rwkv_wkv, state_rwkv_shift, True)
    return (y_prompt, y_sample, wkv_p, wkv_s, shift_p, shift_s, cmp_p, cmp_s, sel_p, sel_s, win_p, win_s)
```
